```python
import math
import jax, jax.numpy as jnp
from jax import lax
import numpy as np

D_MODEL = 2048
BATCH = 1
SEQ = 16384
DEPTH = 2

CTX_LEN = 256
GRID_W = 64

A_HEADS = D_MODEL // 256
A_QK_DIM = 64
A_V_DIM = 2 * A_QK_DIM
A_Q_BLOCK = 128
ROPE_BASE = 10000.0
B_HEADS = D_MODEL // 128
B_HEAD_DIM = 64
B_WIDTH = B_HEADS * B_HEAD_DIM
B_DECAY_RANK = 64
B_ICL_RANK = 64
B_GATE_RANK = 160
RWKV_LN_EPS = 64e-5
C_HEADS = D_MODEL // 128
C_KEY_DIM = 128
C_VAL_DIM = D_MODEL // C_HEADS
C_WIDTH = C_HEADS * C_KEY_DIM
C_CHUNK = 64
N_EXPERTS = 64
TOP_K = 6
EXPERT_FF = 512
SHARED_FF = 1024
ROUTE_SCALE = 2.5
EXPERT_BLOCK = 256
DEEPNORM_ALPHA = (2 * DEPTH) ** 0.25
DEEPNORM_BETA = (8 * DEPTH) ** -0.25

N_EVEN = (DEPTH + 1) // 2
N_ODD = DEPTH // 2
A_SIZES = [A_HEADS * 2 * A_QK_DIM, A_HEADS * 2 * A_QK_DIM, A_HEADS * A_V_DIM]
B_SIZES = [B_WIDTH, B_WIDTH, B_WIDTH, 2 * B_DECAY_RANK, 2 * B_ICL_RANK, B_GATE_RANK]
A_IN = sum(A_SIZES)
B_IN = sum(B_SIZES)
EVEN_IN = A_IN + B_IN
EVEN_MIX = A_HEADS * A_V_DIM + B_WIDTH
ODD_MIX = C_HEADS * C_VAL_DIM
ODD_IN = 3 * C_WIDTH + 2 * ODD_MIX

kernel_name = "hybrid_diffattn_rwkv7_hgrn2_moe_dit"


def split_cols(z, sizes):
    return jnp.split(z, np.cumsum(sizes)[:-1].tolist(), axis=-1)


def to_heads(a, n_heads):
    return a.reshape(a.shape[:-1] + (n_heads, a.shape[-1] // n_heads))


def time_flip(a, direction):
    return jnp.flip(a, axis=1) if direction else a


def modulate(x, shift, scale):
    return x * (1.0 + scale) + shift


def layer_norm(x, g, b, eps=1e-5):
    xf = x.astype(jnp.float32)
    mu = jnp.mean(xf, -1, keepdims=True)
    var = jnp.mean(jnp.square(xf - mu), -1, keepdims=True)
    return ((xf - mu) * lax.rsqrt(var + eps) * g + b).astype(x.dtype)


def rms_norm(x, w, eps=1e-5):
    xf = x.astype(jnp.float32)
    return (xf * lax.rsqrt(jnp.mean(jnp.square(xf), -1, keepdims=True) + eps) * w).astype(x.dtype)


def head_group_norm(y, w, b, eps):
    mu = jnp.mean(y, -1, keepdims=True)
    var = jnp.mean(jnp.square(y - mu), -1, keepdims=True)
    return (y - mu) * lax.rsqrt(var + eps) * w + b


def axial_rope(x):
    n, hd = x.shape[1], x.shape[-1]
    rows = n // GRID_W
    row = jnp.repeat(jnp.arange(rows, dtype=jnp.float32), GRID_W)
    col = jnp.tile(jnp.arange(GRID_W, dtype=jnp.float32), rows)
    half = hd // 2
    inv = ROPE_BASE ** (-jnp.arange(0, half, 2, dtype=jnp.float32) / half)
    bshape = (1, n) + (1,) * (x.ndim - 3) + (half // 2,)

    def rot(xa, pos):
        ang = (pos[:, None] * inv).reshape(bshape)
        cos, sin = jnp.cos(ang).astype(x.dtype), jnp.sin(ang).astype(x.dtype)
        x1, x2 = jnp.split(xa, 2, axis=-1)
        return jnp.concatenate([x1 * cos - x2 * sin, x1 * sin + x2 * cos], axis=-1)

    xr, xc = jnp.split(x, 2, axis=-1)
    return jnp.concatenate([rot(xr, row), rot(xc, col)], axis=-1)


def diff_attention(q_lat, k_lat, v_lat, q_ctx, k_ctx, v_ctx, lam, subln_w, lam_init, with_ctx):
    bsz, n_lat = q_lat.shape[:2]
    q_lat, k_lat = axial_rope(q_lat), axial_rope(k_lat)
    k_all = jnp.concatenate([k_lat, k_ctx], axis=1)
    v_all = jnp.concatenate([v_lat, v_ctx], axis=1)

    def attend(q, k, v):
        s = jnp.einsum("bqhmd,bkhmd->bhmqk", q, k, preferred_element_type=jnp.float32) * (A_QK_DIM ** -0.5)
        p = jax.nn.softmax(s, axis=-1)
        w = (p[:, :, 0] - lam * p[:, :, 1]).astype(v.dtype)
        o = jnp.einsum("bhqk,bkhv->bqhv", w, v)
        return rms_norm(o, subln_w) * (1.0 - lam_init)

    n_blk = n_lat // A_Q_BLOCK
    qb = jnp.moveaxis(q_lat.reshape(bsz, n_blk, A_Q_BLOCK, A_HEADS, 2, A_QK_DIM), 1, 0)
    o_lat = lax.map(lambda qq: attend(qq, k_all, v_all), qb)
    o_lat = jnp.moveaxis(o_lat, 0, 1).reshape(bsz, n_lat, A_HEADS * A_V_DIM)
    o_ctx = attend(q_ctx, k_ctx, v_ctx).reshape(bsz, -1, A_HEADS * A_V_DIM) if with_ctx else None
    return o_lat, o_ctx


def centred_shift(z):
    zp = jnp.pad(z, ((0, 0), (1, 1), (0, 0)))
    return 0.5 * (zp[:, :-2] + zp[:, 2:])


def rwkv7_prepare(z, mu, w0, w2, a0, a2, g2, k_k, k_a):
    z = (z + (centred_shift(z) - z) * mu).astype(jnp.float32)
    r, k, v, wd, ad, gd = split_cols(z, B_SIZES)
    bsz, t = z.shape[:2]
    wd = wd.reshape(bsz, t, 2, B_DECAY_RANK)
    ad = ad.reshape(bsz, t, 2, B_ICL_RANK)
    w_log = -jax.nn.softplus(-(w0 + jnp.einsum("btdr,drc->btdc", jnp.tanh(wd), w2))) - 0.5
    decay = jnp.exp(-jnp.exp(w_log))
    icl = jax.nn.sigmoid(a0 + jnp.einsum("btdr,drc->btdc", ad, a2))
    g = jax.nn.sigmoid(gd) @ g2
    kk = to_heads(k * k_k, B_HEADS)
    kk = kk / jnp.maximum(jnp.sqrt(jnp.sum(kk * kk, -1, keepdims=True)), 1e-12)
    kk = kk.reshape(bsz, t, B_WIDTH)
    k_dir = k[:, :, None] * (1.0 + (icl - 1.0) * k_a)
    b_dir = kk[:, :, None] * icl
    return r, k_dir, v, decay, -kk, b_dir, g


def rwkv7_scan(s0, r, w, k, v, a, b):
    def step(s, xs):
        rt, wt, kt, vt, at, bt = xs
        sa = jnp.einsum("bhvk,bhk->bhv", s, at)
        s = s * wt[:, :, None, :] + sa[..., None] * bt[:, :, None, :] + vt[..., None] * kt[:, :, None, :]
        return s, jnp.einsum("bhvk,bhk->bhv", s, rt)
    s, y = lax.scan(step, s0, tuple(jnp.moveaxis(t, 1, 0) for t in (r, w, k, v, a, b)))
    return s, jnp.moveaxis(y, 0, 1)


def rwkv7_mixer(z_lat, z_ctx, mu, w0, w2, a0, a2, g2, k_k, k_a, r_k, lnx_w, lnx_b, with_ctx):
    lat = rwkv7_prepare(z_lat, mu, w0, w2, a0, a2, g2, k_k, k_a)
    ctx = rwkv7_prepare(z_ctx, mu, w0, w2, a0, a2, g2, k_k, k_a)
    bsz = z_lat.shape[0]
    s0 = jnp.zeros((bsz, B_HEADS, B_HEAD_DIM, B_HEAD_DIM), jnp.float32)

    def scan_inputs(p, d):
        r, k_dir, v, decay, a_vec, b_dir, _ = p
        return [time_flip(to_heads(t, B_HEADS), d)
                for t in (r, decay[:, :, d], k_dir[:, :, d], v, a_vec, b_dir[:, :, d])]

    y_lat, y_ctx = [], []
    for d in range(2):
        s_ctx, yc = rwkv7_scan(s0, *scan_inputs(ctx, d))
        _, yl = rwkv7_scan(s_ctx, *scan_inputs(lat, d))
        y_lat.append(time_flip(yl, d))
        y_ctx.append(time_flip(yc, d))

    def finish(p, y):
        r, k_dir, v, _, _, _, g = p
        rh, vh, kh = to_heads(r, B_HEADS), to_heads(v, B_HEADS), to_heads(k_dir, B_HEADS)
        bonus = jnp.sum(jnp.sum(rh[:, :, None] * kh * r_k, -1, keepdims=True) * vh[:, :, None], axis=2)
        yn = head_group_norm(y, to_heads(lnx_w, B_HEADS), to_heads(lnx_b, B_HEADS), RWKV_LN_EPS)
        return (yn + bonus).reshape(g.shape) * g

    out_lat = finish(lat, y_lat[0] + y_lat[1]).astype(z_lat.dtype)
    out_ctx = finish(ctx, y_ctx[0] + y_ctx[1]).astype(z_ctx.dtype) if with_ctx else None
    return out_lat, out_ctx


def even_mixer(h_lat, h_ctx, w_in, w_out, lam_params, subln_w, mu, w0, w2, a0, a2, g2, k_k, k_a, r_k,
               lnx_w, lnx_b, layer_idx, with_ctx):
    z_lat, z_ctx = h_lat @ w_in, h_ctx @ w_in

    def attn_parts(z):
        q, k, v = split_cols(z[..., :A_IN], A_SIZES)
        b, t = z.shape[:2]
        return (q.reshape(b, t, A_HEADS, 2, A_QK_DIM), k.reshape(b, t, A_HEADS, 2, A_QK_DIM),
                v.reshape(b, t, A_HEADS, A_V_DIM))

    lam_f = lam_params.astype(jnp.float32)
    lam_init = 0.8 - 0.6 * math.exp(-0.3 * layer_idx)
    lam = jnp.exp(jnp.sum(lam_f[0] * lam_f[1])) - jnp.exp(jnp.sum(lam_f[2] * lam_f[3])) + lam_init
    a_lat, a_ctx = diff_attention(*attn_parts(z_lat), *attn_parts(z_ctx), lam, subln_w, lam_init, with_ctx)
    b_lat, b_ctx = rwkv7_mixer(z_lat[..., A_IN:], z_ctx[..., A_IN:], mu, w0, w2, a0, a2, g2, k_k, k_a, r_k,
                               lnx_w, lnx_b, with_ctx)
    out_lat = jnp.concatenate([a_lat, b_lat], axis=-1) @ w_out
    out_ctx = jnp.concatenate([a_ctx, b_ctx], axis=-1) @ w_out if with_ctx else None
    return out_lat, out_ctx


def hgrn2_chunk_scan(s0, q, log_f, k, v):
    bsz, t, h, _ = q.shape
    n_chunk = t // C_CHUNK

    def chunks(a):
        return jnp.moveaxis(a.reshape(bsz, n_chunk, C_CHUNK, h, a.shape[-1]), 1, 0)

    incl = jnp.tril(jnp.ones((C_CHUNK, C_CHUNK), bool))

    def step(s, xs):
        qc, gc, kc, vc = xs
        cum = jnp.cumsum(gc, axis=1)
        decay = jnp.exp(jnp.where(incl[None, :, :, None, None], cum[:, :, None] - cum[:, None, :], -jnp.inf))
        attn = jnp.einsum("bthk,bshk,btshk->bhts", qc, kc, decay)
        o = jnp.einsum("bhts,bshv->bthv", attn, vc) + jnp.einsum("bthk,bhkv->bthv", qc * jnp.exp(cum), s)
        last = cum[:, -1]
        s = jnp.exp(last)[..., None] * s + jnp.einsum("bshk,bshv->bhkv", kc * jnp.exp(last[:, None] - cum), vc)
        return s, o

    s, o = lax.scan(step, s0, tuple(chunks(a) for a in (q, log_f, k, v)))
    return s, jnp.moveaxis(o, 0, 1).reshape(bsz, t, h, v.shape[-1])


def hgrn2_mixer(z_lat, z_ctx, lb, norm_w, with_ctx):
    log_lb, log_rest = jnp.log(lb), jnp.log1p(-lb)

    def prepare(z):
        q, f_fwd, f_bwd, i, g = split_cols(z.astype(jnp.float32), [C_WIDTH] * 3 + [ODD_MIX] * 2)

        def gate(f):
            f = to_heads(f, C_HEADS)
            log_f = jnp.logaddexp(log_lb, log_rest + jax.nn.log_sigmoid(f))
            return log_f, (1.0 - lb) * jax.nn.sigmoid(-f)

        return to_heads(jax.nn.silu(q), C_HEADS), (gate(f_fwd), gate(f_bwd)), to_heads(i, C_HEADS), g

    lat, ctx = prepare(z_lat), prepare(z_ctx)
    bsz = z_lat.shape[0]
    s0 = jnp.zeros((bsz, C_HEADS, C_KEY_DIM, C_VAL_DIM), jnp.float32)
    o_lat, o_ctx = [], []
    for d in range(2):
        lf_c, k_c = ctx[1][d]
        s_ctx, oc = hgrn2_chunk_scan(s0, time_flip(ctx[0], d), time_flip(lf_c, d), time_flip(k_c, d),
                                     time_flip(ctx[2], d))
        lf_l, k_l = lat[1][d]
        _, ol = hgrn2_chunk_scan(s_ctx, time_flip(lat[0], d), time_flip(lf_l, d), time_flip(k_l, d),
                                 time_flip(lat[2], d))
        o_lat.append(time_flip(ol, d))
        o_ctx.append(time_flip(oc, d))

    def finish(o, g):
        o = rms_norm(o, norm_w)
        return o.reshape(g.shape) * jax.nn.silu(g)

    out_lat = finish(o_lat[0] + o_lat[1], lat[3]).astype(z_lat.dtype)
    out_ctx = finish(o_ctx[0] + o_ctx[1], ctx[3]).astype(z_ctx.dtype) if with_ctx else None
    return out_lat, out_ctx


def odd_mixer(h_lat, h_ctx, w_in, w_out, lb, norm_w, with_ctx):
    c_lat, c_ctx = hgrn2_mixer(h_lat @ w_in, h_ctx @ w_in, lb, norm_w, with_ctx)
    return c_lat @ w_out, (c_ctx @ w_out if with_ctx else None)


def moe_ffn(h, w_router, b_router, w_gu, w_down, ws_gu, ws_down):
    n, d = h.shape
    scores = jax.nn.sigmoid(jnp.dot(h, w_router, preferred_element_type=jnp.float32))
    _, top_idx = lax.top_k(scores + b_router.astype(jnp.float32), TOP_K)
    top_s = jnp.take_along_axis(scores, top_idx, axis=-1)
    gates = top_s / jnp.sum(top_s, -1, keepdims=True) * ROUTE_SCALE
    n_assign = n * TOP_K
    expert_of = top_idx.reshape(-1)
    order = jnp.argsort(expert_of, stable=True)
    exp_sorted = expert_of[order]
    tok_sorted = (order // TOP_K).astype(jnp.int32)
    gate_sorted = gates.reshape(-1)[order]
    counts = jnp.bincount(expert_of, length=N_EXPERTS)
    padded = (counts + EXPERT_BLOCK - 1) // EXPERT_BLOCK * EXPERT_BLOCK
    pad_end = jnp.cumsum(padded)
    pad_start = pad_end - padded
    raw_start = jnp.cumsum(counts) - counts
    dest = pad_start[exp_sorted] + jnp.arange(n_assign) - raw_start[exp_sorted]
    n_blocks = -(-(n_assign + N_EXPERTS * (EXPERT_BLOCK - 1)) // EXPERT_BLOCK)
    n_slots = n_blocks * EXPERT_BLOCK
    block_expert = jnp.minimum(
        jnp.searchsorted(pad_end, jnp.arange(n_blocks) * EXPERT_BLOCK, side="right"), N_EXPERTS - 1)
    slot_token = jnp.zeros((n_slots,), jnp.int32).at[dest].set(tok_sorted)
    slot_gate = jnp.zeros((n_slots,), jnp.float32).at[dest].set(gate_sorted)

    def expert_block(args):
        tok, gate, e = args
        a, b = jnp.split(h[tok] @ w_gu[e], 2, axis=-1)
        return ((jax.nn.silu(a) * b) @ w_down[e]) * gate[:, None].astype(h.dtype)

    y = lax.map(expert_block, (slot_token.reshape(n_blocks, EXPERT_BLOCK),
                               slot_gate.reshape(n_blocks, EXPERT_BLOCK), block_expert))
    routed = jax.ops.segment_sum(y.reshape(n_slots, d), slot_token, num_segments=n)
    a, b = jnp.split(h @ ws_gu, 2, axis=-1)
    return routed + (jax.nn.silu(a) * b) @ ws_down


def setup_inputs(seed: int = 0) -> dict:
    key = jax.random.key(seed)
    keys = iter(jax.random.split(key, 48))

    def nrm(shape, scale):
        return jax.random.normal(next(keys), shape, jnp.float32) * scale

    def uni(shape, lo, hi):
        return jax.random.uniform(next(keys), shape, jnp.float32, lo, hi)

    d = D_MODEL
    return {
        "x": nrm((BATCH, SEQ, d), 1.0),
        "c": nrm((BATCH, d), 1.0),
        "ctx": nrm((BATCH, CTX_LEN, d), 1.0),
        "c_ctx": nrm((d,), 1.0),
        "ada_w": nrm((DEPTH, d, 6 * d), d ** -0.5),
        "ada_b": nrm((DEPTH, 6 * d), 0.01),
        "ln_g": 1.0 + nrm((DEPTH, 2, d), 0.02),
        "ln_b": nrm((DEPTH, 2, d), 0.02),
        "even_w_in": nrm((N_EVEN, d, EVEN_IN), d ** -0.5),
        "even_w_out": nrm((N_EVEN, EVEN_MIX, d), EVEN_MIX ** -0.5 * DEEPNORM_BETA),
        "diff_lambda": nrm((N_EVEN, 4, A_QK_DIM), 0.1),
        "diff_subln_w": 1.0 + nrm((N_EVEN, A_V_DIM), 0.02),
        "rwkv_mu": uni((N_EVEN, B_IN), 0.2, 0.8),
        "rwkv_w0": uni((N_EVEN, 2, B_WIDTH), -6.0, -1.0),
        "rwkv_w2": nrm((N_EVEN, 2, B_DECAY_RANK, B_WIDTH), B_DECAY_RANK ** -0.5),
        "rwkv_a0": nrm((N_EVEN, 2, B_WIDTH), 0.1),
        "rwkv_a2": nrm((N_EVEN, 2, B_ICL_RANK, B_WIDTH), B_ICL_RANK ** -0.5),
        "rwkv_g2": nrm((N_EVEN, B_GATE_RANK, B_WIDTH), B_GATE_RANK ** -0.5),
        "rwkv_k_k": 0.85 + nrm((N_EVEN, B_WIDTH), 0.02),
        "rwkv_k_a": 1.0 + nrm((N_EVEN, B_WIDTH), 0.02),
        "rwkv_r_k": nrm((N_EVEN, B_HEADS, B_HEAD_DIM), 0.1),
        "rwkv_lnx_w": 1.0 + nrm((N_EVEN, B_WIDTH), 0.02),
        "rwkv_lnx_b": nrm((N_EVEN, B_WIDTH), 0.02),
        "odd_w_in": nrm((N_ODD, d, ODD_IN), d ** -0.5),
        "odd_w_out": nrm((N_ODD, ODD_MIX, d), ODD_MIX ** -0.5 * DEEPNORM_BETA),
        "hgrn_lb_raw": nrm((DEPTH, C_WIDTH), 0.1),
        "hgrn_norm_w": 1.0 + nrm((N_ODD, C_VAL_DIM), 0.02),
        "moe_router_w": nrm((DEPTH, d, N_EXPERTS), d ** -0.5),
        "moe_router_b": nrm((DEPTH, N_EXPERTS), 0.01),
        "moe_w_gu": nrm((DEPTH, N_EXPERTS, d, 2 * EXPERT_FF), d ** -0.5),
        "moe_w_down": nrm((DEPTH, N_EXPERTS, EXPERT_FF, d), EXPERT_FF ** -0.5 * DEEPNORM_BETA),
        "moe_shared_gu": nrm((DEPTH, d, 2 * SHARED_FF), d ** -0.5),
        "moe_shared_down": nrm((DEPTH, SHARED_FF, d), SHARED_FF ** -0.5 * DEEPNORM_BETA),
    }


def reference(x, c, ctx, c_ctx, ada_w, ada_b, ln_g, ln_b, even_w_in, even_w_out, diff_lambda, diff_subln_w,
              rwkv_mu, rwkv_w0, rwkv_w2, rwkv_a0, rwkv_a2, rwkv_g2, rwkv_k_k, rwkv_k_a, rwkv_r_k,
              rwkv_lnx_w, rwkv_lnx_b, odd_w_in, odd_w_out, hgrn_lb_raw, hgrn_norm_w,
              moe_router_w, moe_router_b, moe_w_gu, moe_w_down, moe_shared_gu, moe_shared_down):
    bsz, n_lat, _ = x.shape
    n_ctx = ctx.shape[1]
    lb_soft = jax.nn.softmax(hgrn_lb_raw.astype(jnp.float32), axis=0)
    lower_bounds = jnp.cumsum(lb_soft, axis=0) - lb_soft[0]
    x_lat, x_ctx = x, ctx
    for l in range(DEPTH):
        with_ctx = l < DEPTH - 1
        mod_lat = jnp.split((jax.nn.silu(c) @ ada_w[l] + ada_b[l])[:, None, :], 6, axis=-1)
        mod_ctx = jnp.split(jax.nn.silu(c_ctx) @ ada_w[l] + ada_b[l], 6, axis=-1)
        h_lat = modulate(x_lat, mod_lat[0], mod_lat[1])
        h_ctx = modulate(x_ctx, mod_ctx[0], mod_ctx[1])
        if l % 2 == 0:
            e = l // 2
            m_lat, m_ctx = even_mixer(h_lat, h_ctx, even_w_in[e], even_w_out[e], diff_lambda[e], diff_subln_w[e],
                                      rwkv_mu[e], rwkv_w0[e], rwkv_w2[e], rwkv_a0[e], rwkv_a2[e], rwkv_g2[e],
                                      rwkv_k_k[e], rwkv_k_a[e], rwkv_r_k[e], rwkv_lnx_w[e], rwkv_lnx_b[e],
                                      l, with_ctx)
        else:
            o = l // 2
            m_lat, m_ctx = odd_mixer(h_lat, h_ctx, odd_w_in[o], odd_w_out[o],
                                     lower_bounds[l].reshape(C_HEADS, C_KEY_DIM), hgrn_norm_w[o], with_ctx)
        x_lat = layer_norm(DEEPNORM_ALPHA * x_lat + mod_lat[2] * m_lat, ln_g[l, 0], ln_b[l, 0])
        if with_ctx:
            x_ctx = layer_norm(DEEPNORM_ALPHA * x_ctx + mod_ctx[2] * m_ctx, ln_g[l, 0], ln_b[l, 0])
            f_in = jnp.concatenate([modulate(x_ctx, mod_ctx[3], mod_ctx[4]),
                                    modulate(x_lat, mod_lat[3], mod_lat[4])], axis=1)
            f_out = moe_ffn(f_in.reshape(-1, D_MODEL), moe_router_w[l], moe_router_b[l], moe_w_gu[l],
                            moe_w_down[l], moe_shared_gu[l], moe_shared_down[l]).reshape(bsz, n_ctx + n_lat, D_MODEL)
            f_ctx, f_lat = f_out[:, :n_ctx], f_out[:, n_ctx:]
            x_ctx = layer_norm(DEEPNORM_ALPHA * x_ctx + mod_ctx[5] * f_ctx, ln_g[l, 1], ln_b[l, 1])
        else:
            f_in = modulate(x_lat, mod_lat[3], mod_lat[4])
            f_lat = moe_ffn(f_in.reshape(-1, D_MODEL), moe_router_w[l], moe_router_b[l], moe_w_gu[l],
                            moe_w_down[l], moe_shared_gu[l], moe_shared_down[l]).reshape(bsz, n_lat, D_MODEL)
        x_lat = layer_norm(DEEPNORM_ALPHA * x_lat + mod_lat[5] * f_lat, ln_g[l, 1], ln_b[l, 1])
    return x_lat
```

```python
import functools
import math

import numpy as np
import jax
import jax.numpy as jnp
from jax import lax
from jax.experimental import pallas as pl
from jax.experimental.pallas import tpu as pltpu

F32 = jnp.float32
BF16 = jnp.bfloat16
HIGHEST = lax.Precision.HIGHEST

GRID_W = 64
A_HEADS = 8
A_QK_DIM = 64
A_V_DIM = 128
ROPE_BASE = 10000.0
B_HEADS = 16
B_HEAD_DIM = 64
B_WIDTH = 1024
B_DECAY_RANK = 64
B_ICL_RANK = 64
B_GATE_RANK = 160
RWKV_LN_EPS = 64e-5
C_HEADS = 16
C_KEY_DIM = 128
N_EXPERTS = 64
TOP_K = 6
EXPERT_FF = 512
SHARED_FF = 1024
ROUTE_SCALE = 2.5
DEPTH = 2
DEEPNORM_ALPHA = (2 * DEPTH) ** 0.25

ROW_TILE = 256
RWKV_CHUNK = 64
EXPERT_BLOCK = 256
LANE = 128
VMEM_LIMIT = 56 * 1024 * 1024


def _cparams(sem):
    return pltpu.CompilerParams(dimension_semantics=sem, vmem_limit_bytes=VMEM_LIMIT)


def _mm(a, b):
    return jnp.dot(a.astype(BF16), b.astype(BF16), preferred_element_type=F32)


def _mm_nt(a, b):
    return lax.dot_general(a.astype(BF16), b.astype(BF16), (((1,), (1,)), ((), ())),
                           preferred_element_type=F32)


def _mm_tn(a, b):
    return lax.dot_general(a.astype(BF16), b.astype(BF16), (((0,), (0,)), ((), ())),
                           preferred_element_type=F32)


def _split3(x):
    hi = x.astype(BF16)
    r1 = x - hi.astype(F32)
    mid = r1.astype(BF16)
    lo = (r1 - mid.astype(F32)).astype(BF16)
    return hi, mid, lo


def _mm_exact_lhs(m_bf16, x):
    hi, mid, lo = _split3(x)
    d = lambda t: jnp.dot(m_bf16, t, preferred_element_type=F32)
    return d(hi) + d(mid) + d(lo)


def _mm_exact_rhs(x, m_bf16):
    hi, mid, lo = _split3(x)
    d = lambda t: jnp.dot(t, m_bf16, preferred_element_type=F32)
    return d(hi) + d(mid) + d(lo)


def _silu(x):
    return x * jax.nn.sigmoid(x)


def _softplus(x):
    return jnp.maximum(x, 0.0) + jnp.log1p(jnp.exp(-jnp.abs(x)))


def _row_tile(m, cands=(1280, 640, 512, 256)):
    for t in cands:
        if m % t == 0:
            return t
    raise ValueError(f"no row tile for {m}")


def _adaln_kernel(c_ref, w_ref, b_ref, o_ref):
    s = _silu(c_ref[...])
    o_ref[...] = _mm(s, w_ref[...]) + b_ref[...]


def adaln(c8, ada_w, ada_b):
    depth, d, n6 = ada_w.shape
    tn = 512
    return pl.pallas_call(
        _adaln_kernel,
        grid=(depth, n6 // tn),
        in_specs=[pl.BlockSpec((8, d), lambda l, j: (0, 0)),
                  pl.BlockSpec((None, d, tn), lambda l, j: (l, 0, j)),
                  pl.BlockSpec((None, 1, tn), lambda l, j: (l, 0, j))],
        out_specs=pl.BlockSpec((None, 8, tn), lambda l, j: (l, 0, j)),
        out_shape=jax.ShapeDtypeStruct((depth, 8, n6), F32),
        compiler_params=_cparams(("parallel", "parallel")),
        name="adaln",
    )(c8, ada_w, ada_b.reshape(depth, 1, n6))


def _mod_rows(mod_ref, i, n_ctx_tiles, idx, d):
    row = jnp.where(i < n_ctx_tiles, 1, 0)
    return mod_ref[pl.ds(row, 1), idx * d:(idx + 1) * d]


def _modulate_kernel(x_ref, mod_ref, o_ref, *, n_ctx_tiles, d):
    i = pl.program_id(0)
    sh = _mod_rows(mod_ref, i, n_ctx_tiles, 0, d)
    sc = _mod_rows(mod_ref, i, n_ctx_tiles, 1, d)
    o_ref[...] = (x_ref[...] * (1.0 + sc) + sh).astype(o_ref.dtype)


def modulate(x, mod, n_ctx):
    n, d = x.shape
    return pl.pallas_call(
        functools.partial(_modulate_kernel, n_ctx_tiles=n_ctx // ROW_TILE, d=d),
        grid=(n // ROW_TILE,),
        in_specs=[pl.BlockSpec((ROW_TILE, d), lambda i: (i, 0)),
                  pl.BlockSpec(mod.shape, lambda i: (0, 0))],
        out_specs=pl.BlockSpec((ROW_TILE, d), lambda i: (i, 0)),
        out_shape=jax.ShapeDtypeStruct((n, d), BF16),
        compiler_params=_cparams(("parallel",)),
        name="modulate",
    )(x, mod)


def _mm_kernel(x_ref, w_ref, o_ref):
    o_ref[...] = jnp.dot(x_ref[...], w_ref[...], preferred_element_type=F32).astype(o_ref.dtype)


def matmul(x, w, out_dtype, tn=512):
    m, k = x.shape
    n = w.shape[1]
    tm = _row_tile(m)
    return pl.pallas_call(
        _mm_kernel,
        grid=(m // tm, n // tn),
        in_specs=[pl.BlockSpec((tm, k), lambda i, j: (i, 0)),
                  pl.BlockSpec((k, tn), lambda i, j: (0, j))],
        out_specs=pl.BlockSpec((tm, tn), lambda i, j: (i, j)),
        out_shape=jax.ShapeDtypeStruct((m, n), out_dtype),
        compiler_params=_cparams(("parallel", "parallel")),
        name="matmul",
    )(x, w)


def _mm_rope_kernel(x_ref, w_ref, cos_ref, sin_ref, o_ref, *, tn):
    acc = jnp.dot(x_ref[...], w_ref[...], preferred_element_type=F32)
    cos = cos_ref[...]
    sin = sin_ref[...]
    lane = lax.broadcasted_iota(jnp.int32, cos.shape, 1)
    first = ((lane // 16) % 2) == 0
    for j in range(tn // LANE):
        blk = acc[:, j * LANE:(j + 1) * LANE]
        partner = jnp.where(first, pltpu.roll(blk, LANE - 16, 1), pltpu.roll(blk, 16, 1))
        o_ref[:, j * LANE:(j + 1) * LANE] = (blk * cos + partner * sin).astype(o_ref.dtype)


def matmul_rope(x, w, cos_t, sin_t, tn=512):
    m, k = x.shape
    n = w.shape[1]
    tm = _row_tile(m)
    return pl.pallas_call(
        functools.partial(_mm_rope_kernel, tn=tn),
        grid=(m // tm, n // tn),
        in_specs=[pl.BlockSpec((tm, k), lambda i, j: (i, 0)),
                  pl.BlockSpec((k, tn), lambda i, j: (0, j)),
                  pl.BlockSpec((tm, LANE), lambda i, j: (i, 0)),
                  pl.BlockSpec((tm, LANE), lambda i, j: (i, 0))],
        out_specs=pl.BlockSpec((tm, tn), lambda i, j: (i, j)),
        out_shape=jax.ShapeDtypeStruct((m, n), BF16),
        compiler_params=_cparams(("parallel", "parallel")),
        name="matmul_rope",
    )(x, w, cos_t, sin_t)


def rope_tables(n_ctx, n_lat):
    t = jnp.arange(n_lat)
    row = (t // GRID_W).astype(F32)
    col = (t % GRID_W).astype(F32)
    half = A_QK_DIM // 2
    inv = ROPE_BASE ** (-jnp.arange(0, half, 2, dtype=F32) / half)
    ang_r = row[:, None] * inv
    ang_c = col[:, None] * inv
    cos64 = jnp.concatenate([jnp.cos(ang_r), jnp.cos(ang_r), jnp.cos(ang_c), jnp.cos(ang_c)], axis=-1)
    sin64 = jnp.concatenate([-jnp.sin(ang_r), jnp.sin(ang_r), -jnp.sin(ang_c), jnp.sin(ang_c)], axis=-1)
    cos_l = jnp.concatenate([cos64, cos64], axis=-1)
    sin_l = jnp.concatenate([sin64, sin64], axis=-1)
    cos_t = jnp.concatenate([jnp.ones((n_ctx, LANE), F32), cos_l], axis=0)
    sin_t = jnp.concatenate([jnp.zeros((n_ctx, LANE), F32), sin_l], axis=0)
    return cos_t, sin_t


def _attn_kernel(lam_ref, q_ref, k_ref, v_ref, w_ref, o_ref, q0_s, q1_s, m_s, l_s, acc_s, *, nk, out_scale):
    ki = pl.program_id(2)

    @pl.when(ki == 0)
    def _():
        q = q_ref[...]
        lane = lax.broadcasted_iota(jnp.int32, q.shape, 1)
        zero = jnp.zeros_like(q)
        q0_s[...] = jnp.where(lane < A_QK_DIM, q, zero)
        q1_s[...] = jnp.where(lane >= A_QK_DIM, q, zero)
        m_s[...] = jnp.full(m_s.shape, -jnp.inf, F32)
        l_s[...] = jnp.zeros(l_s.shape, F32)
        acc_s[...] = jnp.zeros(acc_s.shape, F32)

    k = k_ref[...]
    v = v_ref[...]
    for mi, qs in enumerate((q0_s, q1_s)):
        s = lax.dot_general(qs[...], k, (((1,), (1,)), ((), ())), preferred_element_type=F32)
        m_old = m_s[mi]
        m_new = jnp.maximum(m_old, jnp.max(s, axis=1, keepdims=True))
        p = jnp.exp(s - m_new[:, :1])
        alpha = jnp.exp(m_old - m_new)
        l_s[mi] = alpha * l_s[mi] + jnp.sum(p, axis=1, keepdims=True)
        acc_s[mi] = alpha * acc_s[mi] + jnp.dot(p.astype(BF16), v, preferred_element_type=F32)
        m_s[mi] = m_new

    @pl.when(ki == nk - 1)
    def _():
        lam = lam_ref[0]
        o = acc_s[0] / l_s[0] - lam * (acc_s[1] / l_s[1])
        ms = jnp.mean(o * o, axis=-1, keepdims=True)
        o_ref[...] = (o * lax.rsqrt(ms + 1e-5) * w_ref[...] * out_scale).astype(o_ref.dtype)


def diff_attention(q, k, v, lam, subln_w, lam_init, tq, tk):
    nq = q.shape[0]
    nkv = k.shape[0]
    nk = nkv // tk
    return pl.pallas_call(
        functools.partial(_attn_kernel, nk=nk, out_scale=1.0 - lam_init),
        grid=(A_HEADS, nq // tq, nk),
        in_specs=[pl.BlockSpec(memory_space=pltpu.SMEM),
                  pl.BlockSpec((tq, LANE), lambda h, i, j: (i, h)),
                  pl.BlockSpec((tk, LANE), lambda h, i, j: (j, h)),
                  pl.BlockSpec((tk, LANE), lambda h, i, j: (j, h)),
                  pl.BlockSpec((1, LANE), lambda h, i, j: (0, 0))],
        out_specs=pl.BlockSpec((tq, LANE), lambda h, i, j: (i, h)),
        out_shape=jax.ShapeDtypeStruct((nq, A_HEADS * A_V_DIM), BF16),
        scratch_shapes=[pltpu.VMEM((tq, LANE), BF16), pltpu.VMEM((tq, LANE), BF16),
                        pltpu.VMEM((2, tq, LANE), F32), pltpu.VMEM((2, tq, LANE), F32),
                        pltpu.VMEM((2, tq, LANE), F32)],
        compiler_params=_cparams(("parallel", "parallel", "arbitrary")),
        name="diff_attention",
    )(lam.reshape(1), q, k, v, subln_w.reshape(1, LANE))


def _rwkv_prep_kernel(z_ref, zp_ref, zn_ref, mu_ref, w2_ref, w0_ref, a2_ref, a0_ref, g2_ref, kk_ref, ka_ref,
                      eh_ref, r_o, v_o, kkn_o, g_o, kd_o, lw_o, bd_o, *, n_ctx, n_tot):
    i = pl.program_id(0)
    z = z_ref[...]
    tm = z.shape[0]
    grow = i * tm + lax.broadcasted_iota(jnp.int32, (tm, 1), 0)
    lrow = lax.broadcasted_iota(jnp.int32, (tm, 1), 0)
    prev = jnp.where(lrow == 0, zp_ref[7:8, :], pltpu.roll(z, 1, 0))
    nxt = jnp.where(lrow == tm - 1, zn_ref[0:1, :], pltpu.roll(z, tm - 1, 0))
    has_prev = jnp.logical_and(grow != 0, grow != n_ctx)
    has_next = jnp.logical_and(grow != n_ctx - 1, grow != n_tot - 1)
    prev = jnp.where(has_prev, prev, 0.0)
    nxt = jnp.where(has_next, nxt, 0.0)
    zm = z + (0.5 * (prev + nxt) - z) * mu_ref[...]
    c = B_WIDTH
    r = zm[:, 0:c]
    k = zm[:, c:2 * c]
    v = zm[:, 2 * c:3 * c]
    wd = zm[:, 3 * c:3 * c + 128]
    ad = zm[:, 3 * c + 128:3 * c + 256]
    gd = zm[:, 3 * c + 256:3 * c + 512]
    wl = w0_ref[...] + jnp.dot(jnp.tanh(wd), w2_ref[...], precision=HIGHEST, preferred_element_type=F32)
    al = a0_ref[...] + jnp.dot(ad, a2_ref[...], precision=HIGHEST, preferred_element_type=F32)
    g = _mm(jax.nn.sigmoid(gd), g2_ref[...])
    kk = k * kk_ref[...]
    ss = _mm_exact_rhs(kk * kk, eh_ref[...])
    kkn = kk / jnp.maximum(jnp.sqrt(ss), 1e-12)
    r_o[...] = r
    v_o[...] = v
    kkn_o[...] = kkn
    g_o[...] = g
    for d in range(2):
        w_log = -_softplus(-wl[:, d * c:(d + 1) * c]) - 0.5
        lw_o[d] = -jnp.exp(w_log)
        icl = jax.nn.sigmoid(al[:, d * c:(d + 1) * c])
        kd_o[d] = k * (1.0 + (icl - 1.0) * ka_ref[...])
        bd_o[d] = kkn * icl


def rwkv_prep(zb, mu, w2cat, w0cat, a2cat, a0cat, g2pad, k_k, k_a, ehead, n_ctx):
    n, wz = zb.shape
    tm = ROW_TILE
    nt = n // tm
    c = B_WIDTH
    full = lambda a: pl.BlockSpec(a.shape, lambda i: (0,) * a.ndim)
    tok = pl.BlockSpec((tm, c), lambda i: (i, 0))
    tok2 = pl.BlockSpec((2, tm, c), lambda i: (0, i, 0))
    sd = jax.ShapeDtypeStruct
    return pl.pallas_call(
        functools.partial(_rwkv_prep_kernel, n_ctx=n_ctx, n_tot=n),
        grid=(nt,),
        in_specs=[pl.BlockSpec((tm, wz), lambda i: (i, 0)),
                  pl.BlockSpec((8, wz), lambda i: (jnp.maximum(i * (tm // 8) - 1, 0), 0)),
                  pl.BlockSpec((8, wz), lambda i: (jnp.minimum((i + 1) * (tm // 8), n // 8 - 1), 0)),
                  full(mu), full(w2cat), full(w0cat), full(a2cat), full(a0cat), full(g2pad),
                  full(k_k), full(k_a), full(ehead)],
        out_specs=[tok, tok, tok, tok, tok2, tok2, tok2],
        out_shape=[sd((n, c), F32)] * 4 + [sd((2, n, c), F32)] * 3,
        compiler_params=_cparams(("parallel",)),
        name="rwkv_prep",
    )(zb, zb, zb, mu, w2cat, w0cat, a2cat, a0cat, g2pad, k_k, k_a, ehead)


def _rwkv_chunk_kernel(r_ref, v_ref, kk_ref, kd_ref, lw_ref, bd_ref, mincl_ref, mbef_ref, msame_ref,
                       gh_ref, qy_ref):
    lw = lw_ref[...]
    mincl = mincl_ref[...]
    cum_i = _mm_exact_lhs(mincl, lw)
    cum_e = cum_i - lw
    cum_l = _mm_exact_lhs(msame_ref[...], lw)
    r = r_ref[...]
    v = v_ref[...]
    a = -kk_ref[...]
    k = kd_ref[...]
    b = bd_ref[...]
    at = a * jnp.exp(cum_e)
    rt = r * jnp.exp(cum_i)
    einv = jnp.exp(-cum_i)
    bt = b * einv
    kt = k * einv
    e_l = jnp.exp(cum_l - cum_i)
    bh = b * e_l
    kh = k * e_l
    p_l = jnp.exp(cum_l)
    incl = mincl > 0.5
    bef = mbef_ref[...] > 0.5
    hd = B_HEAD_DIM
    nch = ROW_TILE // RWKV_CHUNK
    zeros_h = jnp.zeros((ROW_TILE, hd), F32)
    eye = (lax.broadcasted_iota(jnp.int32, (RWKV_CHUNK, LANE), 0)
           == lax.broadcasted_iota(jnp.int32, (RWKV_CHUNK, LANE), 1))
    ri = lax.broadcasted_iota(jnp.int32, (ROW_TILE, ROW_TILE), 0)
    ci = lax.broadcasted_iota(jnp.int32, (ROW_TILE, ROW_TILE), 1)
    xr = ri ^ ci
    eye_t = jnp.where(ri == ci, 1.0, 0.0)
    for h in range(LANE // hd):
        sl = slice(h * hd, (h + 1) * hd)
        a_ab = jnp.where(bef, _mm_nt(at[:, sl], bt[:, sl]), 0.0)
        a_ak = jnp.where(bef, _mm_nt(at[:, sl], kt[:, sl]), 0.0)
        m_rb = jnp.where(incl, _mm_nt(rt[:, sl], bt[:, sl]), 0.0)
        m_rk = jnp.where(incl, _mm_nt(rt[:, sl], kt[:, sl]), 0.0)
        vh = v[:, sl]
        t_inv = eye_t + jnp.where(xr == 1, a_ab, 0.0)
        for sh in range(1, 6):
            a_lev = jnp.where((xr >> sh) == 1, a_ab, 0.0)
            t_inv = t_inv + _mm(t_inv, _mm(a_lev, t_inv))
        x = _mm(t_inv, jnp.concatenate([at[:, sl], _mm(a_ak, vh)], axis=1))
        v0 = jnp.concatenate([zeros_h, vh], axis=1)
        qy_ref[h] = jnp.concatenate([rt[:, sl], zeros_h], axis=1) + _mm(m_rb, x) + _mm(m_rk, v0)
        for c in range(nch):
            rows = slice(c * RWKV_CHUNK, (c + 1) * RWKV_CHUNK)
            gh = _mm_tn(bh[rows, sl], x[rows]) + _mm_tn(kh[rows, sl], v0[rows])
            plr = jnp.concatenate([p_l[c * RWKV_CHUNK:c * RWKV_CHUNK + 1, sl], jnp.zeros((1, hd), F32)], axis=1)
            gh_ref[h, rows, :] = gh + jnp.where(eye, jnp.broadcast_to(plr, (RWKV_CHUNK, LANE)), 0.0)


def rwkv_chunk(r, v, kkn, kd, lw, bd, mincl, mbef, msame):
    n, c = r.shape
    tm = ROW_TILE
    tokb = pl.BlockSpec((tm, LANE), lambda d, s, p: (s, p))
    tokd = pl.BlockSpec((None, tm, LANE), lambda d, s, p: (d, s, p))
    mdir = pl.BlockSpec((None, tm, tm), lambda d, s, p: (d, 0, 0))
    outb = pl.BlockSpec((None, 2, tm, LANE), lambda d, s, p: (d, p, s, 0))
    sd = jax.ShapeDtypeStruct((2, B_HEADS, n, LANE), F32)
    return pl.pallas_call(
        _rwkv_chunk_kernel,
        grid=(2, n // tm, B_HEADS // 2),
        in_specs=[tokb, tokb, tokb, tokd, tokd, tokd, mdir, mdir,
                  pl.BlockSpec((tm, tm), lambda d, s, p: (0, 0))],
        out_specs=[outb, outb],
        out_shape=[sd, sd],
        compiler_params=_cparams(("parallel", "parallel", "parallel")),
        name="rwkv_chunk",
    )(r, v, kkn, kd, lw, bd, mincl, mbef, msame)


def _scan_tile(d, j, nt, n_ctx_tiles):
    back = jnp.where(j < n_ctx_tiles, n_ctx_tiles - 1 - j, nt - 1 + n_ctx_tiles - j)
    return jnp.where(d == 0, j, back)


def _rwkv_seq_kernel(gh_ref, qy_ref, y_ref, z_s):
    d = pl.program_id(0)
    j = pl.program_id(1)

    @pl.when(j == 0)
    def _():
        z_s[...] = jnp.zeros(z_s.shape, F32)

    hd = B_HEAD_DIM
    nch = ROW_TILE // RWKV_CHUNK
    for ci in range(nch):
        c = jnp.where(d == 0, ci, nch - 1 - ci)
        off = pl.multiple_of(c * RWKV_CHUNK, RWKV_CHUNK)
        ys = []
        for h in range(B_HEADS):
            z = z_s[h]
            gh = gh_ref[h, pl.ds(off, RWKV_CHUNK), :]
            qy = qy_ref[h, pl.ds(off, RWKV_CHUNK), :]
            ys.append(_mm(qy[:, :hd], z) + qy[:, hd:])
            z_s[h] = _mm(gh[:, :hd], z) + gh[:, hd:]
        y_ref[pl.ds(off, RWKV_CHUNK), :] = jnp.concatenate(ys, axis=1)


def rwkv_seq(gh, qy, n_ctx):
    _, hh, n, _ = gh.shape
    tm = ROW_TILE
    nt = n // tm
    nct = n_ctx // tm
    inb = pl.BlockSpec((None, hh, tm, LANE), lambda d, j: (d, 0, _scan_tile(d, j, nt, nct), 0))
    return pl.pallas_call(
        _rwkv_seq_kernel,
        grid=(2, nt),
        in_specs=[inb, inb],
        out_specs=pl.BlockSpec((None, tm, B_WIDTH), lambda d, j: (d, _scan_tile(d, j, nt, nct), 0)),
        out_shape=jax.ShapeDtypeStruct((2, n, B_WIDTH), F32),
        scratch_shapes=[pltpu.VMEM((hh, B_HEAD_DIM, B_HEAD_DIM), F32)],
        compiler_params=_cparams(("parallel", "arbitrary")),
        name="rwkv_seq",
    )(gh, qy)


def _rwkv_finish_kernel(y_ref, r_ref, v_ref, kd_ref, g_ref, rk_ref, lw_ref, lb_ref, eh_ref, o_ref):
    y = y_ref[0] + y_ref[1]
    eh = eh_ref[...]
    inv = 1.0 / B_HEAD_DIM
    mu = _mm_exact_rhs(y, eh) * inv
    yc = y - mu
    var = _mm_exact_rhs(yc * yc, eh) * inv
    yn = yc * lax.rsqrt(var + RWKV_LN_EPS) * lw_ref[...] + lb_ref[...]
    r = r_ref[...]
    rk = (r * kd_ref[0] + r * kd_ref[1]) * rk_ref[...]
    bonus = _mm_exact_rhs(rk, eh) * v_ref[...]
    o_ref[...] = ((yn + bonus) * g_ref[...]).astype(o_ref.dtype)


def rwkv_finish(y, r, v, kd, g, r_k, lnx_w, lnx_b, ehead):
    n, c = r.shape
    tm = ROW_TILE
    tok = pl.BlockSpec((tm, c), lambda i: (i, 0))
    tok2 = pl.BlockSpec((2, tm, c), lambda i: (0, i, 0))
    full = lambda a: pl.BlockSpec(a.shape, lambda i: (0,) * a.ndim)
    return pl.pallas_call(
        _rwkv_finish_kernel,
        grid=(n // tm,),
        in_specs=[tok2, tok, tok, tok2, tok, full(r_k), full(lnx_w), full(lnx_b), full(ehead)],
        out_specs=tok,
        out_shape=jax.ShapeDtypeStruct((n, c), BF16),
        compiler_params=_cparams(("parallel",)),
        name="rwkv_finish",
    )(y, r, v, kd, g, r_k, lnx_w, lnx_b, ehead)


def _layer_norm(x, g, b):
    mu = jnp.mean(x, axis=-1, keepdims=True)
    xc = x - mu
    var = jnp.mean(xc * xc, axis=-1, keepdims=True)
    return xc * lax.rsqrt(var + 1e-5) * g + b


def _outproj_kernel(a_ref, w_ref, x_ref, mod_ref, g_ref, b_ref, x1_ref, f_ref, *, n_ctx_tiles, d):
    i = pl.program_id(0)
    m = jnp.dot(a_ref[...], w_ref[...], preferred_element_type=F32)
    gate = _mod_rows(mod_ref, i, n_ctx_tiles, 2, d)
    x1 = _layer_norm(DEEPNORM_ALPHA * x_ref[...] + gate * m, g_ref[...], b_ref[...])
    x1_ref[...] = x1
    sh = _mod_rows(mod_ref, i, n_ctx_tiles, 3, d)
    sc = _mod_rows(mod_ref, i, n_ctx_tiles, 4, d)
    f_ref[...] = x1 * (1.0 + sc) + sh


def outproj_ln(a, w_out, x, mod, ln_g, ln_b, n_ctx):
    n, kdim = a.shape
    d = x.shape[1]
    tm = ROW_TILE
    tok = pl.BlockSpec((tm, d), lambda i: (i, 0))
    vec = pl.BlockSpec((1, d), lambda i: (0, 0))
    return pl.pallas_call(
        functools.partial(_outproj_kernel, n_ctx_tiles=n_ctx // tm, d=d),
        grid=(n // tm,),
        in_specs=[pl.BlockSpec((tm, kdim), lambda i: (i, 0)),
                  pl.BlockSpec((kdim, d), lambda i: (0, 0)),
                  tok, pl.BlockSpec(mod.shape, lambda i: (0, 0)), vec, vec],
        out_specs=[tok, tok],
        out_shape=[jax.ShapeDtypeStruct((n, d), F32)] * 2,
        compiler_params=_cparams(("parallel",)),
        name="outproj_ln",
    )(a, w_out, x, mod, ln_g.reshape(1, d), ln_b.reshape(1, d))


def _router_kernel(f_ref, wr_hi_ref, wr_lo_ref, br_ref, wsg_ref, wsd_ref, gates_ref, idx_ref, sh_ref):
    f = f_ref[...]
    f_hi = f.astype(BF16)
    f_lo = (f - f_hi.astype(F32)).astype(BF16)
    dd = lambda a, b: jnp.dot(a, b[...], preferred_element_type=F32)
    logits = dd(f_hi, wr_hi_ref) + dd(f_lo, wr_hi_ref) + dd(f_hi, wr_lo_ref)
    scores = jax.nn.sigmoid(logits)
    lane = lax.broadcasted_iota(jnp.int32, scores.shape, 1).astype(F32)
    neg = jnp.float32(-jnp.inf)
    sel = jnp.where(lane < N_EXPERTS, scores + br_ref[...], neg)
    top = jnp.zeros(scores.shape, F32)
    idx_out = jnp.zeros(scores.shape, F32)
    for j in range(TOP_K):
        mx = jnp.max(sel, axis=1, keepdims=True)
        first = jnp.min(jnp.where(sel == mx, lane, float(LANE)), axis=1, keepdims=True)
        hit = lane == first
        top = jnp.where(hit, scores, top)
        sel = jnp.where(hit, neg, sel)
        idx_out = jnp.where(lane == j, first, idx_out)
    gates_ref[...] = top / jnp.sum(top, axis=1, keepdims=True) * ROUTE_SCALE
    idx_ref[...] = idx_out.astype(jnp.int32)
    ab = dd(f_hi, wsg_ref)
    hdn = _silu(ab[:, :SHARED_FF]) * ab[:, SHARED_FF:]
    sh_ref[...] = dd(hdn.astype(BF16), wsd_ref)


def router_shared(f_in, wr_hi, wr_lo, br, ws_gu, ws_down):
    n, d = f_in.shape
    tm = ROW_TILE
    full = lambda a: pl.BlockSpec(a.shape, lambda i: (0,) * a.ndim)
    sd = jax.ShapeDtypeStruct
    return pl.pallas_call(
        _router_kernel,
        grid=(n // tm,),
        in_specs=[pl.BlockSpec((tm, d), lambda i: (i, 0)), full(wr_hi), full(wr_lo), full(br),
                  full(ws_gu), full(ws_down)],
        out_specs=[pl.BlockSpec((tm, LANE), lambda i: (i, 0)), pl.BlockSpec((tm, LANE), lambda i: (i, 0)),
                   pl.BlockSpec((tm, d), lambda i: (i, 0))],
        out_shape=[sd((n, LANE), F32), sd((n, LANE), jnp.int32), sd((n, d), F32)],
        compiler_params=_cparams(("parallel",)),
        name="router_shared",
    )(f_in, wr_hi, wr_lo, br, ws_gu, ws_down)


def _expert_kernel(be_ref, cnt_ref, tok_ref, dst_ref, f_hbm, gate_ref, wgu_ref, wdn_ref, yt_hbm,
                   xbuf, ybuf, sem_in, sem_out, *, n_blocks):
    i = pl.program_id(0)
    s = i % 2
    eb = EXPERT_BLOCK

    def gather_copy(row, slot, r):
        return pltpu.make_async_copy(f_hbm.at[pl.ds(row, 1)], xbuf.at[slot, pl.ds(r, 1)], sem_in.at[slot])

    def start_gather(blk, slot):
        def body(r, carry):
            gather_copy(tok_ref[blk * eb + r], slot, r).start()
            return carry
        lax.fori_loop(0, eb, body, 0)

    def wait_scatter(slot, n):
        def wait_rows(rows):
            def body(r, carry):
                pltpu.make_async_copy(ybuf.at[slot, pl.ds(0, rows)], yt_hbm.at[pl.ds(0, rows)],
                                      sem_out.at[slot]).wait()
                return carry
            return body
        lax.fori_loop(0, n // 8, wait_rows(8), 0)
        lax.fori_loop(0, n % 8, wait_rows(1), 0)

    @pl.when(jnp.logical_and(i == 0, cnt_ref[0] > 0))
    def _():
        start_gather(0, 0)

    nxt = jnp.minimum(i + 1, n_blocks - 1)

    @pl.when(jnp.logical_and(i + 1 < n_blocks, cnt_ref[nxt] > 0))
    def _():
        start_gather(i + 1, 1 - s)

    prev2 = jnp.maximum(i - 2, 0)

    @pl.when(jnp.logical_and(i >= 2, cnt_ref[prev2] > 0))
    def _():
        wait_scatter(s, cnt_ref[prev2])

    n_valid = cnt_ref[i]

    @pl.when(n_valid > 0)
    def _():
        pltpu.make_async_copy(f_hbm.at[pl.ds(0, eb)], xbuf.at[s], sem_in.at[s]).wait()
        x = xbuf[s].astype(BF16)
        ab = jnp.dot(x, wgu_ref[...].astype(BF16), preferred_element_type=F32)
        hdn = _silu(ab[:, :EXPERT_FF]) * ab[:, EXPERT_FF:]
        y = jnp.dot(hdn.astype(BF16), wdn_ref[...].astype(BF16), preferred_element_type=F32)
        ybuf[s] = y * gate_ref[...]

        def body(r, carry):
            pltpu.make_async_copy(ybuf.at[s, pl.ds(r, 1)], yt_hbm.at[pl.ds(dst_ref[i * eb + r], 1)],
                                  sem_out.at[s]).start()
            return carry
        lax.fori_loop(0, n_valid, body, 0)

    @pl.when(i == n_blocks - 1)
    def _():
        @pl.when(n_valid > 0)
        def _():
            wait_scatter(s, n_valid)
        prev1 = jnp.maximum(i - 1, 0)

        @pl.when(jnp.logical_and(i >= 1, cnt_ref[prev1] > 0))
        def _():
            wait_scatter(1 - s, cnt_ref[prev1])


def routed_experts(f_in, block_expert, block_cnt, slot_token, slot_dest, slot_gate, w_gu, w_down):
    n, d = f_in.shape
    n_blocks = block_expert.shape[0]
    eb = EXPERT_BLOCK
    grid_spec = pltpu.PrefetchScalarGridSpec(
        num_scalar_prefetch=4,
        grid=(n_blocks,),
        in_specs=[pl.BlockSpec(memory_space=pl.ANY),
                  pl.BlockSpec((None, eb, 1), lambda i, be, cnt, tok, dst: (i, 0, 0)),
                  pl.BlockSpec((None, d, 2 * EXPERT_FF), lambda i, be, cnt, tok, dst: (be[i], 0, 0)),
                  pl.BlockSpec((None, EXPERT_FF, d), lambda i, be, cnt, tok, dst: (be[i], 0, 0))],
        out_specs=pl.BlockSpec(memory_space=pl.ANY),
        scratch_shapes=[pltpu.VMEM((2, eb, d), F32), pltpu.VMEM((2, eb, d), F32),
                        pltpu.SemaphoreType.DMA((2,)), pltpu.SemaphoreType.DMA((2,))],
    )
    return pl.pallas_call(
        functools.partial(_expert_kernel, n_blocks=n_blocks),
        grid_spec=grid_spec,
        out_shape=jax.ShapeDtypeStruct((n * TOP_K, d), F32),
        compiler_params=_cparams(("arbitrary",)),
        name="routed_experts",
    )(block_expert, block_cnt, slot_token, slot_dest, f_in, slot_gate.reshape(n_blocks, eb, 1), w_gu, w_down)


def moe_dispatch(gates, idx, n_blocks):
    n = gates.shape[0]
    eb = EXPERT_BLOCK
    g = gates[:, :N_EXPERTS]
    idx6 = idx[:, :TOP_K]
    onehot = jnp.zeros((n, N_EXPERTS), jnp.int32).at[jnp.arange(n)[:, None], idx6].set(1)
    cnt_incl = jnp.cumsum(onehot, axis=0)
    rank = cnt_incl - onehot
    counts = cnt_incl[-1]
    padded = (counts + eb - 1) // eb * eb
    pad_end = jnp.cumsum(padded)
    pad_start = pad_end - padded
    slot_of = pad_start[None, :] + rank
    tok_slots = jnp.take_along_axis(slot_of, idx6, axis=1)
    tok_gates = jnp.take_along_axis(g, idx6, axis=1)
    n_slots = n_blocks * eb
    flat_slots = tok_slots.reshape(-1)
    tok_ids = jnp.repeat(jnp.arange(n, dtype=jnp.int32), TOP_K)
    slot_token = jnp.zeros((n_slots,), jnp.int32).at[flat_slots].set(tok_ids)
    slot_dest = jnp.zeros((n_slots,), jnp.int32).at[flat_slots].set(jnp.arange(n * TOP_K, dtype=jnp.int32))
    slot_gate = jnp.zeros((n_slots,), F32).at[flat_slots].set(tok_gates.reshape(-1))
    blk_start = jnp.arange(n_blocks, dtype=jnp.int32) * eb
    block_expert = jnp.minimum(jnp.searchsorted(pad_end, blk_start, side="right"), N_EXPERTS - 1).astype(jnp.int32)
    used_end = pad_start[block_expert] + counts[block_expert]
    block_cnt = jnp.clip(used_end - blk_start, 0, eb).astype(jnp.int32)
    block_cnt = jnp.where(blk_start < pad_end[-1], block_cnt, 0)
    return block_expert, block_cnt, slot_token, slot_dest, slot_gate


def _combine_kernel(yt_ref, sh_ref, x1_ref, mod_ref, g_ref, b_ref, *rest, n_ctx_tiles, d, with_next):
    i = pl.program_id(0)
    f = sh_ref[...]
    for j in range(TOP_K):
        f = f + yt_ref[:, j * d:(j + 1) * d]
    gate = _mod_rows(mod_ref, i, n_ctx_tiles, 5, d)
    x2 = _layer_norm(DEEPNORM_ALPHA * x1_ref[...] + gate * f, g_ref[...], b_ref[...])
    if with_next:
        modn_ref, x2_ref, h_ref = rest
        x2_ref[...] = x2
        sh = _mod_rows(modn_ref, i, n_ctx_tiles, 0, d)
        sc = _mod_rows(modn_ref, i, n_ctx_tiles, 1, d)
        h_ref[...] = (x2 * (1.0 + sc) + sh).astype(h_ref.dtype)
    else:
        (x2_ref,) = rest
        x2_ref[...] = x2


def moe_combine(yt, shared, x1, mod, ln_g, ln_b, n_ctx, mod_next=None):
    n, d = x1.shape
    tm = ROW_TILE
    tok = pl.BlockSpec((tm, d), lambda i: (i, 0))
    vec = pl.BlockSpec((1, d), lambda i: (0, 0))
    with_next = mod_next is not None
    in_specs = [pl.BlockSpec((tm, TOP_K * d), lambda i: (i, 0)), tok, tok,
                pl.BlockSpec(mod.shape, lambda i: (0, 0)), vec, vec]
    args = [yt.reshape(n, TOP_K * d), shared, x1, mod, ln_g.reshape(1, d), ln_b.reshape(1, d)]
    out_specs = [tok]
    out_shape = [jax.ShapeDtypeStruct((n, d), F32)]
    if with_next:
        in_specs.append(pl.BlockSpec(mod_next.shape, lambda i: (0, 0)))
        args.append(mod_next)
        out_specs.append(tok)
        out_shape.append(jax.ShapeDtypeStruct((n, d), BF16))
    return pl.pallas_call(
        functools.partial(_combine_kernel, n_ctx_tiles=n_ctx // tm, d=d, with_next=with_next),
        grid=(n // tm,),
        in_specs=in_specs, out_specs=out_specs, out_shape=out_shape,
        compiler_params=_cparams(("parallel",)),
        name="moe_combine",
    )(*args)


def moe_layer(f_in, x1, mod, ln_g, ln_b, w_router, b_router, w_gu, w_down, ws_gu, ws_down, n_ctx, mod_next=None):
    n, d = f_in.shape
    wr = jnp.pad(w_router, ((0, 0), (0, LANE - N_EXPERTS)))
    wr_hi = wr.astype(BF16)
    wr_lo = (wr - wr_hi.astype(F32)).astype(BF16)
    br = jnp.pad(b_router.astype(F32), (0, LANE - N_EXPERTS)).reshape(1, LANE)
    gates, idx, shared = router_shared(f_in, wr_hi, wr_lo, br, ws_gu.astype(BF16), ws_down.astype(BF16))
    n_blocks = -(-(n * TOP_K + N_EXPERTS * (EXPERT_BLOCK - 1)) // EXPERT_BLOCK)
    be, cnt, slot_token, slot_dest, slot_gate = moe_dispatch(gates, idx, n_blocks)
    yt = routed_experts(f_in, be, cnt, slot_token, slot_dest, slot_gate, w_gu, w_down)
    return moe_combine(yt, shared, x1, mod, ln_g, ln_b, n_ctx, mod_next)


def _hgrn_levels():
    t = ROW_TILE
    levels = []
    b = t // 2
    while b >= 1:
        levels.append(b)
        b //= 2
    return levels


def _hgrn_kernel(q_ref, f_ref, i_ref, lb_ref, mlev_ref, mincl_ref, o_ref, st_s):
    d = pl.program_id(0)
    j = pl.program_id(2)

    @pl.when(j == 0)
    def _():
        st_s[...] = jnp.zeros(st_s.shape, F32)

    t = ROW_TILE
    lb = lb_ref[...]
    log_lb = jnp.log(lb)
    log_rest = jnp.log1p(-lb)
    fz = f_ref[...]
    q = _silu(q_ref[...])
    v = i_ref[...]
    log_sig = jnp.minimum(fz, 0.0) - jnp.log1p(jnp.exp(-jnp.abs(fz)))
    u = log_rest + log_sig
    mx = jnp.maximum(log_lb, u)
    logf = mx + jnp.log1p(jnp.exp(-jnp.abs(log_lb - u)))
    key = (1.0 - lb) * jax.nn.sigmoid(-fz)

    cum = _mm_exact_lhs(mincl_ref[...], logf)
    cum_l = jnp.where(d == 0, cum[t - 1:t, :], cum[0:1, :])
    d_all = _mm_exact_lhs(mlev_ref[...], logf)

    row = lax.broadcasted_iota(jnp.int32, (t, 1), 0)
    ri = lax.broadcasted_iota(jnp.int32, (t, t), 0)
    ci = lax.broadcasted_iota(jnp.int32, (t, t), 1)
    a = jnp.zeros((t, t), F32)
    for li, b in enumerate(_hgrn_levels()):
        e = jnp.exp(d_all[li * t:(li + 1) * t])
        q_side = ((row // b) % 2) == jnp.where(d == 0, 1, 0)
        qt = jnp.where(q_side, q * e, 0.0)
        kt = jnp.where(q_side, 0.0, key * e)
        a = a + jnp.where((ri // (2 * b)) == (ci // (2 * b)), _mm_nt(qt, kt), 0.0)
    st = st_s[...]
    o = _mm(a, v) + _mm_nt(q * jnp.exp(cum), st) + jnp.sum(q * key, axis=1, keepdims=True) * v
    o_ref[...] = o
    st_s[...] = st * jnp.exp(cum_l) + _mm_tn(v, key * jnp.exp(cum_l - cum))


def hgrn_matrices():
    t = ROW_TILE
    ti = np.arange(t)[:, None]
    ii = np.arange(t)[None, :]
    mlev = np.zeros((2, len(_hgrn_levels()), t, t), np.float32)
    for li, b in enumerate(_hgrn_levels()):
        bnd = (ti // (2 * b)) * (2 * b) + b - 1
        upper = (ti // b) % 2 == 1
        mlev[0, li] = np.where(upper, (ii > bnd) & (ii <= ti), (ii > ti) & (ii <= bnd))
        mlev[1, li] = np.where(upper, (ii > bnd) & (ii < ti), (ii >= ti) & (ii <= bnd))
    mincl = np.stack([(ii <= ti), (ii >= ti)]).astype(np.float32)
    return (jnp.asarray(mlev.reshape(2, -1, t), BF16), jnp.asarray(mincl, BF16))


def hgrn_scan(z, lb, mlev, mincl, n_ctx):
    n = z.shape[0]
    t = ROW_TILE
    nt = n // t
    nct = n_ctx // t
    hh = C_HEADS
    srow = lambda d, h, j: _scan_tile(d, j, nt, nct)
    return pl.pallas_call(
        _hgrn_kernel,
        grid=(2, hh, nt),
        in_specs=[pl.BlockSpec((t, LANE), lambda d, h, j: (srow(d, h, j), h)),
                  pl.BlockSpec((t, LANE), lambda d, h, j: (srow(d, h, j), hh * (1 + d) + h)),
                  pl.BlockSpec((t, LANE), lambda d, h, j: (srow(d, h, j), 3 * hh + h)),
                  pl.BlockSpec((None, 1, LANE), lambda d, h, j: (h, 0, 0)),
                  pl.BlockSpec((None,) + mlev.shape[1:], lambda d, h, j: (d, 0, 0)),
                  pl.BlockSpec((None, t, t), lambda d, h, j: (d, 0, 0))],
        out_specs=pl.BlockSpec((None, t, LANE), lambda d, h, j: (d, srow(d, h, j), h)),
        out_shape=jax.ShapeDtypeStruct((2, n, hh * LANE), F32),
        scratch_shapes=[pltpu.VMEM((LANE, LANE), F32)],
        compiler_params=_cparams(("parallel", "parallel", "arbitrary")),
        name="hgrn_scan",
    )(z, z, z, lb.reshape(hh, 1, LANE), mlev, mincl)


def _hgrn_finish_kernel(o_ref, g_ref, w_ref, out_ref):
    w = w_ref[...]
    for h in range(C_HEADS):
        sl = slice(h * LANE, (h + 1) * LANE)
        o = o_ref[0, :, sl] + o_ref[1, :, sl]
        ms = jnp.mean(o * o, axis=-1, keepdims=True)
        out_ref[:, sl] = (o * lax.rsqrt(ms + 1e-5) * w * _silu(g_ref[:, sl])).astype(out_ref.dtype)


def hgrn_finish(o, z, norm_w):
    n = z.shape[0]
    t = ROW_TILE
    dm = C_HEADS * LANE
    return pl.pallas_call(
        _hgrn_finish_kernel,
        grid=(n // t,),
        in_specs=[pl.BlockSpec((2, t, dm), lambda i: (0, i, 0)),
                  pl.BlockSpec((t, dm), lambda i: (i, 4)),
                  pl.BlockSpec((1, LANE), lambda i: (0, 0))],
        out_specs=pl.BlockSpec((t, dm), lambda i: (i, 0)),
        out_shape=jax.ShapeDtypeStruct((n, dm), BF16),
        compiler_params=_cparams(("parallel",)),
        name="hgrn_finish",
    )(o, z, norm_w.reshape(1, LANE))


def rwkv_matrices():
    t = ROW_TILE
    ti = np.arange(t)[:, None]
    ii = np.arange(t)[None, :]
    same = (ti // RWKV_CHUNK) == (ii // RWKV_CHUNK)
    mincl = np.stack([same & (ii <= ti), same & (ii >= ti)]).astype(np.float32)
    mbef = np.stack([same & (ii < ti), same & (ii > ti)]).astype(np.float32)
    return (jnp.asarray(mincl, BF16), jnp.asarray(mbef, BF16), jnp.asarray(same.astype(np.float32), BF16))


def even_mixer(h, n_ctx, w_in, lam_params, subln_w, mu, w0, w2, a0, a2, g2, k_k, k_a, r_k, lnx_w, lnx_b,
               layer_idx, rope):
    n, d = h.shape
    n_lat = n - n_ctx
    hq = A_HEADS * 2 * A_QK_DIM
    a_in = 3 * hq
    cos_t, sin_t = rope
    w_qk = jnp.concatenate([w_in[:, :hq] * (A_QK_DIM ** -0.5), w_in[:, hq:2 * hq]], axis=1).astype(BF16)
    qk = matmul_rope(h, w_qk, cos_t, sin_t)
    v = matmul(h, w_in[:, 2 * hq:a_in].astype(BF16), BF16)
    b_in = w_in.shape[1] - a_in
    wz = 3 * B_WIDTH + 512
    w_b = jnp.pad(w_in[:, a_in:], ((0, 0), (0, wz - b_in))).astype(BF16)
    zb = matmul(h, w_b, F32)

    lam_f = lam_params.astype(F32)
    lam_init = 0.8 - 0.6 * math.exp(-0.3 * layer_idx)
    lam = jnp.exp(jnp.sum(lam_f[0] * lam_f[1])) - jnp.exp(jnp.sum(lam_f[2] * lam_f[3])) + lam_init
    q = qk[:, :hq]
    k = qk[:, hq:]
    tq = _row_tile(n_lat, (512, 256))
    tk = _row_tile(n, (1280, 640, 256))
    o_lat = diff_attention(q[n_ctx:], k, v, lam, subln_w, lam_init, tq, tk)
    o_ctx = diff_attention(q[:n_ctx], k[:n_ctx], v[:n_ctx], lam, subln_w, lam_init, n_ctx, n_ctx)
    a_out = jnp.concatenate([o_ctx, o_lat], axis=0)

    c = B_WIDTH
    mu_p = jnp.pad(mu, (0, wz - b_in)).reshape(1, wz)
    zc = jnp.zeros((B_DECAY_RANK, c), F32)
    w2cat = jnp.concatenate([jnp.concatenate([w2[0], zc], 1), jnp.concatenate([zc, w2[1]], 1)], 0)
    a2cat = jnp.concatenate([jnp.concatenate([a2[0], zc], 1), jnp.concatenate([zc, a2[1]], 1)], 0)
    w0cat = w0.reshape(1, 2 * c)
    a0cat = a0.reshape(1, 2 * c)
    g2pad = jnp.pad(g2, ((0, 256 - B_GATE_RANK), (0, 0))).astype(BF16)
    hid = np.arange(c) // B_HEAD_DIM
    ehead = jnp.asarray((hid[:, None] == hid[None, :]).astype(np.float32), BF16)
    r, vv, kkn, g, kd, lw, bd = rwkv_prep(zb, mu_p, w2cat, w0cat, a2cat, a0cat, g2pad, k_k.reshape(1, c),
                                          k_a.reshape(1, c), ehead, n_ctx)
    mincl, mbef, msame = rwkv_matrices()
    gh, qy = rwkv_chunk(r, vv, kkn, kd, lw, bd, mincl, mbef, msame)
    y = rwkv_seq(gh, qy, n_ctx)
    b_out = rwkv_finish(y, r, vv, kd, g, r_k.reshape(1, c), lnx_w.reshape(1, c), lnx_b.reshape(1, c), ehead)
    return jnp.concatenate([a_out, b_out], axis=1)


def odd_mixer(h, n_ctx, w_in, lb, norm_w):
    z = matmul(h, w_in.astype(BF16), F32)
    mlev, mincl = hgrn_matrices()
    o = hgrn_scan(z, lb, mlev, mincl, n_ctx)
    return hgrn_finish(o, z, norm_w)


def kernel(x, c, ctx, c_ctx, ada_w, ada_b, ln_g, ln_b, even_w_in, even_w_out, diff_lambda, diff_subln_w, rwkv_mu, rwkv_w0, rwkv_w2, rwkv_a0, rwkv_a2, rwkv_g2, rwkv_k_k, rwkv_k_a, rwkv_r_k, rwkv_lnx_w, rwkv_lnx_b, odd_w_in, odd_w_out, hgrn_lb_raw, hgrn_norm_w, moe_router_w, moe_router_b, moe_w_gu, moe_w_down, moe_shared_gu, moe_shared_down):
    bsz, n_lat, d = x.shape
    n_ctx = ctx.shape[1]
    assert bsz == 1 and n_ctx % ROW_TILE == 0 and n_lat % ROW_TILE == 0
    depth = ada_w.shape[0]
    xs = jnp.concatenate([ctx[0], x[0]], axis=0)
    c8 = jnp.zeros((8, d), F32).at[0].set(c[0]).at[1].set(c_ctx)
    mods = adaln(c8, ada_w, ada_b)
    lb_soft = jax.nn.softmax(hgrn_lb_raw.astype(F32), axis=0)
    lower_bounds = jnp.cumsum(lb_soft, axis=0) - lb_soft[0]
    rope = rope_tables(n_ctx, n_lat)

    h = modulate(xs, mods[0], n_ctx)
    for l in range(depth):
        mod = mods[l]
        if l % 2 == 0:
            e = l // 2
            mix = even_mixer(h, n_ctx, even_w_in[e], diff_lambda[e], diff_subln_w[e], rwkv_mu[e], rwkv_w0[e],
                             rwkv_w2[e], rwkv_a0[e], rwkv_a2[e], rwkv_g2[e], rwkv_k_k[e], rwkv_k_a[e],
                             rwkv_r_k[e], rwkv_lnx_w[e], rwkv_lnx_b[e], l, rope)
            w_out = even_w_out[e]
        else:
            o = l // 2
            mix = odd_mixer(h, n_ctx, odd_w_in[o], lower_bounds[l].reshape(C_HEADS, C_KEY_DIM), hgrn_norm_w[o])
            w_out = odd_w_out[o]
        x1, f_in = outproj_ln(mix, w_out.astype(BF16), xs, mod, ln_g[l, 0], ln_b[l, 0], n_ctx)
        mod_next = mods[l + 1] if l + 1 < depth else None
        res = moe_layer(f_in, x1, mod, ln_g[l, 1], ln_b[l, 1], moe_router_w[l], moe_router_b[l], moe_w_gu[l],
                        moe_w_down[l], moe_shared_gu[l], moe_shared_down[l], n_ctx, mod_next)
        if mod_next is not None:
            xs, h = res
        else:
            (xs,) = res
    return xs[n_ctx:].reshape(bsz, n_lat, d)
```

```python
import functools
import math

import numpy as np
import jax
import jax.numpy as jnp
from jax import lax
from jax.experimental import pallas as pl
from jax.experimental.pallas import tpu as pltpu

F32 = jnp.float32
BF16 = jnp.bfloat16
HIGHEST = lax.Precision.HIGHEST

GRID_W = 64
A_HEADS = 8
A_QK_DIM = 64
A_V_DIM = 128
ROPE_BASE = 10000.0
B_HEADS = 16
B_HEAD_DIM = 64
B_WIDTH = 1024
B_DECAY_RANK = 64
B_ICL_RANK = 64
B_GATE_RANK = 160
RWKV_LN_EPS = 64e-5
C_HEADS = 16
C_KEY_DIM = 128
N_EXPERTS = 64
TOP_K = 6
EXPERT_FF = 512
SHARED_FF = 1024
ROUTE_SCALE = 2.5
DEPTH = 2
DEEPNORM_ALPHA = (2 * DEPTH) ** 0.25

ROW_TILE = 256
RWKV_CHUNK = 64
EXPERT_BLOCK = 256
LANE = 128
TOK_ROWS = 16
COMBINE_TILE = 128
VMEM_LIMIT = 56 * 1024 * 1024


def _cparams(sem):
    return pltpu.CompilerParams(dimension_semantics=sem, vmem_limit_bytes=VMEM_LIMIT)


def _mm(a, b):
    return jnp.dot(a.astype(BF16), b.astype(BF16), preferred_element_type=F32)


def _mm_nt(a, b):
    return lax.dot_general(a.astype(BF16), b.astype(BF16), (((1,), (1,)), ((), ())),
                           preferred_element_type=F32)


def _mm_tn(a, b):
    return lax.dot_general(a.astype(BF16), b.astype(BF16), (((0,), (0,)), ((), ())),
                           preferred_element_type=F32)


def _split3(x):
    hi = x.astype(BF16)
    r1 = x - hi.astype(F32)
    mid = r1.astype(BF16)
    lo = (r1 - mid.astype(F32)).astype(BF16)
    return hi, mid, lo


def _mm_exact_lhs(m_bf16, x):
    hi, mid, lo = _split3(x)
    d = lambda t: jnp.dot(m_bf16, t, preferred_element_type=F32)
    return d(hi) + d(mid) + d(lo)


def _mm_exact_rhs(x, m_bf16):
    hi, mid, lo = _split3(x)
    d = lambda t: jnp.dot(t, m_bf16, preferred_element_type=F32)
    return d(hi) + d(mid) + d(lo)


def _silu(x):
    return x * jax.nn.sigmoid(x)


def _softplus(x):
    return jnp.maximum(x, 0.0) + jnp.log1p(jnp.exp(-jnp.abs(x)))


def _row_tile(m, cands=(1280, 640, 512, 256)):
    for t in cands:
        if m % t == 0:
            return t
    raise ValueError(f"no row tile for {m}")


def _adaln_kernel(c_ref, w_ref, b_ref, o_ref):
    s = _silu(c_ref[...])
    o_ref[...] = _mm(s, w_ref[...]) + b_ref[...]


def adaln(c8, ada_w, ada_b):
    depth, d, n6 = ada_w.shape
    tn = 512
    return pl.pallas_call(
        _adaln_kernel,
        grid=(depth, n6 // tn),
        in_specs=[pl.BlockSpec((8, d), lambda l, j: (0, 0)),
                  pl.BlockSpec((None, d, tn), lambda l, j: (l, 0, j)),
                  pl.BlockSpec((None, 1, tn), lambda l, j: (l, 0, j))],
        out_specs=pl.BlockSpec((None, 8, tn), lambda l, j: (l, 0, j)),
        out_shape=jax.ShapeDtypeStruct((depth, 8, n6), F32),
        compiler_params=_cparams(("parallel", "parallel")),
        name="adaln",
    )(c8, ada_w, ada_b.reshape(depth, 1, n6))


def _mod_rows(mod_ref, i, n_ctx_tiles, idx, d):
    row = jnp.where(i < n_ctx_tiles, 1, 0)
    return mod_ref[pl.ds(row, 1), idx * d:(idx + 1) * d]


def _modulate_kernel(x_ref, mod_ref, o_ref, *, n_ctx_tiles, d):
    i = pl.program_id(0)
    sh = _mod_rows(mod_ref, i, n_ctx_tiles, 0, d)
    sc = _mod_rows(mod_ref, i, n_ctx_tiles, 1, d)
    o_ref[...] = (x_ref[...] * (1.0 + sc) + sh).astype(o_ref.dtype)


def modulate(x, mod, n_ctx):
    n, d = x.shape
    return pl.pallas_call(
        functools.partial(_modulate_kernel, n_ctx_tiles=n_ctx // ROW_TILE, d=d),
        grid=(n // ROW_TILE,),
        in_specs=[pl.BlockSpec((ROW_TILE, d), lambda i: (i, 0)),
                  pl.BlockSpec(mod.shape, lambda i: (0, 0))],
        out_specs=pl.BlockSpec((ROW_TILE, d), lambda i: (i, 0)),
        out_shape=jax.ShapeDtypeStruct((n, d), BF16),
        compiler_params=_cparams(("parallel",)),
        name="modulate",
    )(x, mod)


def _mm_kernel(x_ref, w_ref, o_ref):
    o_ref[...] = jnp.dot(x_ref[...], w_ref[...], preferred_element_type=F32).astype(o_ref.dtype)


def matmul(x, w, out_dtype, tn=512):
    m, k = x.shape
    n = w.shape[1]
    tm = _row_tile(m)
    return pl.pallas_call(
        _mm_kernel,
        grid=(m // tm, n // tn),
        in_specs=[pl.BlockSpec((tm, k), lambda i, j: (i, 0)),
                  pl.BlockSpec((k, tn), lambda i, j: (0, j))],
        out_specs=pl.BlockSpec((tm, tn), lambda i, j: (i, j)),
        out_shape=jax.ShapeDtypeStruct((m, n), out_dtype),
        compiler_params=_cparams(("parallel", "parallel")),
        name="matmul",
    )(x, w)


def _mm_rope_kernel(x_ref, w_ref, cos_ref, sin_ref, o_ref, *, tn):
    acc = jnp.dot(x_ref[...], w_ref[...], preferred_element_type=F32)
    cos = cos_ref[...]
    sin = sin_ref[...]
    lane = lax.broadcasted_iota(jnp.int32, cos.shape, 1)
    first = ((lane // 16) % 2) == 0
    for j in range(tn // LANE):
        blk = acc[:, j * LANE:(j + 1) * LANE]
        partner = jnp.where(first, pltpu.roll(blk, LANE - 16, 1), pltpu.roll(blk, 16, 1))
        o_ref[:, j * LANE:(j + 1) * LANE] = (blk * cos + partner * sin).astype(o_ref.dtype)


def matmul_rope(x, w, cos_t, sin_t, tn=512):
    m, k = x.shape
    n = w.shape[1]
    tm = _row_tile(m)
    return pl.pallas_call(
        functools.partial(_mm_rope_kernel, tn=tn),
        grid=(m // tm, n // tn),
        in_specs=[pl.BlockSpec((tm, k), lambda i, j: (i, 0)),
                  pl.BlockSpec((k, tn), lambda i, j: (0, j)),
                  pl.BlockSpec((tm, LANE), lambda i, j: (i, 0)),
                  pl.BlockSpec((tm, LANE), lambda i, j: (i, 0))],
        out_specs=pl.BlockSpec((tm, tn), lambda i, j: (i, j)),
        out_shape=jax.ShapeDtypeStruct((m, n), BF16),
        compiler_params=_cparams(("parallel", "parallel")),
        name="matmul_rope",
    )(x, w, cos_t, sin_t)


def rope_tables(n_ctx, n_lat):
    t = jnp.arange(n_lat)
    row = (t // GRID_W).astype(F32)
    col = (t % GRID_W).astype(F32)
    half = A_QK_DIM // 2
    inv = ROPE_BASE ** (-jnp.arange(0, half, 2, dtype=F32) / half)
    ang_r = row[:, None] * inv
    ang_c = col[:, None] * inv
    cos64 = jnp.concatenate([jnp.cos(ang_r), jnp.cos(ang_r), jnp.cos(ang_c), jnp.cos(ang_c)], axis=-1)
    sin64 = jnp.concatenate([-jnp.sin(ang_r), jnp.sin(ang_r), -jnp.sin(ang_c), jnp.sin(ang_c)], axis=-1)
    cos_l = jnp.concatenate([cos64, cos64], axis=-1)
    sin_l = jnp.concatenate([sin64, sin64], axis=-1)
    cos_t = jnp.concatenate([jnp.ones((n_ctx, LANE), F32), cos_l], axis=0)
    sin_t = jnp.concatenate([jnp.zeros((n_ctx, LANE), F32), sin_l], axis=0)
    return cos_t, sin_t


def _attn_kernel(lam_ref, q_ref, k_ref, v_ref, w_ref, o_ref, qq_s, m_s, acc_s, *, nk, tk, out_scale):
    ki = pl.program_id(2)

    @pl.when(ki == 0)
    def _():
        q = q_ref[...]
        lane = lax.broadcasted_iota(jnp.int32, q.shape, 1)
        zero = jnp.zeros_like(q)
        qq_s[0] = jnp.where(lane < A_QK_DIM, q, zero)
        qq_s[1] = jnp.where(lane >= A_QK_DIM, q, zero)
        m_s[...] = jnp.full(m_s.shape, -jnp.inf, F32)
        acc_s[...] = jnp.zeros(acc_s.shape, F32)

    k = k_ref[...]
    v = v_ref[...]
    v_ext = jnp.concatenate([v, jnp.ones_like(v)], axis=1)
    n_col = tk // LANE
    for mi in range(2):
        s = lax.dot_general(qq_s[mi], k, (((1,), (1,)), ((), ())), preferred_element_type=F32).astype(BF16)
        cm = s[:, 0:LANE]
        for c in range(1, n_col):
            cm = jnp.maximum(cm, s[:, c * LANE:(c + 1) * LANE])
        m_old = m_s[mi]
        m_new = jnp.maximum(m_old, jnp.max(cm.astype(F32), axis=1, keepdims=True))
        mb = m_new.astype(BF16)
        p = jnp.exp2(s - jnp.concatenate([mb] * n_col, axis=1))
        alpha = jnp.exp2(m_old - m_new)
        acc_s[mi] = (jnp.concatenate([alpha, alpha], axis=1) * acc_s[mi]
                     + jnp.dot(p, v_ext, preferred_element_type=F32))
        m_s[mi] = m_new

    @pl.when(ki == nk - 1)
    def _():
        lam = lam_ref[0]
        o0 = acc_s[0, :, :LANE] / acc_s[0, :, LANE:LANE + 1]
        o1 = acc_s[1, :, :LANE] / acc_s[1, :, LANE:LANE + 1]
        o = o0 - lam * o1
        ms = jnp.mean(o * o, axis=-1, keepdims=True)
        o_ref[...] = (o * lax.rsqrt(ms + 1e-5) * w_ref[...] * out_scale).astype(o_ref.dtype)


def diff_attention(q, k, v, lam, subln_w, lam_init, tq, tk):
    nq = q.shape[0]
    nkv = k.shape[0]
    nk = nkv // tk
    return pl.pallas_call(
        functools.partial(_attn_kernel, nk=nk, tk=tk, out_scale=1.0 - lam_init),
        grid=(A_HEADS, nq // tq, nk),
        in_specs=[pl.BlockSpec(memory_space=pltpu.SMEM),
                  pl.BlockSpec((tq, LANE), lambda h, i, j: (i, h)),
                  pl.BlockSpec((tk, LANE), lambda h, i, j: (j, h)),
                  pl.BlockSpec((tk, LANE), lambda h, i, j: (j, h)),
                  pl.BlockSpec((1, LANE), lambda h, i, j: (0, 0))],
        out_specs=pl.BlockSpec((tq, LANE), lambda h, i, j: (i, h)),
        out_shape=jax.ShapeDtypeStruct((nq, A_HEADS * A_V_DIM), BF16),
        scratch_shapes=[pltpu.VMEM((2, tq, LANE), BF16), pltpu.VMEM((2, tq, LANE), F32),
                        pltpu.VMEM((2, tq, 2 * LANE), F32)],
        compiler_params=_cparams(("parallel", "parallel", "arbitrary")),
        name="diff_attention",
    )(lam.reshape(1), q, k, v, subln_w.reshape(1, LANE))


def _rwkv_prep_kernel(z_ref, zp_ref, zn_ref, mu_ref, w2_ref, w0_ref, a2_ref, a0_ref, g2_ref, kk_ref, ka_ref,
                      eh_ref, r_o, v_o, kkn_o, g_o, kd_o, lw_o, bd_o, *, n_ctx, n_tot):
    i = pl.program_id(0)
    z = z_ref[...]
    tm = z.shape[0]
    grow = i * tm + lax.broadcasted_iota(jnp.int32, (tm, 1), 0)
    lrow = lax.broadcasted_iota(jnp.int32, (tm, 1), 0)
    prev = jnp.where(lrow == 0, zp_ref[7:8, :], pltpu.roll(z, 1, 0))
    nxt = jnp.where(lrow == tm - 1, zn_ref[0:1, :], pltpu.roll(z, tm - 1, 0))
    has_prev = jnp.logical_and(grow != 0, grow != n_ctx)
    has_next = jnp.logical_and(grow != n_ctx - 1, grow != n_tot - 1)
    prev = jnp.where(has_prev, prev, 0.0)
    nxt = jnp.where(has_next, nxt, 0.0)
    zm = z + (0.5 * (prev + nxt) - z) * mu_ref[...]
    c = B_WIDTH
    r = zm[:, 0:c]
    k = zm[:, c:2 * c]
    v = zm[:, 2 * c:3 * c]
    wd = zm[:, 3 * c:3 * c + 128]
    ad = zm[:, 3 * c + 128:3 * c + 256]
    gd = zm[:, 3 * c + 256:3 * c + 512]
    wl = w0_ref[...] + jnp.dot(jnp.tanh(wd), w2_ref[...], precision=HIGHEST, preferred_element_type=F32)
    al = a0_ref[...] + jnp.dot(ad, a2_ref[...], precision=HIGHEST, preferred_element_type=F32)
    g = _mm(jax.nn.sigmoid(gd), g2_ref[...])
    kk = k * kk_ref[...]
    ss = _mm_exact_rhs(kk * kk, eh_ref[...])
    kkn = kk / jnp.maximum(jnp.sqrt(ss), 1e-12)
    r_o[...] = r
    v_o[...] = v
    kkn_o[...] = kkn
    g_o[...] = g
    for d in range(2):
        w_log = -_softplus(-wl[:, d * c:(d + 1) * c]) - 0.5
        lw_o[d] = -jnp.exp(w_log)
        icl = jax.nn.sigmoid(al[:, d * c:(d + 1) * c])
        kd_o[d] = k * (1.0 + (icl - 1.0) * ka_ref[...])
        bd_o[d] = kkn * icl


def rwkv_prep(zb, mu, w2cat, w0cat, a2cat, a0cat, g2pad, k_k, k_a, ehead, n_ctx):
    n, wz = zb.shape
    tm = ROW_TILE
    nt = n // tm
    c = B_WIDTH
    full = lambda a: pl.BlockSpec(a.shape, lambda i: (0,) * a.ndim)
    tok = pl.BlockSpec((tm, c), lambda i: (i, 0))
    tok2 = pl.BlockSpec((2, tm, c), lambda i: (0, i, 0))
    sd = jax.ShapeDtypeStruct
    return pl.pallas_call(
        functools.partial(_rwkv_prep_kernel, n_ctx=n_ctx, n_tot=n),
        grid=(nt,),
        in_specs=[pl.BlockSpec((tm, wz), lambda i: (i, 0)),
                  pl.BlockSpec((8, wz), lambda i: (jnp.maximum(i * (tm // 8) - 1, 0), 0)),
                  pl.BlockSpec((8, wz), lambda i: (jnp.minimum((i + 1) * (tm // 8), n // 8 - 1), 0)),
                  full(mu), full(w2cat), full(w0cat), full(a2cat), full(a0cat), full(g2pad),
                  full(k_k), full(k_a), full(ehead)],
        out_specs=[tok, tok, tok, tok, tok2, tok2, tok2],
        out_shape=[sd((n, c), F32)] * 4 + [sd((2, n, c), F32)] * 3,
        compiler_params=_cparams(("parallel",)),
        name="rwkv_prep",
    )(zb, zb, zb, mu, w2cat, w0cat, a2cat, a0cat, g2pad, k_k, k_a, ehead)


def _rwkv_chunk_kernel(r_ref, v_ref, kk_ref, kd_ref, lw_ref, bd_ref, mincl_ref, mbef_ref, msame_ref,
                       gh_ref, qy_ref):
    lw = lw_ref[...]
    mincl = mincl_ref[...]
    cum_i = _mm_exact_lhs(mincl, lw)
    cum_e = cum_i - lw
    cum_l = _mm_exact_lhs(msame_ref[...], lw)
    r = r_ref[...]
    v = v_ref[...]
    a = -kk_ref[...]
    k = kd_ref[...]
    b = bd_ref[...]
    at = a * jnp.exp(cum_e)
    rt = r * jnp.exp(cum_i)
    einv = jnp.exp(-cum_i)
    bt = b * einv
    kt = k * einv
    e_l = jnp.exp(cum_l - cum_i)
    bh = b * e_l
    kh = k * e_l
    p_l = jnp.exp(cum_l)
    incl = mincl > 0.5
    bef = mbef_ref[...] > 0.5
    hd = B_HEAD_DIM
    nch = ROW_TILE // RWKV_CHUNK
    zeros_h = jnp.zeros((ROW_TILE, hd), F32)
    eye = (lax.broadcasted_iota(jnp.int32, (RWKV_CHUNK, LANE), 0)
           == lax.broadcasted_iota(jnp.int32, (RWKV_CHUNK, LANE), 1))
    ri = lax.broadcasted_iota(jnp.int32, (ROW_TILE, ROW_TILE), 0)
    ci = lax.broadcasted_iota(jnp.int32, (ROW_TILE, ROW_TILE), 1)
    xr = ri ^ ci
    eye_t = jnp.where(ri == ci, 1.0, 0.0)
    for h in range(LANE // hd):
        sl = slice(h * hd, (h + 1) * hd)
        a_ab = jnp.where(bef, _mm_nt(at[:, sl], bt[:, sl]), 0.0)
        a_ak = jnp.where(bef, _mm_nt(at[:, sl], kt[:, sl]), 0.0)
        m_rb = jnp.where(incl, _mm_nt(rt[:, sl], bt[:, sl]), 0.0)
        m_rk = jnp.where(incl, _mm_nt(rt[:, sl], kt[:, sl]), 0.0)
        vh = v[:, sl]
        t_inv = eye_t + jnp.where(xr == 1, a_ab, 0.0)
        for sh in range(1, 6):
            a_lev = jnp.where((xr >> sh) == 1, a_ab, 0.0)
            t_inv = t_inv + _mm(t_inv, _mm(a_lev, t_inv))
        x = _mm(t_inv, jnp.concatenate([at[:, sl], _mm(a_ak, vh)], axis=1))
        v0 = jnp.concatenate([zeros_h, vh], axis=1)
        qy_ref[h] = jnp.concatenate([rt[:, sl], zeros_h], axis=1) + _mm(m_rb, x) + _mm(m_rk, v0)
        for c in range(nch):
            rows = slice(c * RWKV_CHUNK, (c + 1) * RWKV_CHUNK)
            gh = _mm_tn(bh[rows, sl], x[rows]) + _mm_tn(kh[rows, sl], v0[rows])
            plr = jnp.concatenate([p_l[c * RWKV_CHUNK:c * RWKV_CHUNK + 1, sl], jnp.zeros((1, hd), F32)], axis=1)
            gh_ref[h, rows, :] = gh + jnp.where(eye, jnp.broadcast_to(plr, (RWKV_CHUNK, LANE)), 0.0)


def rwkv_chunk(r, v, kkn, kd, lw, bd, mincl, mbef, msame):
    n, c = r.shape
    tm = ROW_TILE
    tokb = pl.BlockSpec((tm, LANE), lambda d, s, p: (s, p))
    tokd = pl.BlockSpec((None, tm, LANE), lambda d, s, p: (d, s, p))
    mdir = pl.BlockSpec((None, tm, tm), lambda d, s, p: (d, 0, 0))
    outb = pl.BlockSpec((None, 2, tm, LANE), lambda d, s, p: (d, p, s, 0))
    sd = jax.ShapeDtypeStruct((2, B_HEADS, n, LANE), F32)
    return pl.pallas_call(
        _rwkv_chunk_kernel,
        grid=(2, n // tm, B_HEADS // 2),
        in_specs=[tokb, tokb, tokb, tokd, tokd, tokd, mdir, mdir,
                  pl.BlockSpec((tm, tm), lambda d, s, p: (0, 0))],
        out_specs=[outb, outb],
        out_shape=[sd, sd],
        compiler_params=_cparams(("parallel", "parallel", "parallel")),
        name="rwkv_chunk",
    )(r, v, kkn, kd, lw, bd, mincl, mbef, msame)


def _scan_tile(d, j, nt, n_ctx_tiles):
    back = jnp.where(j < n_ctx_tiles, n_ctx_tiles - 1 - j, nt - 1 + n_ctx_tiles - j)
    return jnp.where(d == 0, j, back)


def _rwkv_seq_kernel(gh_ref, qy_ref, y_ref, z_s):
    d = pl.program_id(0)
    j = pl.program_id(1)

    @pl.when(j == 0)
    def _():
        z_s[...] = jnp.zeros(z_s.shape, F32)

    hd = B_HEAD_DIM
    nch = ROW_TILE // RWKV_CHUNK
    for ci in range(nch):
        c = jnp.where(d == 0, ci, nch - 1 - ci)
        off = pl.multiple_of(c * RWKV_CHUNK, RWKV_CHUNK)
        ys = []
        for h in range(B_HEADS):
            z = z_s[h]
            gh = gh_ref[h, pl.ds(off, RWKV_CHUNK), :]
            qy = qy_ref[h, pl.ds(off, RWKV_CHUNK), :]
            ys.append(_mm(qy[:, :hd], z) + qy[:, hd:])
            z_s[h] = _mm(gh[:, :hd], z) + gh[:, hd:]
        y_ref[pl.ds(off, RWKV_CHUNK), :] = jnp.concatenate(ys, axis=1)


def rwkv_seq(gh, qy, n_ctx):
    _, hh, n, _ = gh.shape
    tm = ROW_TILE
    nt = n // tm
    nct = n_ctx // tm
    inb = pl.BlockSpec((None, hh, tm, LANE), lambda d, j: (d, 0, _scan_tile(d, j, nt, nct), 0))
    return pl.pallas_call(
        _rwkv_seq_kernel,
        grid=(2, nt),
        in_specs=[inb, inb],
        out_specs=pl.BlockSpec((None, tm, B_WIDTH), lambda d, j: (d, _scan_tile(d, j, nt, nct), 0)),
        out_shape=jax.ShapeDtypeStruct((2, n, B_WIDTH), F32),
        scratch_shapes=[pltpu.VMEM((hh, B_HEAD_DIM, B_HEAD_DIM), F32)],
        compiler_params=_cparams(("parallel", "arbitrary")),
        name="rwkv_seq",
    )(gh, qy)


def _rwkv_finish_kernel(y_ref, r_ref, v_ref, kd_ref, g_ref, rk_ref, lw_ref, lb_ref, eh_ref, o_ref):
    y = y_ref[0] + y_ref[1]
    eh = eh_ref[...]
    inv = 1.0 / B_HEAD_DIM
    mu = _mm_exact_rhs(y, eh) * inv
    yc = y - mu
    var = _mm_exact_rhs(yc * yc, eh) * inv
    yn = yc * lax.rsqrt(var + RWKV_LN_EPS) * lw_ref[...] + lb_ref[...]
    r = r_ref[...]
    rk = (r * kd_ref[0] + r * kd_ref[1]) * rk_ref[...]
    bonus = _mm_exact_rhs(rk, eh) * v_ref[...]
    o_ref[...] = ((yn + bonus) * g_ref[...]).astype(o_ref.dtype)


def rwkv_finish(y, r, v, kd, g, r_k, lnx_w, lnx_b, ehead):
    n, c = r.shape
    tm = ROW_TILE
    tok = pl.BlockSpec((tm, c), lambda i: (i, 0))
    tok2 = pl.BlockSpec((2, tm, c), lambda i: (0, i, 0))
    full = lambda a: pl.BlockSpec(a.shape, lambda i: (0,) * a.ndim)
    return pl.pallas_call(
        _rwkv_finish_kernel,
        grid=(n // tm,),
        in_specs=[tok2, tok, tok, tok2, tok, full(r_k), full(lnx_w), full(lnx_b), full(ehead)],
        out_specs=tok,
        out_shape=jax.ShapeDtypeStruct((n, c), BF16),
        compiler_params=_cparams(("parallel",)),
        name="rwkv_finish",
    )(y, r, v, kd, g, r_k, lnx_w, lnx_b, ehead)


def _layer_norm(x, g, b):
    mu = jnp.mean(x, axis=-1, keepdims=True)
    xc = x - mu
    var = jnp.mean(xc * xc, axis=-1, keepdims=True)
    return xc * lax.rsqrt(var + 1e-5) * g + b


def _store_token_rows(ref, val):
    tm = val.shape[0]
    for s in range(TOK_ROWS):
        ref[pl.ds(s, tm, stride=TOK_ROWS), :] = val[:, s * LANE:(s + 1) * LANE]


def _load_token_rows(ref, tm):
    return jnp.concatenate([ref[pl.ds(s, tm, stride=TOK_ROWS), :] for s in range(TOK_ROWS)], axis=1)


def _outproj_kernel(a_ref, w_ref, x_ref, mod_ref, g_ref, b_ref, x1_ref, f_ref, f3_ref, *, n_ctx_tiles, d):
    i = pl.program_id(0)
    m = jnp.dot(a_ref[...], w_ref[...], preferred_element_type=F32)
    gate = _mod_rows(mod_ref, i, n_ctx_tiles, 2, d)
    x1 = _layer_norm(DEEPNORM_ALPHA * x_ref[...] + gate * m, g_ref[...], b_ref[...])
    x1_ref[...] = x1
    sh = _mod_rows(mod_ref, i, n_ctx_tiles, 3, d)
    sc = _mod_rows(mod_ref, i, n_ctx_tiles, 4, d)
    f = x1 * (1.0 + sc) + sh
    f_ref[...] = f
    _store_token_rows(f3_ref, f)


def outproj_ln(a, w_out, x, mod, ln_g, ln_b, n_ctx):
    n, kdim = a.shape
    d = x.shape[1]
    tm = ROW_TILE
    tok = pl.BlockSpec((tm, d), lambda i: (i, 0))
    vec = pl.BlockSpec((1, d), lambda i: (0, 0))
    return pl.pallas_call(
        functools.partial(_outproj_kernel, n_ctx_tiles=n_ctx // tm, d=d),
        grid=(n // tm,),
        in_specs=[pl.BlockSpec((tm, kdim), lambda i: (i, 0)),
                  pl.BlockSpec((kdim, d), lambda i: (0, 0)),
                  tok, pl.BlockSpec(mod.shape, lambda i: (0, 0)), vec, vec],
        out_specs=[tok, tok, pl.BlockSpec((tm * TOK_ROWS, LANE), lambda i: (i, 0))],
        out_shape=[jax.ShapeDtypeStruct((n, d), F32)] * 2 + [jax.ShapeDtypeStruct((n * TOK_ROWS, LANE), F32)],
        compiler_params=_cparams(("parallel",)),
        name="outproj_ln",
    )(a, w_out, x, mod, ln_g.reshape(1, d), ln_b.reshape(1, d))


PICK_IDX, PICK_RANK, PICK_GATE = 0, 8, 16


def _router_kernel(f_ref, wr_hi_ref, wr_lo_ref, br_ref, wsg_ref, wsd_ref, ltri_ref, pick_ref, sh_ref, cnt_ref,
                   carry_s):
    i = pl.program_id(0)

    @pl.when(i == 0)
    def _():
        carry_s[...] = jnp.zeros(carry_s.shape, F32)

    f = f_ref[...]
    f_hi = f.astype(BF16)
    f_lo = (f - f_hi.astype(F32)).astype(BF16)
    dd = lambda a, b: jnp.dot(a, b[...], preferred_element_type=F32)
    logits = dd(f_hi, wr_hi_ref) + dd(f_lo, wr_hi_ref) + dd(f_hi, wr_lo_ref)
    scores = jax.nn.sigmoid(logits)
    lane = lax.broadcasted_iota(jnp.int32, scores.shape, 1).astype(F32)
    neg = jnp.float32(-jnp.inf)
    sel = jnp.where(lane < N_EXPERTS, scores + br_ref[...], neg)
    top = jnp.zeros(scores.shape, F32)
    chosen = jnp.zeros(scores.shape, F32)
    hits, firsts = [], []
    for j in range(TOP_K):
        mx = jnp.max(sel, axis=1, keepdims=True)
        first = jnp.min(jnp.where(sel == mx, lane, float(LANE)), axis=1, keepdims=True)
        hit = lane == first
        top = jnp.where(hit, scores, top)
        chosen = jnp.where(hit, 1.0, chosen)
        sel = jnp.where(hit, neg, sel)
        hits.append(hit)
        firsts.append(first)
    gates = top / jnp.sum(top, axis=1, keepdims=True) * ROUTE_SCALE
    carry = carry_s[0:1, :]
    rank = dd(ltri_ref[...], chosen.astype(BF16)) + carry
    pick = jnp.zeros(scores.shape, F32)
    for j in range(TOP_K):
        pick = jnp.where(lane == PICK_IDX + j, firsts[j], pick)
        rank_j = jnp.sum(jnp.where(hits[j], rank, 0.0), axis=1, keepdims=True)
        pick = jnp.where(lane == PICK_RANK + j, rank_j, pick)
        gate_j = jnp.sum(jnp.where(hits[j], gates, 0.0), axis=1, keepdims=True)
        pick = jnp.where(lane == PICK_GATE + j, gate_j, pick)
    pick_ref[...] = pick
    carry_s[0:1, :] = carry + jnp.sum(chosen, axis=0, keepdims=True)
    cnt_ref[...] = carry_s[...]
    ab = dd(f_hi, wsg_ref)
    hdn = _silu(ab[:, :SHARED_FF]) * ab[:, SHARED_FF:]
    sh_ref[...] = dd(hdn.astype(BF16), wsd_ref)


def router_shared(f_in, wr_hi, wr_lo, br, ws_gu, ws_down):
    n, d = f_in.shape
    tm = ROW_TILE
    full = lambda a: pl.BlockSpec(a.shape, lambda i: (0,) * a.ndim)
    sd = jax.ShapeDtypeStruct
    ltri = jnp.asarray(np.tril(np.ones((tm, tm), np.float32), -1), BF16)
    return pl.pallas_call(
        _router_kernel,
        grid=(n // tm,),
        in_specs=[pl.BlockSpec((tm, d), lambda i: (i, 0)), full(wr_hi), full(wr_lo), full(br),
                  full(ws_gu), full(ws_down), full(ltri)],
        out_specs=[pl.BlockSpec((tm, LANE), lambda i: (i, 0)), pl.BlockSpec((tm, d), lambda i: (i, 0)),
                   pl.BlockSpec((8, LANE), lambda i: (0, 0))],
        out_shape=[sd((n, LANE), F32), sd((n, d), F32), sd((8, LANE), F32)],
        scratch_shapes=[pltpu.VMEM((8, LANE), F32)],
        compiler_params=_cparams(("arbitrary",)),
        name="router_shared",
    )(f_in, wr_hi, wr_lo, br, ws_gu, ws_down, ltri)


def _slab(ref, row):
    return ref.at[pl.ds(pl.multiple_of(row * TOK_ROWS, TOK_ROWS), TOK_ROWS)]


def _dispatch_kernel(slots_ref, padlo_ref, padhi_ref, f3_ref, xs_hbm, zrow, sem, *, tm, n_tiles):
    i = pl.program_id(0)

    def body(t, carry):
        for j in range(TOP_K):
            slot = slots_ref[(i * tm + t) * TOP_K + j]
            pltpu.make_async_copy(_slab(f3_ref, t), _slab(xs_hbm, slot), sem.at[0]).start()
        return carry
    lax.fori_loop(0, tm, body, 0)
    for _ in range(TOP_K):
        pltpu.make_async_copy(f3_ref, xs_hbm.at[pl.ds(0, tm * TOK_ROWS)], sem.at[0]).wait()

    @pl.when(i == n_tiles - 1)
    def _():
        zrow[...] = jnp.zeros(zrow.shape, F32)

        def per_expert(e, carry):
            lo = padlo_ref[e]
            hi = padhi_ref[e]

            def fill(s, c):
                pltpu.make_async_copy(zrow, _slab(xs_hbm, s), sem.at[1]).start()
                return c
            lax.fori_loop(lo, hi, fill, 0)

            def drain(s, c):
                pltpu.make_async_copy(zrow, _slab(xs_hbm, 0), sem.at[1]).wait()
                return c
            lax.fori_loop(lo, hi, drain, 0)
            return carry
        lax.fori_loop(0, N_EXPERTS + 1, per_expert, 0)


def moe_dispatch_rows(f3, tok_slots, pad_lo, pad_hi, n_slots):
    tm = ROW_TILE
    n = f3.shape[0] // TOK_ROWS
    grid_spec = pltpu.PrefetchScalarGridSpec(
        num_scalar_prefetch=3,
        grid=(n // tm,),
        in_specs=[pl.BlockSpec((tm * TOK_ROWS, LANE), lambda i, sl, lo, hi: (i, 0))],
        out_specs=pl.BlockSpec(memory_space=pl.ANY),
        scratch_shapes=[pltpu.VMEM((TOK_ROWS, LANE), F32), pltpu.SemaphoreType.DMA((2,))],
    )
    return pl.pallas_call(
        functools.partial(_dispatch_kernel, tm=tm, n_tiles=n // tm),
        grid_spec=grid_spec,
        out_shape=jax.ShapeDtypeStruct((n_slots * TOK_ROWS, LANE), F32),
        compiler_params=_cparams(("arbitrary",)),
        name="moe_dispatch",
    )(tok_slots, pad_lo, pad_hi, f3)


def _expert_kernel(be_ref, nu_ref, xs_ref, wgu_ref, wdn_ref, ys_ref):
    i = pl.program_id(0)

    @pl.when(i < nu_ref[0])
    def _():
        x = _load_token_rows(xs_ref, EXPERT_BLOCK).astype(BF16)
        ab = jnp.dot(x, wgu_ref[...].astype(BF16), preferred_element_type=F32)
        hdn = _silu(ab[:, :EXPERT_FF]) * ab[:, EXPERT_FF:]
        y = jnp.dot(hdn.astype(BF16), wdn_ref[...].astype(BF16), preferred_element_type=F32)
        _store_token_rows(ys_ref, y)

    @pl.when(i >= nu_ref[0])
    def _():
        ys_ref[...] = jnp.zeros(ys_ref.shape, F32)


def routed_experts(xs, block_expert, n_used, w_gu, w_down):
    d = w_gu.shape[1]
    n_blocks = block_expert.shape[0]
    rows = EXPERT_BLOCK * TOK_ROWS
    last_used = lambda i, nu: jnp.minimum(i, jnp.maximum(nu[0] - 1, 0))
    grid_spec = pltpu.PrefetchScalarGridSpec(
        num_scalar_prefetch=2,
        grid=(n_blocks,),
        in_specs=[pl.BlockSpec((rows, LANE), lambda i, be, nu: (last_used(i, nu), 0)),
                  pl.BlockSpec((None, d, 2 * EXPERT_FF), lambda i, be, nu: (be[i], 0, 0)),
                  pl.BlockSpec((None, EXPERT_FF, d), lambda i, be, nu: (be[i], 0, 0))],
        out_specs=pl.BlockSpec((rows, LANE), lambda i, be, nu: (i, 0)),
    )
    return pl.pallas_call(
        _expert_kernel,
        grid_spec=grid_spec,
        out_shape=jax.ShapeDtypeStruct((n_blocks * rows, LANE), F32),
        compiler_params=_cparams(("arbitrary",)),
        name="routed_experts",
    )(block_expert, n_used, xs, w_gu, w_down)


def moe_slots(pick, counts, n_blocks):
    eb = EXPERT_BLOCK
    idx6 = pick[:, PICK_IDX:PICK_IDX + TOP_K].astype(jnp.int32)
    rank6 = pick[:, PICK_RANK:PICK_RANK + TOP_K].astype(jnp.int32)
    padded = (counts + eb - 1) // eb * eb
    pad_end = jnp.cumsum(padded)
    pad_start = pad_end - padded
    experts = jnp.arange(N_EXPERTS, dtype=jnp.int32)
    start6 = jnp.sum(jnp.where(idx6[:, :, None] == experts, pad_start, 0), axis=-1)
    tok_slots = (start6 + rank6).reshape(-1).astype(jnp.int32)
    blk_start = jnp.arange(n_blocks, dtype=jnp.int32) * eb
    block_expert = jnp.minimum(jnp.searchsorted(pad_end, blk_start, side="right"), N_EXPERTS - 1).astype(jnp.int32)
    n_used = (pad_end[-1:] // eb).astype(jnp.int32)
    fill_lo = jnp.concatenate([pad_start + counts, pad_end[-1:]]).astype(jnp.int32)
    fill_hi = jnp.concatenate([pad_end, jnp.full((1,), n_blocks * eb, pad_end.dtype)]).astype(jnp.int32)
    return tok_slots, block_expert, n_used, fill_lo, fill_hi


def _combine_kernel(slots_ref, ys_hbm, pick_ref, sh_ref, x1_ref, mod_ref, g_ref, b_ref, *rest,
                    tm, n_tiles, n_ctx_tiles, d, with_next):
    if with_next:
        modn_ref, x2_ref, h_ref, buf, sem = rest
    else:
        x2_ref, buf, sem = rest
    i = pl.program_id(0)
    s = i % 2
    tok_rows = TOP_K * TOK_ROWS

    def start_gather(tile, slot):
        def body(t, carry):
            for j in range(TOP_K):
                src = slots_ref[(tile * tm + t) * TOP_K + j]
                pltpu.make_async_copy(_slab(ys_hbm, src), _slab(buf.at[slot], t * TOP_K + j), sem.at[slot]).start()
            return carry
        lax.fori_loop(0, tm, body, 0)

    @pl.when(i == 0)
    def _():
        start_gather(0, 0)

    @pl.when(i + 1 < n_tiles)
    def _():
        start_gather(i + 1, 1 - s)

    pltpu.make_async_copy(ys_hbm.at[pl.ds(0, tm * tok_rows)], buf.at[s], sem.at[s]).wait()
    pieces = []
    for c in range(TOK_ROWS):
        acc = sh_ref[:, c * LANE:(c + 1) * LANE]
        for j in range(TOP_K):
            gate_j = pick_ref[:, PICK_GATE + j:PICK_GATE + j + 1]
            acc = acc + gate_j * buf[s, pl.ds(j * TOK_ROWS + c, tm, stride=tok_rows), :]
        pieces.append(acc)
    f = jnp.concatenate(pieces, axis=1)
    gate = _mod_rows(mod_ref, i, n_ctx_tiles, 5, d)
    x2 = _layer_norm(DEEPNORM_ALPHA * x1_ref[...] + gate * f, g_ref[...], b_ref[...])
    x2_ref[...] = x2
    if with_next:
        sh = _mod_rows(modn_ref, i, n_ctx_tiles, 0, d)
        sc = _mod_rows(modn_ref, i, n_ctx_tiles, 1, d)
        h_ref[...] = (x2 * (1.0 + sc) + sh).astype(h_ref.dtype)


def moe_combine(ys, tok_slots, pick, shared, x1, mod, ln_g, ln_b, n_ctx, mod_next=None):
    n, d = x1.shape
    tm = COMBINE_TILE
    tok = pl.BlockSpec((tm, d), lambda i, sl: (i, 0))
    vec = pl.BlockSpec((1, d), lambda i, sl: (0, 0))
    with_next = mod_next is not None
    in_specs = [pl.BlockSpec(memory_space=pl.ANY), pl.BlockSpec((tm, LANE), lambda i, sl: (i, 0)), tok, tok,
                pl.BlockSpec(mod.shape, lambda i, sl: (0, 0)), vec, vec]
    args = [ys, pick, shared, x1, mod, ln_g.reshape(1, d), ln_b.reshape(1, d)]
    out_specs = [tok]
    out_shape = [jax.ShapeDtypeStruct((n, d), F32)]
    if with_next:
        in_specs.append(pl.BlockSpec(mod_next.shape, lambda i, sl: (0, 0)))
        args.append(mod_next)
        out_specs.append(tok)
        out_shape.append(jax.ShapeDtypeStruct((n, d), BF16))
    grid_spec = pltpu.PrefetchScalarGridSpec(
        num_scalar_prefetch=1,
        grid=(n // tm,),
        in_specs=in_specs, out_specs=out_specs,
        scratch_shapes=[pltpu.VMEM((2, tm * TOP_K * TOK_ROWS, LANE), F32), pltpu.SemaphoreType.DMA((2,))],
    )
    return pl.pallas_call(
        functools.partial(_combine_kernel, tm=tm, n_tiles=n // tm, n_ctx_tiles=n_ctx // tm, d=d,
                          with_next=with_next),
        grid_spec=grid_spec,
        out_shape=out_shape,
        compiler_params=_cparams(("arbitrary",)),
        name="moe_combine",
    )(tok_slots, *args)


def moe_layer(f_in, f3, x1, mod, ln_g, ln_b, w_router, b_router, w_gu, w_down, ws_gu, ws_down, n_ctx,
              mod_next=None):
    n, d = f_in.shape
    wr = jnp.pad(w_router, ((0, 0), (0, LANE - N_EXPERTS)))
    wr_hi = wr.astype(BF16)
    wr_lo = (wr - wr_hi.astype(F32)).astype(BF16)
    br = jnp.pad(b_router.astype(F32), (0, LANE - N_EXPERTS)).reshape(1, LANE)
    pick, shared, cnt = router_shared(f_in, wr_hi, wr_lo, br, ws_gu.astype(BF16), ws_down.astype(BF16))
    counts = cnt[0, :N_EXPERTS].astype(jnp.int32)
    n_blocks = -(-(n * TOP_K + N_EXPERTS * (EXPERT_BLOCK - 1)) // EXPERT_BLOCK)
    tok_slots, block_expert, n_used, pad_lo, pad_hi = moe_slots(pick, counts, n_blocks)
    xs = moe_dispatch_rows(f3, tok_slots, pad_lo, pad_hi, n_blocks * EXPERT_BLOCK)
    ys = routed_experts(xs, block_expert, n_used, w_gu, w_down)
    return moe_combine(ys, tok_slots, pick, shared, x1, mod, ln_g, ln_b, n_ctx, mod_next)


def _hgrn_levels():
    t = ROW_TILE
    levels = []
    b = t // 2
    while b >= 1:
        levels.append(b)
        b //= 2
    return levels


def _hgrn_kernel(q_ref, f_ref, i_ref, lb_ref, mlev_ref, mincl_ref, o_ref, st_s):
    d = pl.program_id(0)
    j = pl.program_id(2)

    @pl.when(j == 0)
    def _():
        st_s[...] = jnp.zeros(st_s.shape, F32)

    t = ROW_TILE
    lb = lb_ref[...]
    log_lb = jnp.log(lb)
    log_rest = jnp.log1p(-lb)
    fz = f_ref[...]
    q = _silu(q_ref[...])
    v = i_ref[...]
    log_sig = jnp.minimum(fz, 0.0) - jnp.log1p(jnp.exp(-jnp.abs(fz)))
    u = log_rest + log_sig
    mx = jnp.maximum(log_lb, u)
    logf = mx + jnp.log1p(jnp.exp(-jnp.abs(log_lb - u)))
    key = (1.0 - lb) * jax.nn.sigmoid(-fz)

    cum = _mm_exact_lhs(mincl_ref[...], logf)
    cum_l = jnp.where(d == 0, cum[t - 1:t, :], cum[0:1, :])
    d_all = _mm_exact_lhs(mlev_ref[...], logf)

    row = lax.broadcasted_iota(jnp.int32, (t, 1), 0)
    ri = lax.broadcasted_iota(jnp.int32, (t, t), 0)
    ci = lax.broadcasted_iota(jnp.int32, (t, t), 1)
    a = jnp.zeros((t, t), F32)
    for li, b in enumerate(_hgrn_levels()):
        e = jnp.exp(d_all[li * t:(li + 1) * t])
        q_side = ((row // b) % 2) == jnp.where(d == 0, 1, 0)
        qt = jnp.where(q_side, q * e, 0.0)
        kt = jnp.where(q_side, 0.0, key * e)
        a = a + jnp.where((ri // (2 * b)) == (ci // (2 * b)), _mm_nt(qt, kt), 0.0)
    st = st_s[...]
    o = _mm(a, v) + _mm_nt(q * jnp.exp(cum), st) + jnp.sum(q * key, axis=1, keepdims=True) * v
    o_ref[...] = o
    st_s[...] = st * jnp.exp(cum_l) + _mm_tn(v, key * jnp.exp(cum_l - cum))


def hgrn_matrices():
    t = ROW_TILE
    ti = np.arange(t)[:, None]
    ii = np.arange(t)[None, :]
    mlev = np.zeros((2, len(_hgrn_levels()), t, t), np.float32)
    for li, b in enumerate(_hgrn_levels()):
        bnd = (ti // (2 * b)) * (2 * b) + b - 1
        upper = (ti // b) % 2 == 1
        mlev[0, li] = np.where(upper, (ii > bnd) & (ii <= ti), (ii > ti) & (ii <= bnd))
        mlev[1, li] = np.where(upper, (ii > bnd) & (ii < ti), (ii >= ti) & (ii <= bnd))
    mincl = np.stack([(ii <= ti), (ii >= ti)]).astype(np.float32)
    return (jnp.asarray(mlev.reshape(2, -1, t), BF16), jnp.asarray(mincl, BF16))


def hgrn_scan(z, lb, mlev, mincl, n_ctx):
    n = z.shape[0]
    t = ROW_TILE
    nt = n // t
    nct = n_ctx // t
    hh = C_HEADS
    srow = lambda d, h, j: _scan_tile(d, j, nt, nct)
    return pl.pallas_call(
        _hgrn_kernel,
        grid=(2, hh, nt),
        in_specs=[pl.BlockSpec((t, LANE), lambda d, h, j: (srow(d, h, j), h)),
                  pl.BlockSpec((t, LANE), lambda d, h, j: (srow(d, h, j), hh * (1 + d) + h)),
                  pl.BlockSpec((t, LANE), lambda d, h, j: (srow(d, h, j), 3 * hh + h)),
                  pl.BlockSpec((None, 1, LANE), lambda d, h, j: (h, 0, 0)),
                  pl.BlockSpec((None,) + mlev.shape[1:], lambda d, h, j: (d, 0, 0)),
                  pl.BlockSpec((None, t, t), lambda d, h, j: (d, 0, 0))],
        out_specs=pl.BlockSpec((None, t, LANE), lambda d, h, j: (d, srow(d, h, j), h)),
        out_shape=jax.ShapeDtypeStruct((2, n, hh * LANE), F32),
        scratch_shapes=[pltpu.VMEM((LANE, LANE), F32)],
        compiler_params=_cparams(("parallel", "parallel", "arbitrary")),
        name="hgrn_scan",
    )(z, z, z, lb.reshape(hh, 1, LANE), mlev, mincl)


def _hgrn_finish_kernel(o_ref, g_ref, w_ref, out_ref):
    w = w_ref[...]
    for h in range(C_HEADS):
        sl = slice(h * LANE, (h + 1) * LANE)
        o = o_ref[0, :, sl] + o_ref[1, :, sl]
        ms = jnp.mean(o * o, axis=-1, keepdims=True)
        out_ref[:, sl] = (o * lax.rsqrt(ms + 1e-5) * w * _silu(g_ref[:, sl])).astype(out_ref.dtype)


def hgrn_finish(o, z, norm_w):
    n = z.shape[0]
    t = ROW_TILE
    dm = C_HEADS * LANE
    return pl.pallas_call(
        _hgrn_finish_kernel,
        grid=(n // t,),
        in_specs=[pl.BlockSpec((2, t, dm), lambda i: (0, i, 0)),
                  pl.BlockSpec((t, dm), lambda i: (i, 4)),
                  pl.BlockSpec((1, LANE), lambda i: (0, 0))],
        out_specs=pl.BlockSpec((t, dm), lambda i: (i, 0)),
        out_shape=jax.ShapeDtypeStruct((n, dm), BF16),
        compiler_params=_cparams(("parallel",)),
        name="hgrn_finish",
    )(o, z, norm_w.reshape(1, LANE))


def rwkv_matrices():
    t = ROW_TILE
    ti = np.arange(t)[:, None]
    ii = np.arange(t)[None, :]
    same = (ti // RWKV_CHUNK) == (ii // RWKV_CHUNK)
    mincl = np.stack([same & (ii <= ti), same & (ii >= ti)]).astype(np.float32)
    mbef = np.stack([same & (ii < ti), same & (ii > ti)]).astype(np.float32)
    return (jnp.asarray(mincl, BF16), jnp.asarray(mbef, BF16), jnp.asarray(same.astype(np.float32), BF16))


def even_mixer(h, n_ctx, w_in, lam_params, subln_w, mu, w0, w2, a0, a2, g2, k_k, k_a, r_k, lnx_w, lnx_b,
               layer_idx, rope):
    n, d = h.shape
    n_lat = n - n_ctx
    hq = A_HEADS * 2 * A_QK_DIM
    a_in = 3 * hq
    cos_t, sin_t = rope
    q_scale = (A_QK_DIM ** -0.5) * math.log2(math.e)
    w_qk = jnp.concatenate([w_in[:, :hq] * q_scale, w_in[:, hq:2 * hq]], axis=1).astype(BF16)
    qk = matmul_rope(h, w_qk, cos_t, sin_t)
    v = matmul(h, w_in[:, 2 * hq:a_in].astype(BF16), BF16)
    b_in = w_in.shape[1] - a_in
    wz = 3 * B_WIDTH + 512
    w_b = jnp.pad(w_in[:, a_in:], ((0, 0), (0, wz - b_in))).astype(BF16)
    zb = matmul(h, w_b, F32)

    lam_f = lam_params.astype(F32)
    lam_init = 0.8 - 0.6 * math.exp(-0.3 * layer_idx)
    lam = jnp.exp(jnp.sum(lam_f[0] * lam_f[1])) - jnp.exp(jnp.sum(lam_f[2] * lam_f[3])) + lam_init
    q = qk[:, :hq]
    k = qk[:, hq:]
    tq = _row_tile(n_lat, (1024, 512, 256))
    tk = _row_tile(n, (1280, 640, 256))
    o_lat = diff_attention(q[n_ctx:], k, v, lam, subln_w, lam_init, tq, tk)
    o_ctx = diff_attention(q[:n_ctx], k[:n_ctx], v[:n_ctx], lam, subln_w, lam_init, n_ctx, n_ctx)
    a_out = jnp.concatenate([o_ctx, o_lat], axis=0)

    c = B_WIDTH
    mu_p = jnp.pad(mu, (0, wz - b_in)).reshape(1, wz)
    zc = jnp.zeros((B_DECAY_RANK, c), F32)
    w2cat = jnp.concatenate([jnp.concatenate([w2[0], zc], 1), jnp.concatenate([zc, w2[1]], 1)], 0)
    a2cat = jnp.concatenate([jnp.concatenate([a2[0], zc], 1), jnp.concatenate([zc, a2[1]], 1)], 0)
    w0cat = w0.reshape(1, 2 * c)
    a0cat = a0.reshape(1, 2 * c)
    g2pad = jnp.pad(g2, ((0, 256 - B_GATE_RANK), (0, 0))).astype(BF16)
    hid = np.arange(c) // B_HEAD_DIM
    ehead = jnp.asarray((hid[:, None] == hid[None, :]).astype(np.float32), BF16)
    r, vv, kkn, g, kd, lw, bd = rwkv_prep(zb, mu_p, w2cat, w0cat, a2cat, a0cat, g2pad, k_k.reshape(1, c),
                                          k_a.reshape(1, c), ehead, n_ctx)
    mincl, mbef, msame = rwkv_matrices()
    gh, qy = rwkv_chunk(r, vv, kkn, kd, lw, bd, mincl, mbef, msame)
    y = rwkv_seq(gh, qy, n_ctx)
    b_out = rwkv_finish(y, r, vv, kd, g, r_k.reshape(1, c), lnx_w.reshape(1, c), lnx_b.reshape(1, c), ehead)
    return jnp.concatenate([a_out, b_out], axis=1)


def odd_mixer(h, n_ctx, w_in, lb, norm_w):
    z = matmul(h, w_in.astype(BF16), F32)
    mlev, mincl = hgrn_matrices()
    o = hgrn_scan(z, lb, mlev, mincl, n_ctx)
    return hgrn_finish(o, z, norm_w)


def kernel(x, c, ctx, c_ctx, ada_w, ada_b, ln_g, ln_b, even_w_in, even_w_out, diff_lambda, diff_subln_w, rwkv_mu, rwkv_w0, rwkv_w2, rwkv_a0, rwkv_a2, rwkv_g2, rwkv_k_k, rwkv_k_a, rwkv_r_k, rwkv_lnx_w, rwkv_lnx_b, odd_w_in, odd_w_out, hgrn_lb_raw, hgrn_norm_w, moe_router_w, moe_router_b, moe_w_gu, moe_w_down, moe_shared_gu, moe_shared_down):
    bsz, n_lat, d = x.shape
    n_ctx = ctx.shape[1]
    assert bsz == 1 and n_ctx % ROW_TILE == 0 and n_lat % ROW_TILE == 0
    depth = ada_w.shape[0]
    xs = jnp.concatenate([ctx[0], x[0]], axis=0)
    c8 = jnp.zeros((8, d), F32).at[0].set(c[0]).at[1].set(c_ctx)
    mods = adaln(c8, ada_w, ada_b)
    lb_soft = jax.nn.softmax(hgrn_lb_raw.astype(F32), axis=0)
    lower_bounds = jnp.cumsum(lb_soft, axis=0) - lb_soft[0]
    rope = rope_tables(n_ctx, n_lat)

    h = modulate(xs, mods[0], n_ctx)
    for l in range(depth):
        mod = mods[l]
        if l % 2 == 0:
            e = l // 2
            mix = even_mixer(h, n_ctx, even_w_in[e], diff_lambda[e], diff_subln_w[e], rwkv_mu[e], rwkv_w0[e],
                             rwkv_w2[e], rwkv_a0[e], rwkv_a2[e], rwkv_g2[e], rwkv_k_k[e], rwkv_k_a[e],
                             rwkv_r_k[e], rwkv_lnx_w[e], rwkv_lnx_b[e], l, rope)
            w_out = even_w_out[e]
        else:
            o = l // 2
            mix = odd_mixer(h, n_ctx, odd_w_in[o], lower_bounds[l].reshape(C_HEADS, C_KEY_DIM), hgrn_norm_w[o])
            w_out = odd_w_out[o]
        x1, f_in, f3 = outproj_ln(mix, w_out.astype(BF16), xs, mod, ln_g[l, 0], ln_b[l, 0], n_ctx)
        mod_next = mods[l + 1] if l + 1 < depth else None
        res = moe_layer(f_in, f3, x1, mod, ln_g[l, 1], ln_b[l, 1], moe_router_w[l], moe_router_b[l], moe_w_gu[l],
                        moe_w_down[l], moe_shared_gu[l], moe_shared_down[l], n_ctx, mod_next)
        if mod_next is not None:
            xs, h = res
        else:
            (xs,) = res
    return xs[n_ctx:].reshape(bsz, n_lat, d)
```

```python
import functools
import math

import numpy as np
import jax
import jax.numpy as jnp
from jax import lax
from jax.experimental import pallas as pl
from jax.experimental.pallas import tpu as pltpu

F32 = jnp.float32
BF16 = jnp.bfloat16
HIGHEST = lax.Precision.HIGHEST

GRID_W = 64
A_HEADS = 8
A_QK_DIM = 64
A_V_DIM = 128
ROPE_BASE = 10000.0
B_HEADS = 16
B_HEAD_DIM = 64
B_WIDTH = 1024
B_DECAY_RANK = 64
B_ICL_RANK = 64
B_GATE_RANK = 160
RWKV_LN_EPS = 64e-5
C_HEADS = 16
C_KEY_DIM = 128
N_EXPERTS = 64
TOP_K = 6
EXPERT_FF = 512
SHARED_FF = 1024
ROUTE_SCALE = 2.5
DEPTH = 2
DEEPNORM_ALPHA = (2 * DEPTH) ** 0.25

ROW_TILE = 256
RWKV_STEP_HEADS = 4
RWKV_CHUNK = 64
EXPERT_BLOCK = 256
LANE = 128
TOK_ROWS = 16
COMBINE_TILE = 128
VMEM_LIMIT = 56 * 1024 * 1024


def _cparams(sem):
    return pltpu.CompilerParams(dimension_semantics=sem, vmem_limit_bytes=VMEM_LIMIT)


def _mm(a, b):
    return jnp.dot(a.astype(BF16), b.astype(BF16), preferred_element_type=F32)


def _mm_nt(a, b):
    return lax.dot_general(a.astype(BF16), b.astype(BF16), (((1,), (1,)), ((), ())),
                           preferred_element_type=F32)


def _mm_tn(a, b):
    return lax.dot_general(a.astype(BF16), b.astype(BF16), (((0,), (0,)), ((), ())),
                           preferred_element_type=F32)


def _split3(x):
    hi = x.astype(BF16)
    r1 = x - hi.astype(F32)
    mid = r1.astype(BF16)
    lo = (r1 - mid.astype(F32)).astype(BF16)
    return hi, mid, lo


def _mm_exact_lhs(m_bf16, x):
    hi, mid, lo = _split3(x)
    d = lambda t: jnp.dot(m_bf16, t, preferred_element_type=F32)
    return d(hi) + d(mid) + d(lo)


def _mm_exact_rhs(x, m_bf16):
    hi, mid, lo = _split3(x)
    d = lambda t: jnp.dot(t, m_bf16, preferred_element_type=F32)
    return d(hi) + d(mid) + d(lo)


def _silu(x):
    return x * jax.nn.sigmoid(x)


def _softplus(x):
    return jnp.maximum(x, 0.0) + jnp.log1p(jnp.exp(-jnp.abs(x)))


def _row_tile(m, cands=(1280, 640, 512, 256)):
    for t in cands:
        if m % t == 0:
            return t
    raise ValueError(f"no row tile for {m}")


def _adaln_kernel(c_ref, w_ref, b_ref, o_ref):
    s = _silu(c_ref[...])
    o_ref[...] = _mm(s, w_ref[...]) + b_ref[...]


def adaln(c8, ada_w, ada_b):
    depth, d, n6 = ada_w.shape
    tn = 512
    return pl.pallas_call(
        _adaln_kernel,
        grid=(depth, n6 // tn),
        in_specs=[pl.BlockSpec((8, d), lambda l, j: (0, 0)),
                  pl.BlockSpec((None, d, tn), lambda l, j: (l, 0, j)),
                  pl.BlockSpec((None, 1, tn), lambda l, j: (l, 0, j))],
        out_specs=pl.BlockSpec((None, 8, tn), lambda l, j: (l, 0, j)),
        out_shape=jax.ShapeDtypeStruct((depth, 8, n6), F32),
        compiler_params=_cparams(("parallel", "parallel")),
        name="adaln",
    )(c8, ada_w, ada_b.reshape(depth, 1, n6))


def _mod_rows(mod_ref, i, n_ctx_tiles, idx, d):
    row = jnp.where(i >= n_ctx_tiles, 1, 0)
    return mod_ref[pl.ds(row, 1), idx * d:(idx + 1) * d]


def _modulate_kernel(x_ref, mod_ref, o_ref, *, n_ctx_tiles, d):
    i = pl.program_id(0)
    sh = _mod_rows(mod_ref, i, n_ctx_tiles, 0, d)
    sc = _mod_rows(mod_ref, i, n_ctx_tiles, 1, d)
    o_ref[...] = (x_ref[...] * (1.0 + sc) + sh).astype(o_ref.dtype)


def modulate(x, mod, n_ctx):
    n, d = x.shape
    return pl.pallas_call(
        functools.partial(_modulate_kernel, n_ctx_tiles=(n - n_ctx) // ROW_TILE, d=d),
        grid=(n // ROW_TILE,),
        in_specs=[pl.BlockSpec((ROW_TILE, d), lambda i: (i, 0)),
                  pl.BlockSpec(mod.shape, lambda i: (0, 0))],
        out_specs=pl.BlockSpec((ROW_TILE, d), lambda i: (i, 0)),
        out_shape=jax.ShapeDtypeStruct((n, d), BF16),
        compiler_params=_cparams(("parallel",)),
        name="modulate",
    )(x, mod)


def _mm_kernel(x_ref, w_ref, o_ref):
    o_ref[...] = jnp.dot(x_ref[...], w_ref[...], preferred_element_type=F32).astype(o_ref.dtype)


def matmul(x, w, out_dtype, tn=512):
    m, k = x.shape
    n = w.shape[1]
    tm = _row_tile(m)
    return pl.pallas_call(
        _mm_kernel,
        grid=(m // tm, n // tn),
        in_specs=[pl.BlockSpec((tm, k), lambda i, j: (i, 0)),
                  pl.BlockSpec((k, tn), lambda i, j: (0, j))],
        out_specs=pl.BlockSpec((tm, tn), lambda i, j: (i, j)),
        out_shape=jax.ShapeDtypeStruct((m, n), out_dtype),
        compiler_params=_cparams(("parallel", "parallel")),
        name="matmul",
    )(x, w)


def _mm_rope_kernel(x_ref, w_ref, cos_ref, sin_ref, o_ref, *, tn):
    acc = jnp.dot(x_ref[...], w_ref[...], preferred_element_type=F32)
    cos = cos_ref[...]
    sin = sin_ref[...]
    lane = lax.broadcasted_iota(jnp.int32, cos.shape, 1)
    first = ((lane // 16) % 2) == 0
    for j in range(tn // LANE):
        blk = acc[:, j * LANE:(j + 1) * LANE]
        partner = jnp.where(first, pltpu.roll(blk, LANE - 16, 1), pltpu.roll(blk, 16, 1))
        o_ref[:, j * LANE:(j + 1) * LANE] = (blk * cos + partner * sin).astype(o_ref.dtype)


def matmul_rope(x, w, cos_t, sin_t, tn=512):
    m, k = x.shape
    n = w.shape[1]
    tm = _row_tile(m)
    return pl.pallas_call(
        functools.partial(_mm_rope_kernel, tn=tn),
        grid=(m // tm, n // tn),
        in_specs=[pl.BlockSpec((tm, k), lambda i, j: (i, 0)),
                  pl.BlockSpec((k, tn), lambda i, j: (0, j)),
                  pl.BlockSpec((tm, LANE), lambda i, j: (i, 0)),
                  pl.BlockSpec((tm, LANE), lambda i, j: (i, 0))],
        out_specs=pl.BlockSpec((tm, tn), lambda i, j: (i, j)),
        out_shape=jax.ShapeDtypeStruct((m, n), BF16),
        compiler_params=_cparams(("parallel", "parallel")),
        name="matmul_rope",
    )(x, w, cos_t, sin_t)


def rope_tables(n_ctx, n_lat):
    t = jnp.arange(n_lat)
    row = (t // GRID_W).astype(F32)
    col = (t % GRID_W).astype(F32)
    half = A_QK_DIM // 2
    inv = ROPE_BASE ** (-jnp.arange(0, half, 2, dtype=F32) / half)
    ang_r = row[:, None] * inv
    ang_c = col[:, None] * inv
    cos64 = jnp.concatenate([jnp.cos(ang_r), jnp.cos(ang_r), jnp.cos(ang_c), jnp.cos(ang_c)], axis=-1)
    sin64 = jnp.concatenate([-jnp.sin(ang_r), jnp.sin(ang_r), -jnp.sin(ang_c), jnp.sin(ang_c)], axis=-1)
    cos_l = jnp.concatenate([cos64, cos64], axis=-1)
    sin_l = jnp.concatenate([sin64, sin64], axis=-1)
    cos_t = jnp.concatenate([cos_l, jnp.ones((n_ctx, LANE), F32)], axis=0)
    sin_t = jnp.concatenate([sin_l, jnp.zeros((n_ctx, LANE), F32)], axis=0)
    return cos_t, sin_t


def _attn_kernel(lam_ref, q_ref, k_ref, v_ref, w_ref, o_ref, qq_s, m_s, acc_s, *, nk, tk, out_scale):
    ki = pl.program_id(2)

    @pl.when(ki == 0)
    def _():
        q = q_ref[...]
        lane = lax.broadcasted_iota(jnp.int32, q.shape, 1)
        zero = jnp.zeros_like(q)
        qq_s[0] = jnp.where(lane < A_QK_DIM, q, zero)
        qq_s[1] = jnp.where(lane >= A_QK_DIM, q, zero)
        m_s[...] = jnp.full(m_s.shape, -jnp.inf, F32)
        acc_s[...] = jnp.zeros(acc_s.shape, F32)

    k = k_ref[...]
    v = v_ref[...]
    v_ext = jnp.concatenate([v, jnp.ones_like(v)], axis=1)
    n_col = tk // LANE
    for mi in range(2):
        s = lax.dot_general(qq_s[mi], k, (((1,), (1,)), ((), ())), preferred_element_type=F32).astype(BF16)
        cm = s[:, 0:LANE]
        for c in range(1, n_col):
            cm = jnp.maximum(cm, s[:, c * LANE:(c + 1) * LANE])
        m_old = m_s[mi]
        m_new = jnp.maximum(m_old, jnp.max(cm.astype(F32), axis=1, keepdims=True))
        mb = m_new.astype(BF16)
        p = jnp.exp2(s - jnp.concatenate([mb] * n_col, axis=1))
        alpha = jnp.exp2(m_old - m_new)
        acc_s[mi] = (jnp.concatenate([alpha, alpha], axis=1) * acc_s[mi]
                     + jnp.dot(p, v_ext, preferred_element_type=F32))
        m_s[mi] = m_new

    @pl.when(ki == nk - 1)
    def _():
        lam = lam_ref[0]
        o0 = acc_s[0, :, :LANE] / acc_s[0, :, LANE:LANE + 1]
        o1 = acc_s[1, :, :LANE] / acc_s[1, :, LANE:LANE + 1]
        o = o0 - lam * o1
        ms = jnp.mean(o * o, axis=-1, keepdims=True)
        o_ref[...] = (o * lax.rsqrt(ms + 1e-5) * w_ref[...] * out_scale).astype(o_ref.dtype)


def diff_attention(qk, v, lam, subln_w, lam_init, tq, tk, nq_tiles, q_tile0, nk, k_tile0):
    return pl.pallas_call(
        functools.partial(_attn_kernel, nk=nk, tk=tk, out_scale=1.0 - lam_init),
        grid=(A_HEADS, nq_tiles, nk),
        in_specs=[pl.BlockSpec(memory_space=pltpu.SMEM),
                  pl.BlockSpec((tq, LANE), lambda h, i, j: (q_tile0 + i, h)),
                  pl.BlockSpec((tk, LANE), lambda h, i, j: (k_tile0 + j, A_HEADS + h)),
                  pl.BlockSpec((tk, LANE), lambda h, i, j: (k_tile0 + j, h)),
                  pl.BlockSpec((1, LANE), lambda h, i, j: (0, 0))],
        out_specs=pl.BlockSpec((tq, LANE), lambda h, i, j: (i, h)),
        out_shape=jax.ShapeDtypeStruct((nq_tiles * tq, A_HEADS * A_V_DIM), BF16),
        scratch_shapes=[pltpu.VMEM((2, tq, LANE), BF16), pltpu.VMEM((2, tq, LANE), F32),
                        pltpu.VMEM((2, tq, 2 * LANE), F32)],
        compiler_params=_cparams(("parallel", "parallel", "arbitrary")),
        name="diff_attention",
    )(lam.reshape(1), qk, qk, v, subln_w.reshape(1, LANE))


def _rwkv_prep_kernel(z_ref, zp_ref, zn_ref, mu_ref, w2_ref, w0_ref, a2_ref, a0_ref, g2_ref, kk_ref, ka_ref,
                      eh_ref, r_o, v_o, kkn_o, g_o, kd_o, lw_o, bd_o, *, n_first, n_tot):
    i = pl.program_id(0)
    z = z_ref[...]
    tm = z.shape[0]
    grow = i * tm + lax.broadcasted_iota(jnp.int32, (tm, 1), 0)
    lrow = lax.broadcasted_iota(jnp.int32, (tm, 1), 0)
    prev = jnp.where(lrow == 0, zp_ref[7:8, :], pltpu.roll(z, 1, 0))
    nxt = jnp.where(lrow == tm - 1, zn_ref[0:1, :], pltpu.roll(z, tm - 1, 0))
    has_prev = jnp.logical_and(grow != 0, grow != n_first)
    has_next = jnp.logical_and(grow != n_first - 1, grow != n_tot - 1)
    prev = jnp.where(has_prev, prev, 0.0)
    nxt = jnp.where(has_next, nxt, 0.0)
    zm = z + (0.5 * (prev + nxt) - z) * mu_ref[...]
    c = B_WIDTH
    r = zm[:, 0:c]
    k = zm[:, c:2 * c]
    v = zm[:, 2 * c:3 * c]
    wd = zm[:, 3 * c:3 * c + 128]
    ad = zm[:, 3 * c + 128:3 * c + 256]
    gd = zm[:, 3 * c + 256:3 * c + 512]
    wl = w0_ref[...] + jnp.dot(jnp.tanh(wd), w2_ref[...], precision=HIGHEST, preferred_element_type=F32)
    al = a0_ref[...] + jnp.dot(ad, a2_ref[...], precision=HIGHEST, preferred_element_type=F32)
    g = _mm(jax.nn.sigmoid(gd), g2_ref[...])
    kk = k * kk_ref[...]
    ss = _mm_exact_rhs(kk * kk, eh_ref[...])
    kkn = kk / jnp.maximum(jnp.sqrt(ss), 1e-12)
    r_o[...] = r
    v_o[...] = v
    kkn_o[...] = kkn
    g_o[...] = g
    for d in range(2):
        w_log = -_softplus(-wl[:, d * c:(d + 1) * c]) - 0.5
        lw_o[d] = -jnp.exp(w_log)
        icl = jax.nn.sigmoid(al[:, d * c:(d + 1) * c])
        kd_o[d] = k * (1.0 + (icl - 1.0) * ka_ref[...])
        bd_o[d] = kkn * icl


def rwkv_prep(zb, mu, w2cat, w0cat, a2cat, a0cat, g2pad, k_k, k_a, ehead, n_ctx):
    n, wz = zb.shape
    tm = ROW_TILE
    nt = n // tm
    c = B_WIDTH
    full = lambda a: pl.BlockSpec(a.shape, lambda i: (0,) * a.ndim)
    tok = pl.BlockSpec((tm, c), lambda i: (i, 0))
    tok2 = pl.BlockSpec((2, tm, c), lambda i: (0, i, 0))
    sd = jax.ShapeDtypeStruct
    return pl.pallas_call(
        functools.partial(_rwkv_prep_kernel, n_first=n - n_ctx, n_tot=n),
        grid=(nt,),
        in_specs=[pl.BlockSpec((tm, wz), lambda i: (i, 0)),
                  pl.BlockSpec((8, wz), lambda i: (jnp.maximum(i * (tm // 8) - 1, 0), 0)),
                  pl.BlockSpec((8, wz), lambda i: (jnp.minimum((i + 1) * (tm // 8), n // 8 - 1), 0)),
                  full(mu), full(w2cat), full(w0cat), full(a2cat), full(a0cat), full(g2pad),
                  full(k_k), full(k_a), full(ehead)],
        out_specs=[tok, tok, tok, tok, tok2, tok2, tok2],
        out_shape=[sd((n, c), F32)] * 4 + [sd((2, n, c), F32)] * 3,
        compiler_params=_cparams(("parallel",)),
        name="rwkv_prep",
    )(zb, zb, zb, mu, w2cat, w0cat, a2cat, a0cat, g2pad, k_k, k_a, ehead)


def _rwkv_chunk_kernel(r_ref, v_ref, kk_ref, kd_ref, lw_ref, bd_ref, mincl_ref, mbef_ref, msame_ref,
                       gh_ref, qy_ref):
    lw = lw_ref[...]
    mincl = mincl_ref[...]
    cum_i = _mm_exact_lhs(mincl, lw)
    cum_e = cum_i - lw
    cum_l = _mm_exact_lhs(msame_ref[...], lw)
    r = r_ref[...]
    v = v_ref[...]
    a = -kk_ref[...]
    k = kd_ref[...]
    b = bd_ref[...]
    at = a * jnp.exp(cum_e)
    rt = r * jnp.exp(cum_i)
    einv = jnp.exp(-cum_i)
    bt = b * einv
    kt = k * einv
    e_l = jnp.exp(cum_l - cum_i)
    bh = b * e_l
    kh = k * e_l
    p_l = jnp.exp(cum_l)
    incl = mincl > 0.5
    bef = mbef_ref[...] > 0.5
    hd = B_HEAD_DIM
    nch = ROW_TILE // RWKV_CHUNK
    zeros_h = jnp.zeros((ROW_TILE, hd), F32)
    eye = (lax.broadcasted_iota(jnp.int32, (RWKV_CHUNK, LANE), 0)
           == lax.broadcasted_iota(jnp.int32, (RWKV_CHUNK, LANE), 1))
    ri = lax.broadcasted_iota(jnp.int32, (ROW_TILE, ROW_TILE), 0)
    ci = lax.broadcasted_iota(jnp.int32, (ROW_TILE, ROW_TILE), 1)
    xr = ri ^ ci
    eye_t = jnp.where(ri == ci, 1.0, 0.0)
    for h in range(RWKV_STEP_HEADS):
        sl = slice(h * hd, (h + 1) * hd)
        a_ab = jnp.where(bef, _mm_nt(at[:, sl], bt[:, sl]), 0.0)
        a_ak = jnp.where(bef, _mm_nt(at[:, sl], kt[:, sl]), 0.0)
        m_rb = jnp.where(incl, _mm_nt(rt[:, sl], bt[:, sl]), 0.0)
        m_rk = jnp.where(incl, _mm_nt(rt[:, sl], kt[:, sl]), 0.0)
        vh = v[:, sl]
        t_inv = eye_t + jnp.where(xr == 1, a_ab, 0.0)
        for sh in range(1, 6):
            a_lev = jnp.where((xr >> sh) == 1, a_ab, 0.0)
            t_inv = t_inv + _mm(t_inv, _mm(a_lev, t_inv))
        x = _mm(t_inv, jnp.concatenate([at[:, sl], _mm(a_ak, vh)], axis=1))
        v0 = jnp.concatenate([zeros_h, vh], axis=1)
        qy_ref[h] = jnp.concatenate([rt[:, sl], zeros_h], axis=1) + _mm(m_rb, x) + _mm(m_rk, v0)
        for c in range(nch):
            rows = slice(c * RWKV_CHUNK, (c + 1) * RWKV_CHUNK)
            gh = _mm_tn(bh[rows, sl], x[rows]) + _mm_tn(kh[rows, sl], v0[rows])
            plr = jnp.concatenate([p_l[c * RWKV_CHUNK:c * RWKV_CHUNK + 1, sl], jnp.zeros((1, hd), F32)], axis=1)
            gh_ref[h, rows, :] = gh + jnp.where(eye, jnp.broadcast_to(plr, (RWKV_CHUNK, LANE)), 0.0)


def rwkv_chunk(r, v, kkn, kd, lw, bd, mincl, mbef, msame):
    n, c = r.shape
    tm = ROW_TILE
    wide = RWKV_STEP_HEADS * B_HEAD_DIM
    tokb = pl.BlockSpec((tm, wide), lambda d, s, p: (s, p))
    tokd = pl.BlockSpec((None, tm, wide), lambda d, s, p: (d, s, p))
    mdir = pl.BlockSpec((None, tm, tm), lambda d, s, p: (d, 0, 0))
    outb = pl.BlockSpec((None, RWKV_STEP_HEADS, tm, LANE), lambda d, s, p: (d, p, s, 0))
    sd = jax.ShapeDtypeStruct((2, B_HEADS, n, LANE), F32)
    return pl.pallas_call(
        _rwkv_chunk_kernel,
        grid=(2, n // tm, B_HEADS // RWKV_STEP_HEADS),
        in_specs=[tokb, tokb, tokb, tokd, tokd, tokd, mdir, mdir,
                  pl.BlockSpec((tm, tm), lambda d, s, p: (0, 0))],
        out_specs=[outb, outb],
        out_shape=[sd, sd],
        compiler_params=_cparams(("parallel", "parallel", "parallel")),
        name="rwkv_chunk",
    )(r, v, kkn, kd, lw, bd, mincl, mbef, msame)


def _scan_tile(d, j, nt, n_ctx_tiles):
    n_lat_tiles = nt - n_ctx_tiles
    fwd = jnp.where(j < n_ctx_tiles, n_lat_tiles + j, j - n_ctx_tiles)
    return jnp.where(d == 0, fwd, nt - 1 - j)


def _rwkv_seq_kernel(gh_ref, qy_ref, y_ref, z_s):
    d = pl.program_id(0)
    j = pl.program_id(1)

    @pl.when(j == 0)
    def _():
        z_s[...] = jnp.zeros(z_s.shape, F32)

    hd = B_HEAD_DIM
    nch = ROW_TILE // RWKV_CHUNK
    for ci in range(nch):
        c = jnp.where(d == 0, ci, nch - 1 - ci)
        off = pl.multiple_of(c * RWKV_CHUNK, RWKV_CHUNK)
        ys = []
        for h in range(B_HEADS):
            z = z_s[h]
            gh = gh_ref[h, pl.ds(off, RWKV_CHUNK), :]
            qy = qy_ref[h, pl.ds(off, RWKV_CHUNK), :]
            ys.append(_mm(qy[:, :hd], z) + qy[:, hd:])
            z_s[h] = _mm(gh[:, :hd], z) + gh[:, hd:]
        y_ref[pl.ds(off, RWKV_CHUNK), :] = jnp.concatenate(ys, axis=1)


def rwkv_seq(gh, qy, n_ctx):
    _, hh, n, _ = gh.shape
    tm = ROW_TILE
    nt = n // tm
    nct = n_ctx // tm
    inb = pl.BlockSpec((None, hh, tm, LANE), lambda d, j: (d, 0, _scan_tile(d, j, nt, nct), 0))
    return pl.pallas_call(
        _rwkv_seq_kernel,
        grid=(2, nt),
        in_specs=[inb, inb],
        out_specs=pl.BlockSpec((None, tm, B_WIDTH), lambda d, j: (d, _scan_tile(d, j, nt, nct), 0)),
        out_shape=jax.ShapeDtypeStruct((2, n, B_WIDTH), F32),
        scratch_shapes=[pltpu.VMEM((hh, B_HEAD_DIM, B_HEAD_DIM), F32)],
        compiler_params=_cparams(("parallel", "arbitrary")),
        name="rwkv_seq",
    )(gh, qy)


def _rwkv_finish_kernel(y_ref, r_ref, v_ref, kd_ref, g_ref, rk_ref, lw_ref, lb_ref, eh_ref, o_ref):
    y = y_ref[0] + y_ref[1]
    eh = eh_ref[...]
    inv = 1.0 / B_HEAD_DIM
    mu = _mm_exact_rhs(y, eh) * inv
    yc = y - mu
    var = _mm_exact_rhs(yc * yc, eh) * inv
    yn = yc * lax.rsqrt(var + RWKV_LN_EPS) * lw_ref[...] + lb_ref[...]
    r = r_ref[...]
    rk = (r * kd_ref[0] + r * kd_ref[1]) * rk_ref[...]
    bonus = _mm_exact_rhs(rk, eh) * v_ref[...]
    o_ref[...] = ((yn + bonus) * g_ref[...]).astype(o_ref.dtype)


def rwkv_finish(y, r, v, kd, g, r_k, lnx_w, lnx_b, ehead):
    n, c = r.shape
    tm = ROW_TILE
    tok = pl.BlockSpec((tm, c), lambda i: (i, 0))
    tok2 = pl.BlockSpec((2, tm, c), lambda i: (0, i, 0))
    full = lambda a: pl.BlockSpec(a.shape, lambda i: (0,) * a.ndim)
    return pl.pallas_call(
        _rwkv_finish_kernel,
        grid=(n // tm,),
        in_specs=[tok2, tok, tok, tok2, tok, full(r_k), full(lnx_w), full(lnx_b), full(ehead)],
        out_specs=tok,
        out_shape=jax.ShapeDtypeStruct((n, c), BF16),
        compiler_params=_cparams(("parallel",)),
        name="rwkv_finish",
    )(y, r, v, kd, g, r_k, lnx_w, lnx_b, ehead)


def _layer_norm(x, g, b):
    mu = jnp.mean(x, axis=-1, keepdims=True)
    xc = x - mu
    var = jnp.mean(xc * xc, axis=-1, keepdims=True)
    return xc * lax.rsqrt(var + 1e-5) * g + b


def _store_token_rows(ref, val):
    tm = val.shape[0]
    for s in range(TOK_ROWS):
        ref[pl.ds(s, tm, stride=TOK_ROWS), :] = val[:, s * LANE:(s + 1) * LANE]


def _load_token_rows(ref, tm):
    return jnp.concatenate([ref[pl.ds(s, tm, stride=TOK_ROWS), :] for s in range(TOK_ROWS)], axis=1)


def _outproj_kernel(a_ref, w_ref, x_ref, mod_ref, g_ref, b_ref, x1_ref, f_ref, f3_ref, *, n_ctx_tiles, d):
    i = pl.program_id(0)
    m = jnp.dot(a_ref[...], w_ref[...], preferred_element_type=F32)
    gate = _mod_rows(mod_ref, i, n_ctx_tiles, 2, d)
    x1 = _layer_norm(DEEPNORM_ALPHA * x_ref[...] + gate * m, g_ref[...], b_ref[...])
    x1_ref[...] = x1
    sh = _mod_rows(mod_ref, i, n_ctx_tiles, 3, d)
    sc = _mod_rows(mod_ref, i, n_ctx_tiles, 4, d)
    f = x1 * (1.0 + sc) + sh
    f_ref[...] = f
    _store_token_rows(f3_ref, f)


def outproj_ln(a, w_out, x, mod, ln_g, ln_b, n_ctx):
    n, kdim = a.shape
    d = x.shape[1]
    tm = ROW_TILE
    tok = pl.BlockSpec((tm, d), lambda i: (i, 0))
    vec = pl.BlockSpec((1, d), lambda i: (0, 0))
    return pl.pallas_call(
        functools.partial(_outproj_kernel, n_ctx_tiles=(n - n_ctx) // tm, d=d),
        grid=(n // tm,),
        in_specs=[pl.BlockSpec((tm, kdim), lambda i: (i, 0)),
                  pl.BlockSpec((kdim, d), lambda i: (0, 0)),
                  tok, pl.BlockSpec(mod.shape, lambda i: (0, 0)), vec, vec],
        out_specs=[tok, tok, pl.BlockSpec((tm * TOK_ROWS, LANE), lambda i: (i, 0))],
        out_shape=[jax.ShapeDtypeStruct((n, d), F32)] * 2 + [jax.ShapeDtypeStruct((n * TOK_ROWS, LANE), F32)],
        compiler_params=_cparams(("parallel",)),
        name="outproj_ln",
    )(a, w_out, x, mod, ln_g.reshape(1, d), ln_b.reshape(1, d))


PICK_IDX, PICK_RANK, PICK_GATE = 0, 8, 16


def _router_kernel(f_ref, wr_hi_ref, wr_lo_ref, br_ref, wsg_ref, wsd_ref, ltri_ref, pick_ref, sh_ref, cnt_ref,
                   carry_s):
    i = pl.program_id(0)

    @pl.when(i == 0)
    def _():
        carry_s[...] = jnp.zeros(carry_s.shape, F32)

    f = f_ref[...]
    f_hi = f.astype(BF16)
    f_lo = (f - f_hi.astype(F32)).astype(BF16)
    dd = lambda a, b: jnp.dot(a, b[...], preferred_element_type=F32)
    logits = dd(f_hi, wr_hi_ref) + dd(f_lo, wr_hi_ref) + dd(f_hi, wr_lo_ref)
    scores = jax.nn.sigmoid(logits)
    lane = lax.broadcasted_iota(jnp.int32, scores.shape, 1).astype(F32)
    neg = jnp.float32(-jnp.inf)
    sel = jnp.where(lane < N_EXPERTS, scores + br_ref[...], neg)
    top = jnp.zeros(scores.shape, F32)
    chosen = jnp.zeros(scores.shape, F32)
    hits, firsts = [], []
    for j in range(TOP_K):
        mx = jnp.max(sel, axis=1, keepdims=True)
        first = jnp.min(jnp.where(sel == mx, lane, float(LANE)), axis=1, keepdims=True)
        hit = lane == first
        top = jnp.where(hit, scores, top)
        chosen = jnp.where(hit, 1.0, chosen)
        sel = jnp.where(hit, neg, sel)
        hits.append(hit)
        firsts.append(first)
    gates = top / jnp.sum(top, axis=1, keepdims=True) * ROUTE_SCALE
    carry = carry_s[0:1, :]
    rank = dd(ltri_ref[...], chosen.astype(BF16)) + carry
    pick = jnp.zeros(scores.shape, F32)
    for j in range(TOP_K):
        pick = jnp.where(lane == PICK_IDX + j, firsts[j], pick)
        rank_j = jnp.sum(jnp.where(hits[j], rank, 0.0), axis=1, keepdims=True)
        pick = jnp.where(lane == PICK_RANK + j, rank_j, pick)
        gate_j = jnp.sum(jnp.where(hits[j], gates, 0.0), axis=1, keepdims=True)
        pick = jnp.where(lane == PICK_GATE + j, gate_j, pick)
    pick_ref[...] = pick
    carry_s[0:1, :] = carry + jnp.sum(chosen, axis=0, keepdims=True)
    cnt_ref[...] = carry_s[...]
    ab = dd(f_hi, wsg_ref)
    hdn = _silu(ab[:, :SHARED_FF]) * ab[:, SHARED_FF:]
    sh_ref[...] = dd(hdn.astype(BF16), wsd_ref)


def router_shared(f_in, wr_hi, wr_lo, br, ws_gu, ws_down):
    n, d = f_in.shape
    tm = ROW_TILE
    full = lambda a: pl.BlockSpec(a.shape, lambda i: (0,) * a.ndim)
    sd = jax.ShapeDtypeStruct
    ltri = jnp.asarray(np.tril(np.ones((tm, tm), np.float32), -1), BF16)
    return pl.pallas_call(
        _router_kernel,
        grid=(n // tm,),
        in_specs=[pl.BlockSpec((tm, d), lambda i: (i, 0)), full(wr_hi), full(wr_lo), full(br),
                  full(ws_gu), full(ws_down), full(ltri)],
        out_specs=[pl.BlockSpec((tm, LANE), lambda i: (i, 0)), pl.BlockSpec((tm, d), lambda i: (i, 0)),
                   pl.BlockSpec((8, LANE), lambda i: (0, 0))],
        out_shape=[sd((n, LANE), F32), sd((n, d), F32), sd((8, LANE), F32)],
        scratch_shapes=[pltpu.VMEM((8, LANE), F32)],
        compiler_params=_cparams(("arbitrary",)),
        name="router_shared",
    )(f_in, wr_hi, wr_lo, br, ws_gu, ws_down, ltri)


def _slab(ref, row):
    return ref.at[pl.ds(pl.multiple_of(row * TOK_ROWS, TOK_ROWS), TOK_ROWS)]


def _dispatch_kernel(slots_ref, padlo_ref, padhi_ref, f3_ref, xs_hbm, zrow, sem, *, tm, n_tiles):
    i = pl.program_id(0)

    def body(t, carry):
        for j in range(TOP_K):
            slot = slots_ref[(i * tm + t) * TOP_K + j]
            pltpu.make_async_copy(_slab(f3_ref, t), _slab(xs_hbm, slot), sem.at[0]).start()
        return carry
    lax.fori_loop(0, tm, body, 0)
    for _ in range(TOP_K):
        pltpu.make_async_copy(f3_ref, xs_hbm.at[pl.ds(0, tm * TOK_ROWS)], sem.at[0]).wait()

    @pl.when(i == n_tiles - 1)
    def _():
        zrow[...] = jnp.zeros(zrow.shape, F32)

        def per_expert(e, carry):
            lo = padlo_ref[e]
            hi = padhi_ref[e]

            def fill(s, c):
                pltpu.make_async_copy(zrow, _slab(xs_hbm, s), sem.at[1]).start()
                return c
            lax.fori_loop(lo, hi, fill, 0)

            def drain(s, c):
                pltpu.make_async_copy(zrow, _slab(xs_hbm, 0), sem.at[1]).wait()
                return c
            lax.fori_loop(lo, hi, drain, 0)
            return carry
        lax.fori_loop(0, N_EXPERTS + 1, per_expert, 0)


def moe_dispatch_rows(f3, tok_slots, pad_lo, pad_hi, n_slots):
    tm = ROW_TILE
    n = f3.shape[0] // TOK_ROWS
    grid_spec = pltpu.PrefetchScalarGridSpec(
        num_scalar_prefetch=3,
        grid=(n // tm,),
        in_specs=[pl.BlockSpec((tm * TOK_ROWS, LANE), lambda i, sl, lo, hi: (i, 0))],
        out_specs=pl.BlockSpec(memory_space=pl.ANY),
        scratch_shapes=[pltpu.VMEM((TOK_ROWS, LANE), F32), pltpu.SemaphoreType.DMA((2,))],
    )
    return pl.pallas_call(
        functools.partial(_dispatch_kernel, tm=tm, n_tiles=n // tm),
        grid_spec=grid_spec,
        out_shape=jax.ShapeDtypeStruct((n_slots * TOK_ROWS, LANE), F32),
        compiler_params=_cparams(("arbitrary",)),
        name="moe_dispatch",
    )(tok_slots, pad_lo, pad_hi, f3)


def _expert_kernel(be_ref, nu_ref, xs_ref, wgu_ref, wdn_ref, ys_ref, wgu_bf, wdn_bf):
    i = pl.program_id(0)

    @pl.when(jnp.logical_or(i == 0, be_ref[i] != be_ref[jnp.maximum(i - 1, 0)]))
    def _():
        wgu_bf[...] = wgu_ref[...].astype(BF16)
        wdn_bf[...] = wdn_ref[...].astype(BF16)

    @pl.when(i < nu_ref[0])
    def _():
        x = _load_token_rows(xs_ref, EXPERT_BLOCK).astype(BF16)
        ab = jnp.dot(x, wgu_bf[...], preferred_element_type=F32)
        hdn = _silu(ab[:, :EXPERT_FF]) * ab[:, EXPERT_FF:]
        y = jnp.dot(hdn.astype(BF16), wdn_bf[...], preferred_element_type=F32)
        _store_token_rows(ys_ref, y)

    @pl.when(i >= nu_ref[0])
    def _():
        ys_ref[...] = jnp.zeros(ys_ref.shape, F32)


def routed_experts(xs, block_expert, n_used, w_gu, w_down):
    d = w_gu.shape[1]
    n_blocks = block_expert.shape[0]
    rows = EXPERT_BLOCK * TOK_ROWS
    last_used = lambda i, nu: jnp.minimum(i, jnp.maximum(nu[0] - 1, 0))
    grid_spec = pltpu.PrefetchScalarGridSpec(
        num_scalar_prefetch=2,
        grid=(n_blocks,),
        in_specs=[pl.BlockSpec((rows, LANE), lambda i, be, nu: (last_used(i, nu), 0)),
                  pl.BlockSpec((None, d, 2 * EXPERT_FF), lambda i, be, nu: (be[i], 0, 0)),
                  pl.BlockSpec((None, EXPERT_FF, d), lambda i, be, nu: (be[i], 0, 0))],
        out_specs=pl.BlockSpec((rows, LANE), lambda i, be, nu: (i, 0)),
        scratch_shapes=[pltpu.VMEM((d, 2 * EXPERT_FF), BF16), pltpu.VMEM((EXPERT_FF, d), BF16)],
    )
    return pl.pallas_call(
        _expert_kernel,
        grid_spec=grid_spec,
        out_shape=jax.ShapeDtypeStruct((n_blocks * rows, LANE), F32),
        compiler_params=_cparams(("arbitrary",)),
        name="routed_experts",
    )(block_expert, n_used, xs, w_gu, w_down)


def moe_slots(pick, counts, n_blocks):
    eb = EXPERT_BLOCK
    idx6 = pick[:, PICK_IDX:PICK_IDX + TOP_K].astype(jnp.int32)
    rank6 = pick[:, PICK_RANK:PICK_RANK + TOP_K].astype(jnp.int32)
    padded = (counts + eb - 1) // eb * eb
    pad_end = jnp.cumsum(padded)
    pad_start = pad_end - padded
    experts = jnp.arange(N_EXPERTS, dtype=jnp.int32)
    start6 = jnp.sum(jnp.where(idx6[:, :, None] == experts, pad_start, 0), axis=-1)
    tok_slots = (start6 + rank6).reshape(-1).astype(jnp.int32)
    blk_start = jnp.arange(n_blocks, dtype=jnp.int32) * eb
    block_expert = jnp.minimum(jnp.sum(pad_end[None, :] <= blk_start[:, None], axis=1), N_EXPERTS - 1)
    block_expert = block_expert.astype(jnp.int32)
    n_used = (pad_end[-1:] // eb).astype(jnp.int32)
    fill_lo = jnp.concatenate([pad_start + counts, pad_end[-1:]]).astype(jnp.int32)
    fill_hi = jnp.concatenate([pad_end, jnp.full((1,), n_blocks * eb, pad_end.dtype)]).astype(jnp.int32)
    return tok_slots, block_expert, n_used, fill_lo, fill_hi


def _combine_kernel(slots_ref, ys_hbm, pick_ref, sh_ref, x1_ref, mod_ref, g_ref, b_ref, *rest,
                    tm, n_tiles, n_ctx_tiles, d, with_next):
    if with_next:
        modn_ref, x2_ref, h_ref, buf, sem = rest
    else:
        x2_ref, buf, sem = rest
    i = pl.program_id(0)
    s = i % 2
    tok_rows = TOP_K * TOK_ROWS

    def start_gather(tile, slot):
        def body(t, carry):
            for j in range(TOP_K):
                src = slots_ref[(tile * tm + t) * TOP_K + j]
                pltpu.make_async_copy(_slab(ys_hbm, src), _slab(buf.at[slot], t * TOP_K + j), sem.at[slot]).start()
            return carry
        lax.fori_loop(0, tm, body, 0)

    @pl.when(i == 0)
    def _():
        start_gather(0, 0)

    @pl.when(i + 1 < n_tiles)
    def _():
        start_gather(i + 1, 1 - s)

    pltpu.make_async_copy(ys_hbm.at[pl.ds(0, tm * tok_rows)], buf.at[s], sem.at[s]).wait()
    pieces = []
    for c in range(TOK_ROWS):
        acc = sh_ref[:, c * LANE:(c + 1) * LANE]
        for j in range(TOP_K):
            gate_j = pick_ref[:, PICK_GATE + j:PICK_GATE + j + 1]
            acc = acc + gate_j * buf[s, pl.ds(j * TOK_ROWS + c, tm, stride=tok_rows), :]
        pieces.append(acc)
    f = jnp.concatenate(pieces, axis=1)
    gate = _mod_rows(mod_ref, i, n_ctx_tiles, 5, d)
    x2 = _layer_norm(DEEPNORM_ALPHA * x1_ref[...] + gate * f, g_ref[...], b_ref[...])
    x2_ref[...] = x2
    if with_next:
        sh = _mod_rows(modn_ref, i, n_ctx_tiles, 0, d)
        sc = _mod_rows(modn_ref, i, n_ctx_tiles, 1, d)
        h_ref[...] = (x2 * (1.0 + sc) + sh).astype(h_ref.dtype)


def moe_combine(ys, tok_slots, pick, shared, x1, mod, ln_g, ln_b, n_ctx, mod_next=None):
    n, d = x1.shape
    tm = COMBINE_TILE
    tok = pl.BlockSpec((tm, d), lambda i, sl: (i, 0))
    vec = pl.BlockSpec((1, d), lambda i, sl: (0, 0))
    with_next = mod_next is not None
    in_specs = [pl.BlockSpec(memory_space=pl.ANY), pl.BlockSpec((tm, LANE), lambda i, sl: (i, 0)), tok, tok,
                pl.BlockSpec(mod.shape, lambda i, sl: (0, 0)), vec, vec]
    args = [ys, pick, shared, x1, mod, ln_g.reshape(1, d), ln_b.reshape(1, d)]
    out_specs = [tok]
    out_shape = [jax.ShapeDtypeStruct((n, d), F32)]
    if with_next:
        in_specs.append(pl.BlockSpec(mod_next.shape, lambda i, sl: (0, 0)))
        args.append(mod_next)
        out_specs.append(tok)
        out_shape.append(jax.ShapeDtypeStruct((n, d), BF16))
    grid_spec = pltpu.PrefetchScalarGridSpec(
        num_scalar_prefetch=1,
        grid=(n // tm,),
        in_specs=in_specs, out_specs=out_specs,
        scratch_shapes=[pltpu.VMEM((2, tm * TOP_K * TOK_ROWS, LANE), F32), pltpu.SemaphoreType.DMA((2,))],
    )
    return pl.pallas_call(
        functools.partial(_combine_kernel, tm=tm, n_tiles=n // tm, n_ctx_tiles=(n - n_ctx) // tm, d=d,
                          with_next=with_next),
        grid_spec=grid_spec,
        out_shape=out_shape,
        compiler_params=_cparams(("arbitrary",)),
        name="moe_combine",
    )(tok_slots, *args)


def moe_layer(f_in, f3, x1, mod, ln_g, ln_b, w_router, b_router, w_gu, w_down, ws_gu, ws_down, n_ctx,
              mod_next=None):
    n, d = f_in.shape
    wr = jnp.pad(w_router, ((0, 0), (0, LANE - N_EXPERTS)))
    wr_hi = wr.astype(BF16)
    wr_lo = (wr - wr_hi.astype(F32)).astype(BF16)
    br = jnp.pad(b_router.astype(F32), (0, LANE - N_EXPERTS)).reshape(1, LANE)
    pick, shared, cnt = router_shared(f_in, wr_hi, wr_lo, br, ws_gu.astype(BF16), ws_down.astype(BF16))
    counts = cnt[0, :N_EXPERTS].astype(jnp.int32)
    n_blocks = -(-(n * TOP_K + N_EXPERTS * (EXPERT_BLOCK - 1)) // EXPERT_BLOCK)
    tok_slots, block_expert, n_used, pad_lo, pad_hi = moe_slots(pick, counts, n_blocks)
    xs = moe_dispatch_rows(f3, tok_slots, pad_lo, pad_hi, n_blocks * EXPERT_BLOCK)
    ys = routed_experts(xs, block_expert, n_used, w_gu, w_down)
    return moe_combine(ys, tok_slots, pick, shared, x1, mod, ln_g, ln_b, n_ctx, mod_next)


HGRN_SMALL_LEVELS = (4, 2)


def _hgrn_levels():
    t = ROW_TILE
    levels = []
    b = t // 2
    while b >= 1:
        levels.append(b)
        b //= 2
    return levels


def _hgrn_kernel(q_ref, f_ref, i_ref, lb_ref, mlev_ref, mincl_ref, o_ref, st_s, cum_s):
    d = pl.program_id(0)
    j = pl.program_id(2)

    @pl.when(j == 0)
    def _():
        st_s[...] = jnp.zeros(st_s.shape, F32)

    t = ROW_TILE
    lb = lb_ref[...]
    log_lb = jnp.log(lb)
    log_rest = jnp.log1p(-lb)
    fz = f_ref[...]
    q = _silu(q_ref[...])
    v = i_ref[...]
    log_sig = jnp.minimum(fz, 0.0) - jnp.log1p(jnp.exp(-jnp.abs(fz)))
    u = log_rest + log_sig
    mx = jnp.maximum(log_lb, u)
    logf = mx + jnp.log1p(jnp.exp(-jnp.abs(log_lb - u)))
    key = (1.0 - lb) * jax.nn.sigmoid(-fz)

    cum = _mm_exact_lhs(mincl_ref[...], logf)
    cum_l = jnp.where(d == 0, cum[t - 1:t, :], cum[0:1, :])
    cum_s[...] = cum
    d_small = _mm_exact_lhs(mlev_ref[...], logf)

    row = lax.broadcasted_iota(jnp.int32, (t, 1), 0)
    xr = (lax.broadcasted_iota(jnp.int32, (t, t), 0) ^ lax.broadcasted_iota(jnp.int32, (t, t), 1))
    a = jnp.zeros((t, t), F32)
    for li, b in enumerate(_hgrn_levels()):
        q_side = ((row // b) % 2) == jnp.where(d == 0, 1, 0)
        if b >= 8:
            piv = jnp.concatenate(
                [jnp.broadcast_to(cum_s[pl.ds(g * 2 * b + b - 1 + d, 1), :], (2 * b, LANE))
                 for g in range(t // (2 * b))], axis=0)
            dlev = jnp.where(q_side, cum - piv, piv - cum)
        elif b > 1:
            k = HGRN_SMALL_LEVELS.index(b)
            dlev = d_small[k * t:(k + 1) * t]
        else:
            dlev = jnp.where(q_side, logf, 0.0)
        e = jnp.exp(dlev)
        qt = jnp.where(q_side, q * e, 0.0)
        kt = jnp.where(q_side, 0.0, key * e)
        same_group = (xr >> (b.bit_length())) == 0
        a = a + jnp.where(same_group, _mm_nt(qt, kt), 0.0)
    st = st_s[...]
    o = _mm(a, v) + _mm_nt(q * jnp.exp(cum), st) + jnp.sum(q * key, axis=1, keepdims=True) * v
    o_ref[...] = o
    st_s[...] = st * jnp.exp(cum_l) + _mm_tn(v, key * jnp.exp(cum_l - cum))


def hgrn_matrices():
    t = ROW_TILE
    ti = np.arange(t)[:, None]
    ii = np.arange(t)[None, :]
    mlev = np.zeros((2, len(HGRN_SMALL_LEVELS), t, t), np.float32)
    for li, b in enumerate(HGRN_SMALL_LEVELS):
        bnd = (ti // (2 * b)) * (2 * b) + b - 1
        upper = (ti // b) % 2 == 1
        mlev[0, li] = np.where(upper, (ii > bnd) & (ii <= ti), (ii > ti) & (ii <= bnd))
        mlev[1, li] = np.where(upper, (ii > bnd) & (ii < ti), (ii >= ti) & (ii <= bnd))
    mincl = np.stack([(ii <= ti), (ii >= ti)]).astype(np.float32)
    return (jnp.asarray(mlev.reshape(2, -1, t), BF16), jnp.asarray(mincl, BF16))


def hgrn_scan(z, lb, mlev, mincl, n_ctx):
    n = z.shape[0]
    t = ROW_TILE
    nt = n // t
    nct = n_ctx // t
    hh = C_HEADS
    srow = lambda d, h, j: _scan_tile(d, j, nt, nct)
    return pl.pallas_call(
        _hgrn_kernel,
        grid=(2, hh, nt),
        in_specs=[pl.BlockSpec((t, LANE), lambda d, h, j: (srow(d, h, j), h)),
                  pl.BlockSpec((t, LANE), lambda d, h, j: (srow(d, h, j), hh * (1 + d) + h)),
                  pl.BlockSpec((t, LANE), lambda d, h, j: (srow(d, h, j), 3 * hh + h)),
                  pl.BlockSpec((None, 1, LANE), lambda d, h, j: (h, 0, 0)),
                  pl.BlockSpec((None,) + mlev.shape[1:], lambda d, h, j: (d, 0, 0)),
                  pl.BlockSpec((None, t, t), lambda d, h, j: (d, 0, 0))],
        out_specs=pl.BlockSpec((None, t, LANE), lambda d, h, j: (d, srow(d, h, j), h)),
        out_shape=jax.ShapeDtypeStruct((2, n, hh * LANE), F32),
        scratch_shapes=[pltpu.VMEM((LANE, LANE), F32), pltpu.VMEM((t, LANE), F32)],
        compiler_params=_cparams(("parallel", "parallel", "arbitrary")),
        name="hgrn_scan",
    )(z, z, z, lb.reshape(hh, 1, LANE), mlev, mincl)


def _hgrn_finish_kernel(o_ref, g_ref, w_ref, out_ref):
    w = w_ref[...]
    for h in range(C_HEADS):
        sl = slice(h * LANE, (h + 1) * LANE)
        o = o_ref[0, :, sl] + o_ref[1, :, sl]
        ms = jnp.mean(o * o, axis=-1, keepdims=True)
        out_ref[:, sl] = (o * lax.rsqrt(ms + 1e-5) * w * _silu(g_ref[:, sl])).astype(out_ref.dtype)


def hgrn_finish(o, z, norm_w):
    n = z.shape[0]
    t = ROW_TILE
    dm = C_HEADS * LANE
    return pl.pallas_call(
        _hgrn_finish_kernel,
        grid=(n // t,),
        in_specs=[pl.BlockSpec((2, t, dm), lambda i: (0, i, 0)),
                  pl.BlockSpec((t, dm), lambda i: (i, 4)),
                  pl.BlockSpec((1, LANE), lambda i: (0, 0))],
        out_specs=pl.BlockSpec((t, dm), lambda i: (i, 0)),
        out_shape=jax.ShapeDtypeStruct((n, dm), BF16),
        compiler_params=_cparams(("parallel",)),
        name="hgrn_finish",
    )(o, z, norm_w.reshape(1, LANE))


def rwkv_matrices():
    t = ROW_TILE
    ti = np.arange(t)[:, None]
    ii = np.arange(t)[None, :]
    same = (ti // RWKV_CHUNK) == (ii // RWKV_CHUNK)
    mincl = np.stack([same & (ii <= ti), same & (ii >= ti)]).astype(np.float32)
    mbef = np.stack([same & (ii < ti), same & (ii > ti)]).astype(np.float32)
    return (jnp.asarray(mincl, BF16), jnp.asarray(mbef, BF16), jnp.asarray(same.astype(np.float32), BF16))


def even_mixer(h, n_ctx, w_in, lam_params, subln_w, mu, w0, w2, a0, a2, g2, k_k, k_a, r_k, lnx_w, lnx_b,
               layer_idx, rope):
    n, d = h.shape
    n_lat = n - n_ctx
    hq = A_HEADS * 2 * A_QK_DIM
    a_in = 3 * hq
    cos_t, sin_t = rope
    q_scale = (A_QK_DIM ** -0.5) * math.log2(math.e)
    w_qk = jnp.concatenate([w_in[:, :hq] * q_scale, w_in[:, hq:2 * hq]], axis=1).astype(BF16)
    qk = matmul_rope(h, w_qk, cos_t, sin_t)
    v = matmul(h, w_in[:, 2 * hq:a_in].astype(BF16), BF16)
    b_in = w_in.shape[1] - a_in
    wz = 3 * B_WIDTH + 512
    w_b = jnp.pad(w_in[:, a_in:], ((0, 0), (0, wz - b_in))).astype(BF16)
    zb = matmul(h, w_b, F32)

    lam_f = lam_params.astype(F32)
    lam_init = 0.8 - 0.6 * math.exp(-0.3 * layer_idx)
    lam = jnp.exp(jnp.sum(lam_f[0] * lam_f[1])) - jnp.exp(jnp.sum(lam_f[2] * lam_f[3])) + lam_init
    tq = _row_tile(n_lat, (1024, 512, 256))
    tk = _row_tile(n, (1280, 640, 256))
    o_lat = diff_attention(qk, v, lam, subln_w, lam_init, tq, tk, n_lat // tq, 0, n // tk, 0)
    o_ctx = diff_attention(qk, v, lam, subln_w, lam_init, n_ctx, n_ctx, 1, n_lat // n_ctx, 1, n_lat // n_ctx)
    a_out = jnp.concatenate([o_lat, o_ctx], axis=0)

    c = B_WIDTH
    mu_p = jnp.pad(mu, (0, wz - b_in)).reshape(1, wz)
    zc = jnp.zeros((B_DECAY_RANK, c), F32)
    w2cat = jnp.concatenate([jnp.concatenate([w2[0], zc], 1), jnp.concatenate([zc, w2[1]], 1)], 0)
    a2cat = jnp.concatenate([jnp.concatenate([a2[0], zc], 1), jnp.concatenate([zc, a2[1]], 1)], 0)
    w0cat = w0.reshape(1, 2 * c)
    a0cat = a0.reshape(1, 2 * c)
    g2pad = jnp.pad(g2, ((0, 256 - B_GATE_RANK), (0, 0))).astype(BF16)
    hid = np.arange(c) // B_HEAD_DIM
    ehead = jnp.asarray((hid[:, None] == hid[None, :]).astype(np.float32), BF16)
    r, vv, kkn, g, kd, lw, bd = rwkv_prep(zb, mu_p, w2cat, w0cat, a2cat, a0cat, g2pad, k_k.reshape(1, c),
                                          k_a.reshape(1, c), ehead, n_ctx)
    mincl, mbef, msame = rwkv_matrices()
    gh, qy = rwkv_chunk(r, vv, kkn, kd, lw, bd, mincl, mbef, msame)
    y = rwkv_seq(gh, qy, n_ctx)
    b_out = rwkv_finish(y, r, vv, kd, g, r_k.reshape(1, c), lnx_w.reshape(1, c), lnx_b.reshape(1, c), ehead)
    return jnp.concatenate([a_out, b_out], axis=1)


def odd_mixer(h, n_ctx, w_in, lb, norm_w):
    z = matmul(h, w_in.astype(BF16), F32)
    mlev, mincl = hgrn_matrices()
    o = hgrn_scan(z, lb, mlev, mincl, n_ctx)
    return hgrn_finish(o, z, norm_w)


def kernel(x, c, ctx, c_ctx, ada_w, ada_b, ln_g, ln_b, even_w_in, even_w_out, diff_lambda, diff_subln_w, rwkv_mu, rwkv_w0, rwkv_w2, rwkv_a0, rwkv_a2, rwkv_g2, rwkv_k_k, rwkv_k_a, rwkv_r_k, rwkv_lnx_w, rwkv_lnx_b, odd_w_in, odd_w_out, hgrn_lb_raw, hgrn_norm_w, moe_router_w, moe_router_b, moe_w_gu, moe_w_down, moe_shared_gu, moe_shared_down):
    bsz, n_lat, d = x.shape
    n_ctx = ctx.shape[1]
    assert bsz == 1 and n_ctx % ROW_TILE == 0 and n_lat % ROW_TILE == 0
    depth = ada_w.shape[0]
    xs = jnp.concatenate([x[0], ctx[0]], axis=0)
    c8 = jnp.zeros((8, d), F32).at[0].set(c[0]).at[1].set(c_ctx)
    mods = adaln(c8, ada_w, ada_b)
    lb_soft = jax.nn.softmax(hgrn_lb_raw.astype(F32), axis=0)
    lower_bounds = jnp.cumsum(lb_soft, axis=0) - lb_soft[0]
    rope = rope_tables(n_ctx, n_lat)

    h = modulate(xs, mods[0], n_ctx)
    for l in range(depth):
        mod = mods[l]
        if l % 2 == 0:
            e = l // 2
            mix = even_mixer(h, n_ctx, even_w_in[e], diff_lambda[e], diff_subln_w[e], rwkv_mu[e], rwkv_w0[e],
                             rwkv_w2[e], rwkv_a0[e], rwkv_a2[e], rwkv_g2[e], rwkv_k_k[e], rwkv_k_a[e],
                             rwkv_r_k[e], rwkv_lnx_w[e], rwkv_lnx_b[e], l, rope)
            w_out = even_w_out[e]
        else:
            o = l // 2
            mix = odd_mixer(h, n_ctx, odd_w_in[o], lower_bounds[l].reshape(C_HEADS, C_KEY_DIM), hgrn_norm_w[o])
            w_out = odd_w_out[o]
        x1, f_in, f3 = outproj_ln(mix, w_out.astype(BF16), xs, mod, ln_g[l, 0], ln_b[l, 0], n_ctx)
        mod_next = mods[l + 1] if l + 1 < depth else None
        res = moe_layer(f_in, f3, x1, mod, ln_g[l, 1], ln_b[l, 1], moe_router_w[l], moe_router_b[l], moe_w_gu[l],
                        moe_w_down[l], moe_shared_gu[l], moe_shared_down[l], n_ctx, mod_next)
        if mod_next is not None:
            xs, h = res
        else:
            (xs,) = res
    return xs[:n_lat].reshape(bsz, n_lat, d)
```

```python
import functools
import math

import numpy as np
import jax
import jax.numpy as jnp
from jax import lax
from jax.experimental import pallas as pl
from jax.experimental.pallas import tpu as pltpu

F32 = jnp.float32
BF16 = jnp.bfloat16
HIGHEST = lax.Precision.HIGHEST

GRID_W = 64
A_HEADS = 8
A_QK_DIM = 64
A_V_DIM = 128
ROPE_BASE = 10000.0
B_HEADS = 16
B_HEAD_DIM = 64
B_WIDTH = 1024
B_DECAY_RANK = 64
B_ICL_RANK = 64
B_GATE_RANK = 160
RWKV_LN_EPS = 64e-5
C_HEADS = 16
C_KEY_DIM = 128
N_EXPERTS = 64
TOP_K = 6
EXPERT_FF = 512
SHARED_FF = 1024
ROUTE_SCALE = 2.5
DEPTH = 2
DEEPNORM_ALPHA = (2 * DEPTH) ** 0.25

ROW_TILE = 256
RWKV_STEP_HEADS = 4
RWKV_CHUNK = 64
EXPERT_BLOCK = 256
LANE = 128
TOK_ROWS = 16
COMBINE_TILE = 128
VMEM_LIMIT = 56 * 1024 * 1024


def _cparams(sem):
    return pltpu.CompilerParams(dimension_semantics=sem, vmem_limit_bytes=VMEM_LIMIT)


def _mm(a, b):
    return jnp.dot(a.astype(BF16), b.astype(BF16), preferred_element_type=F32)


def _mm_nt(a, b):
    return lax.dot_general(a.astype(BF16), b.astype(BF16), (((1,), (1,)), ((), ())),
                           preferred_element_type=F32)


def _mm_tn(a, b):
    return lax.dot_general(a.astype(BF16), b.astype(BF16), (((0,), (0,)), ((), ())),
                           preferred_element_type=F32)


def _split3(x):
    hi = x.astype(BF16)
    r1 = x - hi.astype(F32)
    mid = r1.astype(BF16)
    lo = (r1 - mid.astype(F32)).astype(BF16)
    return hi, mid, lo


def _mm_exact_lhs(m_bf16, x):
    hi, mid, lo = _split3(x)
    d = lambda t: jnp.dot(m_bf16, t, preferred_element_type=F32)
    return d(hi) + d(mid) + d(lo)


def _mm_exact_rhs(x, m_bf16):
    hi, mid, lo = _split3(x)
    d = lambda t: jnp.dot(t, m_bf16, preferred_element_type=F32)
    return d(hi) + d(mid) + d(lo)


def _silu(x):
    return x * jax.nn.sigmoid(x)


def _softplus(x):
    return jnp.maximum(x, 0.0) + jnp.log1p(jnp.exp(-jnp.abs(x)))


def _row_tile(m, cands=(1280, 640, 512, 256)):
    for t in cands:
        if m % t == 0:
            return t
    raise ValueError(f"no row tile for {m}")


def _adaln_kernel(c_ref, w_ref, b_ref, o_ref):
    s = _silu(c_ref[...])
    o_ref[...] = _mm(s, w_ref[...]) + b_ref[...]


def adaln(c8, ada_w, ada_b):
    depth, d, n6 = ada_w.shape
    tn = 512
    return pl.pallas_call(
        _adaln_kernel,
        grid=(depth, n6 // tn),
        in_specs=[pl.BlockSpec((8, d), lambda l, j: (0, 0)),
                  pl.BlockSpec((None, d, tn), lambda l, j: (l, 0, j)),
                  pl.BlockSpec((None, 1, tn), lambda l, j: (l, 0, j))],
        out_specs=pl.BlockSpec((None, 8, tn), lambda l, j: (l, 0, j)),
        out_shape=jax.ShapeDtypeStruct((depth, 8, n6), F32),
        compiler_params=_cparams(("parallel", "parallel")),
        name="adaln",
    )(c8, ada_w, ada_b.reshape(depth, 1, n6))


def _mod_rows(mod_ref, i, n_ctx_tiles, idx, d):
    row = jnp.where(i >= n_ctx_tiles, 1, 0)
    return mod_ref[pl.ds(row, 1), idx * d:(idx + 1) * d]


def _modulate_kernel(x_ref, mod_ref, o_ref, *, n_ctx_tiles, d):
    i = pl.program_id(0)
    sh = _mod_rows(mod_ref, i, n_ctx_tiles, 0, d)
    sc = _mod_rows(mod_ref, i, n_ctx_tiles, 1, d)
    o_ref[...] = (x_ref[...] * (1.0 + sc) + sh).astype(o_ref.dtype)


def modulate(x, mod, n_ctx):
    n, d = x.shape
    return pl.pallas_call(
        functools.partial(_modulate_kernel, n_ctx_tiles=(n - n_ctx) // ROW_TILE, d=d),
        grid=(n // ROW_TILE,),
        in_specs=[pl.BlockSpec((ROW_TILE, d), lambda i: (i, 0)),
                  pl.BlockSpec(mod.shape, lambda i: (0, 0))],
        out_specs=pl.BlockSpec((ROW_TILE, d), lambda i: (i, 0)),
        out_shape=jax.ShapeDtypeStruct((n, d), BF16),
        compiler_params=_cparams(("parallel",)),
        name="modulate",
    )(x, mod)


def _mm_kernel(x_ref, w_ref, o_ref):
    o_ref[...] = jnp.dot(x_ref[...], w_ref[...], preferred_element_type=F32).astype(o_ref.dtype)


def matmul(x, w, out_dtype, tn=512):
    m, k = x.shape
    n = w.shape[1]
    tm = _row_tile(m)
    return pl.pallas_call(
        _mm_kernel,
        grid=(m // tm, n // tn),
        in_specs=[pl.BlockSpec((tm, k), lambda i, j: (i, 0)),
                  pl.BlockSpec((k, tn), lambda i, j: (0, j))],
        out_specs=pl.BlockSpec((tm, tn), lambda i, j: (i, j)),
        out_shape=jax.ShapeDtypeStruct((m, n), out_dtype),
        compiler_params=_cparams(("parallel", "parallel")),
        name="matmul",
    )(x, w)


def _mm_rope_kernel(x_ref, w_ref, cos_ref, sin_ref, o_ref, *, tn):
    acc = jnp.dot(x_ref[...], w_ref[...], preferred_element_type=F32)
    cos = cos_ref[...]
    sin = sin_ref[...]
    lane = lax.broadcasted_iota(jnp.int32, cos.shape, 1)
    first = ((lane // 16) % 2) == 0
    for j in range(tn // LANE):
        blk = acc[:, j * LANE:(j + 1) * LANE]
        partner = jnp.where(first, pltpu.roll(blk, LANE - 16, 1), pltpu.roll(blk, 16, 1))
        o_ref[:, j * LANE:(j + 1) * LANE] = (blk * cos + partner * sin).astype(o_ref.dtype)


def matmul_rope(x, w, cos_t, sin_t, tn=512):
    m, k = x.shape
    n = w.shape[1]
    tm = _row_tile(m)
    return pl.pallas_call(
        functools.partial(_mm_rope_kernel, tn=tn),
        grid=(m // tm, n // tn),
        in_specs=[pl.BlockSpec((tm, k), lambda i, j: (i, 0)),
                  pl.BlockSpec((k, tn), lambda i, j: (0, j)),
                  pl.BlockSpec((tm, LANE), lambda i, j: (i, 0)),
                  pl.BlockSpec((tm, LANE), lambda i, j: (i, 0))],
        out_specs=pl.BlockSpec((tm, tn), lambda i, j: (i, j)),
        out_shape=jax.ShapeDtypeStruct((m, n), BF16),
        compiler_params=_cparams(("parallel", "parallel")),
        name="matmul_rope",
    )(x, w, cos_t, sin_t)


def rope_tables(n_ctx, n_lat):
    t = jnp.arange(n_lat)
    row = (t // GRID_W).astype(F32)
    col = (t % GRID_W).astype(F32)
    half = A_QK_DIM // 2
    inv = ROPE_BASE ** (-jnp.arange(0, half, 2, dtype=F32) / half)
    ang_r = row[:, None] * inv
    ang_c = col[:, None] * inv
    cos64 = jnp.concatenate([jnp.cos(ang_r), jnp.cos(ang_r), jnp.cos(ang_c), jnp.cos(ang_c)], axis=-1)
    sin64 = jnp.concatenate([-jnp.sin(ang_r), jnp.sin(ang_r), -jnp.sin(ang_c), jnp.sin(ang_c)], axis=-1)
    cos_l = jnp.concatenate([cos64, cos64], axis=-1)
    sin_l = jnp.concatenate([sin64, sin64], axis=-1)
    cos_t = jnp.concatenate([cos_l, jnp.ones((n_ctx, LANE), F32)], axis=0)
    sin_t = jnp.concatenate([sin_l, jnp.zeros((n_ctx, LANE), F32)], axis=0)
    return cos_t, sin_t


ATTN_ROW_SPLITS = 2


def _attn_kernel(lam_ref, q_ref, k_ref, v_ref, w_ref, o_ref, qq_s, m_s, acc_s, *, nk, tk, out_scale):
    ki = pl.program_id(2)

    @pl.when(ki == 0)
    def _():
        q = q_ref[...]
        lane = lax.broadcasted_iota(jnp.int32, q.shape, 1)
        zero = jnp.zeros_like(q)
        qq_s[0] = jnp.where(lane < A_QK_DIM, q, zero)
        qq_s[1] = jnp.where(lane >= A_QK_DIM, q, zero)
        m_s[...] = jnp.full(m_s.shape, -jnp.inf, F32)
        acc_s[...] = jnp.zeros(acc_s.shape, F32)

    k = k_ref[...]
    v = v_ref[...]
    v_ext = jnp.concatenate([v, jnp.ones_like(v)], axis=1)
    n_col = tk // LANE
    part = qq_s.shape[1] // ATTN_ROW_SPLITS
    chains = [(mi, slice(rp * part, (rp + 1) * part)) for mi in range(2) for rp in range(ATTN_ROW_SPLITS)]

    def scores(c):
        mi, rows = chains[c]
        return lax.dot_general(qq_s[mi, rows, :], k, (((1,), (1,)), ((), ())),
                               preferred_element_type=F32).astype(BF16)

    def softmax(c, s):
        mi, rows = chains[c]
        cm = s[:, 0:LANE]
        for cc in range(1, n_col):
            cm = jnp.maximum(cm, s[:, cc * LANE:(cc + 1) * LANE])
        m_old = m_s[mi, rows, :]
        m_new = jnp.maximum(m_old, jnp.max(cm.astype(F32), axis=1, keepdims=True))
        m_s[mi, rows, :] = m_new
        mb = m_new.astype(BF16)
        return jnp.exp2(s - jnp.concatenate([mb] * n_col, axis=1)), jnp.exp2(m_old - m_new)

    def accumulate(c, p, alpha):
        mi, rows = chains[c]
        acc_s[mi, rows, :] = (jnp.concatenate([alpha, alpha], axis=1) * acc_s[mi, rows, :]
                              + jnp.dot(p, v_ext, preferred_element_type=F32))

    n_ch = len(chains)
    s, pa = {}, {}
    for step in range(n_ch + 2):
        if step < n_ch:
            s[step] = scores(step)
        if 1 <= step <= n_ch:
            pa[step - 1] = softmax(step - 1, s.pop(step - 1))
        if 2 <= step:
            accumulate(step - 2, *pa.pop(step - 2))

    @pl.when(ki == nk - 1)
    def _():
        lam = lam_ref[0]
        o0 = acc_s[0, :, :LANE] / acc_s[0, :, LANE:LANE + 1]
        o1 = acc_s[1, :, :LANE] / acc_s[1, :, LANE:LANE + 1]
        o = o0 - lam * o1
        ms = jnp.mean(o * o, axis=-1, keepdims=True)
        o_ref[...] = (o * lax.rsqrt(ms + 1e-5) * w_ref[...] * out_scale).astype(o_ref.dtype)


def diff_attention(qk, v, lam, subln_w, lam_init, tq, tk, nq_tiles, q_tile0, nk, k_tile0):
    return pl.pallas_call(
        functools.partial(_attn_kernel, nk=nk, tk=tk, out_scale=1.0 - lam_init),
        grid=(A_HEADS, nq_tiles, nk),
        in_specs=[pl.BlockSpec(memory_space=pltpu.SMEM),
                  pl.BlockSpec((tq, LANE), lambda h, i, j: (q_tile0 + i, h)),
                  pl.BlockSpec((tk, LANE), lambda h, i, j: (k_tile0 + j, A_HEADS + h)),
                  pl.BlockSpec((tk, LANE), lambda h, i, j: (k_tile0 + j, h)),
                  pl.BlockSpec((1, LANE), lambda h, i, j: (0, 0))],
        out_specs=pl.BlockSpec((tq, LANE), lambda h, i, j: (i, h)),
        out_shape=jax.ShapeDtypeStruct((nq_tiles * tq, A_HEADS * A_V_DIM), BF16),
        scratch_shapes=[pltpu.VMEM((2, tq, LANE), BF16), pltpu.VMEM((2, tq, LANE), F32),
                        pltpu.VMEM((2, tq, 2 * LANE), F32)],
        compiler_params=_cparams(("parallel", "parallel", "arbitrary")),
        name="diff_attention",
    )(lam.reshape(1), qk, qk, v, subln_w.reshape(1, LANE))


def _rwkv_prep_kernel(z_ref, zp_ref, zn_ref, mu_ref, w2_ref, w0_ref, a2_ref, a0_ref, g2_ref, kk_ref, ka_ref,
                      eh_ref, r_o, v_o, kkn_o, g_o, kd_o, lw_o, bd_o, *, n_first, n_tot):
    i = pl.program_id(0)
    z = z_ref[...]
    tm = z.shape[0]
    grow = i * tm + lax.broadcasted_iota(jnp.int32, (tm, 1), 0)
    lrow = lax.broadcasted_iota(jnp.int32, (tm, 1), 0)
    prev = jnp.where(lrow == 0, zp_ref[7:8, :], pltpu.roll(z, 1, 0))
    nxt = jnp.where(lrow == tm - 1, zn_ref[0:1, :], pltpu.roll(z, tm - 1, 0))
    has_prev = jnp.logical_and(grow != 0, grow != n_first)
    has_next = jnp.logical_and(grow != n_first - 1, grow != n_tot - 1)
    prev = jnp.where(has_prev, prev, 0.0)
    nxt = jnp.where(has_next, nxt, 0.0)
    zm = z + (0.5 * (prev + nxt) - z) * mu_ref[...]
    c = B_WIDTH
    r = zm[:, 0:c]
    k = zm[:, c:2 * c]
    v = zm[:, 2 * c:3 * c]
    wd = zm[:, 3 * c:3 * c + 128]
    ad = zm[:, 3 * c + 128:3 * c + 256]
    gd = zm[:, 3 * c + 256:3 * c + 512]
    wl = w0_ref[...] + jnp.dot(jnp.tanh(wd), w2_ref[...], precision=HIGHEST, preferred_element_type=F32)
    al = a0_ref[...] + jnp.dot(ad, a2_ref[...], precision=HIGHEST, preferred_element_type=F32)
    g = _mm(jax.nn.sigmoid(gd), g2_ref[...])
    kk = k * kk_ref[...]
    ss = _mm_exact_rhs(kk * kk, eh_ref[...])
    kkn = kk / jnp.maximum(jnp.sqrt(ss), 1e-12)
    r_o[...] = r
    v_o[...] = v
    kkn_o[...] = kkn
    g_o[...] = g
    for d in range(2):
        w_log = -_softplus(-wl[:, d * c:(d + 1) * c]) - 0.5
        lw_o[d] = -jnp.exp(w_log)
        icl = jax.nn.sigmoid(al[:, d * c:(d + 1) * c])
        kd_o[d] = k * (1.0 + (icl - 1.0) * ka_ref[...])
        bd_o[d] = kkn * icl


def rwkv_prep(zb, mu, w2cat, w0cat, a2cat, a0cat, g2pad, k_k, k_a, ehead, n_ctx):
    n, wz = zb.shape
    tm = ROW_TILE
    nt = n // tm
    c = B_WIDTH
    full = lambda a: pl.BlockSpec(a.shape, lambda i: (0,) * a.ndim)
    tok = pl.BlockSpec((tm, c), lambda i: (i, 0))
    tok2 = pl.BlockSpec((2, tm, c), lambda i: (0, i, 0))
    sd = jax.ShapeDtypeStruct
    return pl.pallas_call(
        functools.partial(_rwkv_prep_kernel, n_first=n - n_ctx, n_tot=n),
        grid=(nt,),
        in_specs=[pl.BlockSpec((tm, wz), lambda i: (i, 0)),
                  pl.BlockSpec((8, wz), lambda i: (jnp.maximum(i * (tm // 8) - 1, 0), 0)),
                  pl.BlockSpec((8, wz), lambda i: (jnp.minimum((i + 1) * (tm // 8), n // 8 - 1), 0)),
                  full(mu), full(w2cat), full(w0cat), full(a2cat), full(a0cat), full(g2pad),
                  full(k_k), full(k_a), full(ehead)],
        out_specs=[tok, tok, tok, tok, tok2, tok2, tok2],
        out_shape=[sd((n, c), F32)] * 4 + [sd((2, n, c), F32)] * 3,
        compiler_params=_cparams(("parallel",)),
        name="rwkv_prep",
    )(zb, zb, zb, mu, w2cat, w0cat, a2cat, a0cat, g2pad, k_k, k_a, ehead)


def _rwkv_chunk_kernel(r_ref, v_ref, kk_ref, kd_ref, lw_ref, bd_ref, mincl_ref, mbef_ref, msame_ref,
                       gh_ref, qy_ref):
    lw = lw_ref[...]
    mincl = mincl_ref[...]
    cum_i = _mm_exact_lhs(mincl, lw)
    cum_e = cum_i - lw
    cum_l = _mm_exact_lhs(msame_ref[...], lw)
    r = r_ref[...]
    v = v_ref[...]
    a = -kk_ref[...]
    k = kd_ref[...]
    b = bd_ref[...]
    at = a * jnp.exp(cum_e)
    rt = r * jnp.exp(cum_i)
    einv = jnp.exp(-cum_i)
    bt = b * einv
    kt = k * einv
    e_l = jnp.exp(cum_l - cum_i)
    bh = b * e_l
    kh = k * e_l
    p_l = jnp.exp(cum_l)
    incl = mincl > 0.5
    bef = mbef_ref[...] > 0.5
    hd = B_HEAD_DIM
    nch = ROW_TILE // RWKV_CHUNK
    zeros_h = jnp.zeros((ROW_TILE, hd), F32)
    eye = (lax.broadcasted_iota(jnp.int32, (RWKV_CHUNK, LANE), 0)
           == lax.broadcasted_iota(jnp.int32, (RWKV_CHUNK, LANE), 1))
    ri = lax.broadcasted_iota(jnp.int32, (ROW_TILE, ROW_TILE), 0)
    ci = lax.broadcasted_iota(jnp.int32, (ROW_TILE, ROW_TILE), 1)
    xr = ri ^ ci
    eye_t = jnp.where(ri == ci, 1.0, 0.0)
    for h in range(RWKV_STEP_HEADS):
        sl = slice(h * hd, (h + 1) * hd)
        a_ab = jnp.where(bef, _mm_nt(at[:, sl], bt[:, sl]), 0.0)
        a_ak = jnp.where(bef, _mm_nt(at[:, sl], kt[:, sl]), 0.0)
        m_rb = jnp.where(incl, _mm_nt(rt[:, sl], bt[:, sl]), 0.0)
        m_rk = jnp.where(incl, _mm_nt(rt[:, sl], kt[:, sl]), 0.0)
        vh = v[:, sl]
        t_inv = eye_t + jnp.where(xr == 1, a_ab, 0.0)
        for sh in range(1, 6):
            a_lev = jnp.where((xr >> sh) == 1, a_ab, 0.0)
            t_inv = t_inv + _mm(t_inv, _mm(a_lev, t_inv))
        x = _mm(t_inv, jnp.concatenate([at[:, sl], _mm(a_ak, vh)], axis=1))
        v0 = jnp.concatenate([zeros_h, vh], axis=1)
        qy_ref[h] = jnp.concatenate([rt[:, sl], zeros_h], axis=1) + _mm(m_rb, x) + _mm(m_rk, v0)
        for c in range(nch):
            rows = slice(c * RWKV_CHUNK, (c + 1) * RWKV_CHUNK)
            gh = _mm_tn(bh[rows, sl], x[rows]) + _mm_tn(kh[rows, sl], v0[rows])
            plr = jnp.concatenate([p_l[c * RWKV_CHUNK:c * RWKV_CHUNK + 1, sl], jnp.zeros((1, hd), F32)], axis=1)
            gh_ref[h, rows, :] = gh + jnp.where(eye, jnp.broadcast_to(plr, (RWKV_CHUNK, LANE)), 0.0)


def rwkv_chunk(r, v, kkn, kd, lw, bd, mincl, mbef, msame):
    n, c = r.shape
    tm = ROW_TILE
    wide = RWKV_STEP_HEADS * B_HEAD_DIM
    tokb = pl.BlockSpec((tm, wide), lambda d, s, p: (s, p))
    tokd = pl.BlockSpec((None, tm, wide), lambda d, s, p: (d, s, p))
    mdir = pl.BlockSpec((None, tm, tm), lambda d, s, p: (d, 0, 0))
    outb = pl.BlockSpec((None, RWKV_STEP_HEADS, tm, LANE), lambda d, s, p: (d, p, s, 0))
    sd = jax.ShapeDtypeStruct((2, B_HEADS, n, LANE), F32)
    return pl.pallas_call(
        _rwkv_chunk_kernel,
        grid=(2, n // tm, B_HEADS // RWKV_STEP_HEADS),
        in_specs=[tokb, tokb, tokb, tokd, tokd, tokd, mdir, mdir,
                  pl.BlockSpec((tm, tm), lambda d, s, p: (0, 0))],
        out_specs=[outb, outb],
        out_shape=[sd, sd],
        compiler_params=_cparams(("parallel", "parallel", "parallel")),
        name="rwkv_chunk",
    )(r, v, kkn, kd, lw, bd, mincl, mbef, msame)


def _scan_tile(d, j, nt, n_ctx_tiles):
    n_lat_tiles = nt - n_ctx_tiles
    fwd = jnp.where(j < n_ctx_tiles, n_lat_tiles + j, j - n_ctx_tiles)
    return jnp.where(d == 0, fwd, nt - 1 - j)


def _rwkv_seq_kernel(gh_ref, qy_ref, y_ref, z_s):
    d = pl.program_id(0)
    j = pl.program_id(1)

    @pl.when(j == 0)
    def _():
        z_s[...] = jnp.zeros(z_s.shape, F32)

    hd = B_HEAD_DIM
    nch = ROW_TILE // RWKV_CHUNK
    for ci in range(nch):
        c = jnp.where(d == 0, ci, nch - 1 - ci)
        off = pl.multiple_of(c * RWKV_CHUNK, RWKV_CHUNK)
        ys = []
        for h in range(B_HEADS):
            z = z_s[h]
            gh = gh_ref[h, pl.ds(off, RWKV_CHUNK), :]
            qy = qy_ref[h, pl.ds(off, RWKV_CHUNK), :]
            ys.append(_mm(qy[:, :hd], z) + qy[:, hd:])
            z_s[h] = _mm(gh[:, :hd], z) + gh[:, hd:]
        y_ref[pl.ds(off, RWKV_CHUNK), :] = jnp.concatenate(ys, axis=1)


def rwkv_seq(gh, qy, n_ctx):
    _, hh, n, _ = gh.shape
    tm = ROW_TILE
    nt = n // tm
    nct = n_ctx // tm
    inb = pl.BlockSpec((None, hh, tm, LANE), lambda d, j: (d, 0, _scan_tile(d, j, nt, nct), 0))
    return pl.pallas_call(
        _rwkv_seq_kernel,
        grid=(2, nt),
        in_specs=[inb, inb],
        out_specs=pl.BlockSpec((None, tm, B_WIDTH), lambda d, j: (d, _scan_tile(d, j, nt, nct), 0)),
        out_shape=jax.ShapeDtypeStruct((2, n, B_WIDTH), F32),
        scratch_shapes=[pltpu.VMEM((hh, B_HEAD_DIM, B_HEAD_DIM), F32)],
        compiler_params=_cparams(("parallel", "arbitrary")),
        name="rwkv_seq",
    )(gh, qy)


def _rwkv_finish_kernel(y_ref, r_ref, v_ref, kd_ref, g_ref, rk_ref, lw_ref, lb_ref, eh_ref, o_ref):
    y = y_ref[0] + y_ref[1]
    eh = eh_ref[...]
    inv = 1.0 / B_HEAD_DIM
    mu = _mm_exact_rhs(y, eh) * inv
    yc = y - mu
    var = _mm_exact_rhs(yc * yc, eh) * inv
    yn = yc * lax.rsqrt(var + RWKV_LN_EPS) * lw_ref[...] + lb_ref[...]
    r = r_ref[...]
    rk = (r * kd_ref[0] + r * kd_ref[1]) * rk_ref[...]
    bonus = _mm_exact_rhs(rk, eh) * v_ref[...]
    o_ref[...] = ((yn + bonus) * g_ref[...]).astype(o_ref.dtype)


def rwkv_finish(y, r, v, kd, g, r_k, lnx_w, lnx_b, ehead):
    n, c = r.shape
    tm = ROW_TILE
    tok = pl.BlockSpec((tm, c), lambda i: (i, 0))
    tok2 = pl.BlockSpec((2, tm, c), lambda i: (0, i, 0))
    full = lambda a: pl.BlockSpec(a.shape, lambda i: (0,) * a.ndim)
    return pl.pallas_call(
        _rwkv_finish_kernel,
        grid=(n // tm,),
        in_specs=[tok2, tok, tok, tok2, tok, full(r_k), full(lnx_w), full(lnx_b), full(ehead)],
        out_specs=tok,
        out_shape=jax.ShapeDtypeStruct((n, c), BF16),
        compiler_params=_cparams(("parallel",)),
        name="rwkv_finish",
    )(y, r, v, kd, g, r_k, lnx_w, lnx_b, ehead)


def _layer_norm(x, g, b):
    mu = jnp.mean(x, axis=-1, keepdims=True)
    xc = x - mu
    var = jnp.mean(xc * xc, axis=-1, keepdims=True)
    return xc * lax.rsqrt(var + 1e-5) * g + b


def _store_token_rows(ref, val):
    tm = val.shape[0]
    for s in range(TOK_ROWS):
        ref[pl.ds(s, tm, stride=TOK_ROWS), :] = val[:, s * LANE:(s + 1) * LANE]


def _load_token_rows(ref, tm):
    return jnp.concatenate([ref[pl.ds(s, tm, stride=TOK_ROWS), :] for s in range(TOK_ROWS)], axis=1)


def _outproj_kernel(a_ref, w_ref, x_ref, mod_ref, g_ref, b_ref, x1_ref, f_ref, f3_ref, *, n_ctx_tiles, d):
    i = pl.program_id(0)
    m = jnp.dot(a_ref[...], w_ref[...], preferred_element_type=F32)
    gate = _mod_rows(mod_ref, i, n_ctx_tiles, 2, d)
    x1 = _layer_norm(DEEPNORM_ALPHA * x_ref[...] + gate * m, g_ref[...], b_ref[...])
    x1_ref[...] = x1
    sh = _mod_rows(mod_ref, i, n_ctx_tiles, 3, d)
    sc = _mod_rows(mod_ref, i, n_ctx_tiles, 4, d)
    f = x1 * (1.0 + sc) + sh
    f_ref[...] = f
    _store_token_rows(f3_ref, f)


def outproj_ln(a, w_out, x, mod, ln_g, ln_b, n_ctx):
    n, kdim = a.shape
    d = x.shape[1]
    tm = ROW_TILE
    tok = pl.BlockSpec((tm, d), lambda i: (i, 0))
    vec = pl.BlockSpec((1, d), lambda i: (0, 0))
    return pl.pallas_call(
        functools.partial(_outproj_kernel, n_ctx_tiles=(n - n_ctx) // tm, d=d),
        grid=(n // tm,),
        in_specs=[pl.BlockSpec((tm, kdim), lambda i: (i, 0)),
                  pl.BlockSpec((kdim, d), lambda i: (0, 0)),
                  tok, pl.BlockSpec(mod.shape, lambda i: (0, 0)), vec, vec],
        out_specs=[tok, tok, pl.BlockSpec((tm * TOK_ROWS, LANE), lambda i: (i, 0))],
        out_shape=[jax.ShapeDtypeStruct((n, d), F32)] * 2 + [jax.ShapeDtypeStruct((n * TOK_ROWS, LANE), F32)],
        compiler_params=_cparams(("parallel",)),
        name="outproj_ln",
    )(a, w_out, x, mod, ln_g.reshape(1, d), ln_b.reshape(1, d))


PICK_IDX, PICK_RANK, PICK_GATE = 0, 8, 16


def _router_kernel(f_ref, wr_hi_ref, wr_lo_ref, br_ref, wsg_ref, wsd_ref, ltri_ref, pick_ref, sh_ref, cnt_ref,
                   carry_s):
    i = pl.program_id(0)

    @pl.when(i == 0)
    def _():
        carry_s[...] = jnp.zeros(carry_s.shape, F32)

    f = f_ref[...]
    f_hi = f.astype(BF16)
    f_lo = (f - f_hi.astype(F32)).astype(BF16)
    dd = lambda a, b: jnp.dot(a, b[...], preferred_element_type=F32)
    logits = dd(f_hi, wr_hi_ref) + dd(f_lo, wr_hi_ref) + dd(f_hi, wr_lo_ref)
    scores = jax.nn.sigmoid(logits)
    lane = lax.broadcasted_iota(jnp.int32, scores.shape, 1).astype(F32)
    neg = jnp.float32(-jnp.inf)
    sel = jnp.where(lane < N_EXPERTS, scores + br_ref[...], neg)
    top = jnp.zeros(scores.shape, F32)
    chosen = jnp.zeros(scores.shape, F32)
    hits, firsts = [], []
    for j in range(TOP_K):
        mx = jnp.max(sel, axis=1, keepdims=True)
        first = jnp.min(jnp.where(sel == mx, lane, float(LANE)), axis=1, keepdims=True)
        hit = lane == first
        top = jnp.where(hit, scores, top)
        chosen = jnp.where(hit, 1.0, chosen)
        sel = jnp.where(hit, neg, sel)
        hits.append(hit)
        firsts.append(first)
    gates = top / jnp.sum(top, axis=1, keepdims=True) * ROUTE_SCALE
    carry = carry_s[0:1, :]
    rank = dd(ltri_ref[...], chosen.astype(BF16)) + carry
    pick = jnp.zeros(scores.shape, F32)
    for j in range(TOP_K):
        pick = jnp.where(lane == PICK_IDX + j, firsts[j], pick)
        rank_j = jnp.sum(jnp.where(hits[j], rank, 0.0), axis=1, keepdims=True)
        pick = jnp.where(lane == PICK_RANK + j, rank_j, pick)
        gate_j = jnp.sum(jnp.where(hits[j], gates, 0.0), axis=1, keepdims=True)
        pick = jnp.where(lane == PICK_GATE + j, gate_j, pick)
    pick_ref[...] = pick
    carry_s[0:1, :] = carry + jnp.sum(chosen, axis=0, keepdims=True)
    cnt_ref[...] = carry_s[...]
    ab = dd(f_hi, wsg_ref)
    hdn = _silu(ab[:, :SHARED_FF]) * ab[:, SHARED_FF:]
    sh_ref[...] = dd(hdn.astype(BF16), wsd_ref)


def router_shared(f_in, wr_hi, wr_lo, br, ws_gu, ws_down):
    n, d = f_in.shape
    tm = ROW_TILE
    full = lambda a: pl.BlockSpec(a.shape, lambda i: (0,) * a.ndim)
    sd = jax.ShapeDtypeStruct
    ltri = jnp.asarray(np.tril(np.ones((tm, tm), np.float32), -1), BF16)
    return pl.pallas_call(
        _router_kernel,
        grid=(n // tm,),
        in_specs=[pl.BlockSpec((tm, d), lambda i: (i, 0)), full(wr_hi), full(wr_lo), full(br),
                  full(ws_gu), full(ws_down), full(ltri)],
        out_specs=[pl.BlockSpec((tm, LANE), lambda i: (i, 0)), pl.BlockSpec((tm, d), lambda i: (i, 0)),
                   pl.BlockSpec((8, LANE), lambda i: (0, 0))],
        out_shape=[sd((n, LANE), F32), sd((n, d), F32), sd((8, LANE), F32)],
        scratch_shapes=[pltpu.VMEM((8, LANE), F32)],
        compiler_params=_cparams(("arbitrary",)),
        name="router_shared",
    )(f_in, wr_hi, wr_lo, br, ws_gu, ws_down, ltri)


def _slab(ref, row):
    return ref.at[pl.ds(pl.multiple_of(row * TOK_ROWS, TOK_ROWS), TOK_ROWS)]


def _dispatch_kernel(slots_ref, padlo_ref, padhi_ref, f3_ref, xs_hbm, zrow, sem, *, tm, n_tiles):
    i = pl.program_id(0)

    def body(t, carry):
        for j in range(TOP_K):
            slot = slots_ref[(i * tm + t) * TOP_K + j]
            pltpu.make_async_copy(_slab(f3_ref, t), _slab(xs_hbm, slot), sem.at[0]).start()
        return carry
    lax.fori_loop(0, tm, body, 0)
    for _ in range(TOP_K):
        pltpu.make_async_copy(f3_ref, xs_hbm.at[pl.ds(0, tm * TOK_ROWS)], sem.at[0]).wait()

    @pl.when(i == n_tiles - 1)
    def _():
        zrow[...] = jnp.zeros(zrow.shape, F32)

        def per_expert(e, carry):
            lo = padlo_ref[e]
            hi = padhi_ref[e]

            def fill(s, c):
                pltpu.make_async_copy(zrow, _slab(xs_hbm, s), sem.at[1]).start()
                return c
            lax.fori_loop(lo, hi, fill, 0)

            def drain(s, c):
                pltpu.make_async_copy(zrow, _slab(xs_hbm, 0), sem.at[1]).wait()
                return c
            lax.fori_loop(lo, hi, drain, 0)
            return carry
        lax.fori_loop(0, N_EXPERTS + 1, per_expert, 0)


def moe_dispatch_rows(f3, tok_slots, pad_lo, pad_hi, n_slots):
    tm = ROW_TILE
    n = f3.shape[0] // TOK_ROWS
    grid_spec = pltpu.PrefetchScalarGridSpec(
        num_scalar_prefetch=3,
        grid=(n // tm,),
        in_specs=[pl.BlockSpec((tm * TOK_ROWS, LANE), lambda i, sl, lo, hi: (i, 0))],
        out_specs=pl.BlockSpec(memory_space=pl.ANY),
        scratch_shapes=[pltpu.VMEM((TOK_ROWS, LANE), F32), pltpu.SemaphoreType.DMA((2,))],
    )
    return pl.pallas_call(
        functools.partial(_dispatch_kernel, tm=tm, n_tiles=n // tm),
        grid_spec=grid_spec,
        out_shape=jax.ShapeDtypeStruct((n_slots * TOK_ROWS, LANE), F32),
        compiler_params=_cparams(("arbitrary",)),
        name="moe_dispatch",
    )(tok_slots, pad_lo, pad_hi, f3)


def _expert_kernel(be_ref, nu_ref, xs_ref, wgu_ref, wdn_ref, ys_ref, wgu_bf, wdn_bf):
    i = pl.program_id(0)

    @pl.when(jnp.logical_or(i == 0, be_ref[i] != be_ref[jnp.maximum(i - 1, 0)]))
    def _():
        wgu_bf[...] = wgu_ref[...].astype(BF16)
        wdn_bf[...] = wdn_ref[...].astype(BF16)

    @pl.when(i < nu_ref[0])
    def _():
        x = _load_token_rows(xs_ref, EXPERT_BLOCK).astype(BF16)
        ab = jnp.dot(x, wgu_bf[...], preferred_element_type=F32)
        hdn = _silu(ab[:, :EXPERT_FF]) * ab[:, EXPERT_FF:]
        y = jnp.dot(hdn.astype(BF16), wdn_bf[...], preferred_element_type=F32)
        _store_token_rows(ys_ref, y)

    @pl.when(i >= nu_ref[0])
    def _():
        ys_ref[...] = jnp.zeros(ys_ref.shape, F32)


def routed_experts(xs, block_expert, n_used, w_gu_all, w_down_all, layer):
    d = w_gu_all.shape[2]
    n_blocks = block_expert.shape[0]
    rows = EXPERT_BLOCK * TOK_ROWS
    last_used = lambda i, nu: jnp.minimum(i, jnp.maximum(nu[0] - 1, 0))
    grid_spec = pltpu.PrefetchScalarGridSpec(
        num_scalar_prefetch=2,
        grid=(n_blocks,),
        in_specs=[pl.BlockSpec((rows, LANE), lambda i, be, nu: (last_used(i, nu), 0)),
                  pl.BlockSpec((None, None, d, 2 * EXPERT_FF), lambda i, be, nu: (layer, be[i], 0, 0)),
                  pl.BlockSpec((None, None, EXPERT_FF, d), lambda i, be, nu: (layer, be[i], 0, 0))],
        out_specs=pl.BlockSpec((rows, LANE), lambda i, be, nu: (i, 0)),
        scratch_shapes=[pltpu.VMEM((d, 2 * EXPERT_FF), BF16), pltpu.VMEM((EXPERT_FF, d), BF16)],
    )
    return pl.pallas_call(
        _expert_kernel,
        grid_spec=grid_spec,
        out_shape=jax.ShapeDtypeStruct((n_blocks * rows, LANE), F32),
        compiler_params=_cparams(("arbitrary",)),
        name="routed_experts",
    )(block_expert, n_used, xs, w_gu_all, w_down_all)


def moe_slots(pick, counts, n_blocks):
    eb = EXPERT_BLOCK
    idx6 = pick[:, PICK_IDX:PICK_IDX + TOP_K].astype(jnp.int32)
    rank6 = pick[:, PICK_RANK:PICK_RANK + TOP_K].astype(jnp.int32)
    padded = (counts + eb - 1) // eb * eb
    pad_end = jnp.cumsum(padded)
    pad_start = pad_end - padded
    experts = jnp.arange(N_EXPERTS, dtype=jnp.int32)
    start6 = jnp.sum(jnp.where(idx6[:, :, None] == experts, pad_start, 0), axis=-1)
    tok_slots = (start6 + rank6).reshape(-1).astype(jnp.int32)
    blk_start = jnp.arange(n_blocks, dtype=jnp.int32) * eb
    block_expert = jnp.minimum(jnp.sum(pad_end[None, :] <= blk_start[:, None], axis=1), N_EXPERTS - 1)
    block_expert = block_expert.astype(jnp.int32)
    n_used = (pad_end[-1:] // eb).astype(jnp.int32)
    fill_lo = jnp.concatenate([pad_start + counts, pad_end[-1:]]).astype(jnp.int32)
    fill_hi = jnp.concatenate([pad_end, jnp.full((1,), n_blocks * eb, pad_end.dtype)]).astype(jnp.int32)
    return tok_slots, block_expert, n_used, fill_lo, fill_hi


def _combine_kernel(slots_ref, ys_hbm, pick_ref, sh_ref, x1_ref, mod_ref, g_ref, b_ref, *rest,
                    tm, n_tiles, n_ctx_tiles, d, with_next):
    if with_next:
        modn_ref, x2_ref, h_ref, buf, sem = rest
    else:
        x2_ref, buf, sem = rest
    i = pl.program_id(0)
    s = i % 2
    tok_rows = TOP_K * TOK_ROWS

    def start_gather(tile, slot):
        def body(t, carry):
            for j in range(TOP_K):
                src = slots_ref[(tile * tm + t) * TOP_K + j]
                pltpu.make_async_copy(_slab(ys_hbm, src), _slab(buf.at[slot], t * TOP_K + j), sem.at[slot]).start()
            return carry
        lax.fori_loop(0, tm, body, 0)

    @pl.when(i == 0)
    def _():
        start_gather(0, 0)

    @pl.when(i + 1 < n_tiles)
    def _():
        start_gather(i + 1, 1 - s)

    pltpu.make_async_copy(ys_hbm.at[pl.ds(0, tm * tok_rows)], buf.at[s], sem.at[s]).wait()
    pieces = []
    for c in range(TOK_ROWS):
        acc = sh_ref[:, c * LANE:(c + 1) * LANE]
        for j in range(TOP_K):
            gate_j = pick_ref[:, PICK_GATE + j:PICK_GATE + j + 1]
            acc = acc + gate_j * buf[s, pl.ds(j * TOK_ROWS + c, tm, stride=tok_rows), :]
        pieces.append(acc)
    f = jnp.concatenate(pieces, axis=1)
    gate = _mod_rows(mod_ref, i, n_ctx_tiles, 5, d)
    x2 = _layer_norm(DEEPNORM_ALPHA * x1_ref[...] + gate * f, g_ref[...], b_ref[...])
    x2_ref[...] = x2
    if with_next:
        sh = _mod_rows(modn_ref, i, n_ctx_tiles, 0, d)
        sc = _mod_rows(modn_ref, i, n_ctx_tiles, 1, d)
        h_ref[...] = (x2 * (1.0 + sc) + sh).astype(h_ref.dtype)


def moe_combine(ys, tok_slots, pick, shared, x1, mod, ln_g, ln_b, n_ctx, mod_next=None):
    n, d = x1.shape
    tm = COMBINE_TILE
    tok = pl.BlockSpec((tm, d), lambda i, sl: (i, 0))
    vec = pl.BlockSpec((1, d), lambda i, sl: (0, 0))
    with_next = mod_next is not None
    in_specs = [pl.BlockSpec(memory_space=pl.ANY), pl.BlockSpec((tm, LANE), lambda i, sl: (i, 0)), tok, tok,
                pl.BlockSpec(mod.shape, lambda i, sl: (0, 0)), vec, vec]
    args = [ys, pick, shared, x1, mod, ln_g.reshape(1, d), ln_b.reshape(1, d)]
    out_specs = [tok]
    out_shape = [jax.ShapeDtypeStruct((n, d), F32)]
    if with_next:
        in_specs.append(pl.BlockSpec(mod_next.shape, lambda i, sl: (0, 0)))
        args.append(mod_next)
        out_specs.append(tok)
        out_shape.append(jax.ShapeDtypeStruct((n, d), BF16))
    grid_spec = pltpu.PrefetchScalarGridSpec(
        num_scalar_prefetch=1,
        grid=(n // tm,),
        in_specs=in_specs, out_specs=out_specs,
        scratch_shapes=[pltpu.VMEM((2, tm * TOP_K * TOK_ROWS, LANE), F32), pltpu.SemaphoreType.DMA((2,))],
    )
    return pl.pallas_call(
        functools.partial(_combine_kernel, tm=tm, n_tiles=n // tm, n_ctx_tiles=(n - n_ctx) // tm, d=d,
                          with_next=with_next),
        grid_spec=grid_spec,
        out_shape=out_shape,
        compiler_params=_cparams(("arbitrary",)),
        name="moe_combine",
    )(tok_slots, *args)


def moe_layer(f_in, f3, x1, mod, ln_g, ln_b, w_router, b_router, w_gu_all, w_down_all, layer, ws_gu, ws_down,
              n_ctx, mod_next=None):
    n, d = f_in.shape
    wr = jnp.pad(w_router, ((0, 0), (0, LANE - N_EXPERTS)))
    wr_hi = wr.astype(BF16)
    wr_lo = (wr - wr_hi.astype(F32)).astype(BF16)
    br = jnp.pad(b_router.astype(F32), (0, LANE - N_EXPERTS)).reshape(1, LANE)
    pick, shared, cnt = router_shared(f_in, wr_hi, wr_lo, br, ws_gu.astype(BF16), ws_down.astype(BF16))
    counts = cnt[0, :N_EXPERTS].astype(jnp.int32)
    n_blocks = -(-(n * TOP_K + N_EXPERTS * (EXPERT_BLOCK - 1)) // EXPERT_BLOCK)
    tok_slots, block_expert, n_used, pad_lo, pad_hi = moe_slots(pick, counts, n_blocks)
    xs = moe_dispatch_rows(f3, tok_slots, pad_lo, pad_hi, n_blocks * EXPERT_BLOCK)
    ys = routed_experts(xs, block_expert, n_used, w_gu_all, w_down_all, layer)
    return moe_combine(ys, tok_slots, pick, shared, x1, mod, ln_g, ln_b, n_ctx, mod_next)


HGRN_SMALL_LEVELS = (4, 2)
HGRN_STEP_HEADS = 2


def _hgrn_levels():
    t = ROW_TILE
    levels = []
    b = t // 2
    while b >= 1:
        levels.append(b)
        b //= 2
    return levels


def _hgrn_kernel(q_ref, f_ref, i_ref, lb_ref, mlev_ref, mincl_ref, o_ref, st_s, cum_s):
    d = pl.program_id(0)
    j = pl.program_id(2)

    @pl.when(j == 0)
    def _():
        st_s[...] = jnp.zeros(st_s.shape, F32)

    t = ROW_TILE
    lb = lb_ref[...]
    log_lb = jnp.log(lb)
    log_rest = jnp.log1p(-lb)
    fz = f_ref[...]
    q = _silu(q_ref[...])
    v = i_ref[...]
    log_sig = jnp.minimum(fz, 0.0) - jnp.log1p(jnp.exp(-jnp.abs(fz)))
    u = log_rest + log_sig
    mx = jnp.maximum(log_lb, u)
    logf = mx + jnp.log1p(jnp.exp(-jnp.abs(log_lb - u)))
    key = (1.0 - lb) * jax.nn.sigmoid(-fz)

    cum = _mm_exact_lhs(mincl_ref[...], logf)
    cum_l = jnp.where(d == 0, cum[t - 1:t, :], cum[0:1, :])
    cum_s[...] = cum
    d_small = _mm_exact_lhs(mlev_ref[...], logf)

    wide = HGRN_STEP_HEADS * LANE
    heads = [slice(h * LANE, (h + 1) * LANE) for h in range(HGRN_STEP_HEADS)]
    row = lax.broadcasted_iota(jnp.int32, (t, 1), 0)
    xr = (lax.broadcasted_iota(jnp.int32, (t, t), 0) ^ lax.broadcasted_iota(jnp.int32, (t, t), 1))
    a = [jnp.zeros((t, t), F32) for _ in heads]
    for li, b in enumerate(_hgrn_levels()):
        q_side = ((row // b) % 2) == jnp.where(d == 0, 1, 0)
        if b >= 8:
            piv = jnp.concatenate(
                [jnp.broadcast_to(cum_s[pl.ds(g * 2 * b + b - 1 + d, 1), :], (2 * b, wide))
                 for g in range(t // (2 * b))], axis=0)
            dlev = jnp.where(q_side, cum - piv, piv - cum)
        elif b > 1:
            k = HGRN_SMALL_LEVELS.index(b)
            dlev = d_small[k * t:(k + 1) * t]
        else:
            dlev = jnp.where(q_side, logf, 0.0)
        e = jnp.exp(dlev)
        qt = jnp.where(q_side, q * e, 0.0)
        kt = jnp.where(q_side, 0.0, key * e)
        same_group = (xr >> (b.bit_length())) == 0
        for h, sl in enumerate(heads):
            a[h] = a[h] + jnp.where(same_group, _mm_nt(qt[:, sl], kt[:, sl]), 0.0)
    q_in = q * jnp.exp(cum)
    k_out = key * jnp.exp(cum_l - cum)
    decay_l = jnp.exp(cum_l)
    qk = q * key
    for h, sl in enumerate(heads):
        st = st_s[h]
        o_ref[:, sl] = (_mm(a[h], v[:, sl]) + _mm_nt(q_in[:, sl], st)
                        + jnp.sum(qk[:, sl], axis=1, keepdims=True) * v[:, sl])
        st_s[h] = st * decay_l[:, sl] + _mm_tn(v[:, sl], k_out[:, sl])


def hgrn_matrices():
    t = ROW_TILE
    ti = np.arange(t)[:, None]
    ii = np.arange(t)[None, :]
    mlev = np.zeros((2, len(HGRN_SMALL_LEVELS), t, t), np.float32)
    for li, b in enumerate(HGRN_SMALL_LEVELS):
        bnd = (ti // (2 * b)) * (2 * b) + b - 1
        upper = (ti // b) % 2 == 1
        mlev[0, li] = np.where(upper, (ii > bnd) & (ii <= ti), (ii > ti) & (ii <= bnd))
        mlev[1, li] = np.where(upper, (ii > bnd) & (ii < ti), (ii >= ti) & (ii <= bnd))
    mincl = np.stack([(ii <= ti), (ii >= ti)]).astype(np.float32)
    return (jnp.asarray(mlev.reshape(2, -1, t), BF16), jnp.asarray(mincl, BF16))


def hgrn_scan(z, lb, mlev, mincl, n_ctx):
    n = z.shape[0]
    t = ROW_TILE
    nt = n // t
    nct = n_ctx // t
    hh = C_HEADS
    sh = HGRN_STEP_HEADS
    wide = sh * LANE
    hb = hh // sh
    srow = lambda d, h, j: _scan_tile(d, j, nt, nct)
    return pl.pallas_call(
        _hgrn_kernel,
        grid=(2, hb, nt),
        in_specs=[pl.BlockSpec((t, wide), lambda d, h, j: (srow(d, h, j), h)),
                  pl.BlockSpec((t, wide), lambda d, h, j: (srow(d, h, j), hb * (1 + d) + h)),
                  pl.BlockSpec((t, wide), lambda d, h, j: (srow(d, h, j), 3 * hb + h)),
                  pl.BlockSpec((None, 1, wide), lambda d, h, j: (h, 0, 0)),
                  pl.BlockSpec((None,) + mlev.shape[1:], lambda d, h, j: (d, 0, 0)),
                  pl.BlockSpec((None, t, t), lambda d, h, j: (d, 0, 0))],
        out_specs=pl.BlockSpec((None, t, wide), lambda d, h, j: (d, srow(d, h, j), h)),
        out_shape=jax.ShapeDtypeStruct((2, n, hh * LANE), F32),
        scratch_shapes=[pltpu.VMEM((sh, LANE, LANE), F32), pltpu.VMEM((t, wide), F32)],
        compiler_params=_cparams(("parallel", "parallel", "arbitrary")),
        name="hgrn_scan",
    )(z, z, z, lb.reshape(hb, 1, wide), mlev, mincl)


def _hgrn_finish_kernel(o_ref, g_ref, w_ref, out_ref):
    w = w_ref[...]
    for h in range(C_HEADS):
        sl = slice(h * LANE, (h + 1) * LANE)
        o = o_ref[0, :, sl] + o_ref[1, :, sl]
        ms = jnp.mean(o * o, axis=-1, keepdims=True)
        out_ref[:, sl] = (o * lax.rsqrt(ms + 1e-5) * w * _silu(g_ref[:, sl])).astype(out_ref.dtype)


def hgrn_finish(o, z, norm_w):
    n = z.shape[0]
    t = ROW_TILE
    dm = C_HEADS * LANE
    return pl.pallas_call(
        _hgrn_finish_kernel,
        grid=(n // t,),
        in_specs=[pl.BlockSpec((2, t, dm), lambda i: (0, i, 0)),
                  pl.BlockSpec((t, dm), lambda i: (i, 4)),
                  pl.BlockSpec((1, LANE), lambda i: (0, 0))],
        out_specs=pl.BlockSpec((t, dm), lambda i: (i, 0)),
        out_shape=jax.ShapeDtypeStruct((n, dm), BF16),
        compiler_params=_cparams(("parallel",)),
        name="hgrn_finish",
    )(o, z, norm_w.reshape(1, LANE))


def rwkv_matrices():
    t = ROW_TILE
    ti = np.arange(t)[:, None]
    ii = np.arange(t)[None, :]
    same = (ti // RWKV_CHUNK) == (ii // RWKV_CHUNK)
    mincl = np.stack([same & (ii <= ti), same & (ii >= ti)]).astype(np.float32)
    mbef = np.stack([same & (ii < ti), same & (ii > ti)]).astype(np.float32)
    return (jnp.asarray(mincl, BF16), jnp.asarray(mbef, BF16), jnp.asarray(same.astype(np.float32), BF16))


def even_mixer(h, n_ctx, w_in, lam_params, subln_w, mu, w0, w2, a0, a2, g2, k_k, k_a, r_k, lnx_w, lnx_b,
               layer_idx, rope):
    n, d = h.shape
    n_lat = n - n_ctx
    hq = A_HEADS * 2 * A_QK_DIM
    a_in = 3 * hq
    cos_t, sin_t = rope
    q_scale = (A_QK_DIM ** -0.5) * math.log2(math.e)
    w_qk = jnp.concatenate([w_in[:, :hq] * q_scale, w_in[:, hq:2 * hq]], axis=1).astype(BF16)
    qk = matmul_rope(h, w_qk, cos_t, sin_t)
    v = matmul(h, w_in[:, 2 * hq:a_in].astype(BF16), BF16)
    b_in = w_in.shape[1] - a_in
    wz = 3 * B_WIDTH + 512
    w_b = jnp.pad(w_in[:, a_in:], ((0, 0), (0, wz - b_in))).astype(BF16)
    zb = matmul(h, w_b, F32)

    lam_f = lam_params.astype(F32)
    lam_init = 0.8 - 0.6 * math.exp(-0.3 * layer_idx)
    lam = jnp.exp(jnp.sum(lam_f[0] * lam_f[1])) - jnp.exp(jnp.sum(lam_f[2] * lam_f[3])) + lam_init
    tq = _row_tile(n_lat, (2048, 1024, 512, 256))
    tk = _row_tile(n, (1280, 640, 256))
    o_lat = diff_attention(qk, v, lam, subln_w, lam_init, tq, tk, n_lat // tq, 0, n // tk, 0)
    o_ctx = diff_attention(qk, v, lam, subln_w, lam_init, n_ctx, n_ctx, 1, n_lat // n_ctx, 1, n_lat // n_ctx)
    a_out = jnp.concatenate([o_lat, o_ctx], axis=0)

    c = B_WIDTH
    mu_p = jnp.pad(mu, (0, wz - b_in)).reshape(1, wz)
    zc = jnp.zeros((B_DECAY_RANK, c), F32)
    w2cat = jnp.concatenate([jnp.concatenate([w2[0], zc], 1), jnp.concatenate([zc, w2[1]], 1)], 0)
    a2cat = jnp.concatenate([jnp.concatenate([a2[0], zc], 1), jnp.concatenate([zc, a2[1]], 1)], 0)
    w0cat = w0.reshape(1, 2 * c)
    a0cat = a0.reshape(1, 2 * c)
    g2pad = jnp.pad(g2, ((0, 256 - B_GATE_RANK), (0, 0))).astype(BF16)
    hid = np.arange(c) // B_HEAD_DIM
    ehead = jnp.asarray((hid[:, None] == hid[None, :]).astype(np.float32), BF16)
    r, vv, kkn, g, kd, lw, bd = rwkv_prep(zb, mu_p, w2cat, w0cat, a2cat, a0cat, g2pad, k_k.reshape(1, c),
                                          k_a.reshape(1, c), ehead, n_ctx)
    mincl, mbef, msame = rwkv_matrices()
    gh, qy = rwkv_chunk(r, vv, kkn, kd, lw, bd, mincl, mbef, msame)
    y = rwkv_seq(gh, qy, n_ctx)
    b_out = rwkv_finish(y, r, vv, kd, g, r_k.reshape(1, c), lnx_w.reshape(1, c), lnx_b.reshape(1, c), ehead)
    return jnp.concatenate([a_out, b_out], axis=1)


def odd_mixer(h, n_ctx, w_in, lb, norm_w):
    z = matmul(h, w_in.astype(BF16), F32)
    mlev, mincl = hgrn_matrices()
    o = hgrn_scan(z, lb, mlev, mincl, n_ctx)
    return hgrn_finish(o, z, norm_w)


def kernel(x, c, ctx, c_ctx, ada_w, ada_b, ln_g, ln_b, even_w_in, even_w_out, diff_lambda, diff_subln_w, rwkv_mu, rwkv_w0, rwkv_w2, rwkv_a0, rwkv_a2, rwkv_g2, rwkv_k_k, rwkv_k_a, rwkv_r_k, rwkv_lnx_w, rwkv_lnx_b, odd_w_in, odd_w_out, hgrn_lb_raw, hgrn_norm_w, moe_router_w, moe_router_b, moe_w_gu, moe_w_down, moe_shared_gu, moe_shared_down):
    bsz, n_lat, d = x.shape
    n_ctx = ctx.shape[1]
    assert bsz == 1 and n_ctx % ROW_TILE == 0 and n_lat % ROW_TILE == 0
    depth = ada_w.shape[0]
    xs = jnp.concatenate([x[0], ctx[0]], axis=0)
    c8 = jnp.zeros((8, d), F32).at[0].set(c[0]).at[1].set(c_ctx)
    mods = adaln(c8, ada_w, ada_b)
    lb_soft = jax.nn.softmax(hgrn_lb_raw.astype(F32), axis=0)
    lower_bounds = jnp.cumsum(lb_soft, axis=0) - lb_soft[0]
    rope = rope_tables(n_ctx, n_lat)

    h = modulate(xs, mods[0], n_ctx)
    for l in range(depth):
        mod = mods[l]
        if l % 2 == 0:
            e = l // 2
            mix = even_mixer(h, n_ctx, even_w_in[e], diff_lambda[e], diff_subln_w[e], rwkv_mu[e], rwkv_w0[e],
                             rwkv_w2[e], rwkv_a0[e], rwkv_a2[e], rwkv_g2[e], rwkv_k_k[e], rwkv_k_a[e],
                             rwkv_r_k[e], rwkv_lnx_w[e], rwkv_lnx_b[e], l, rope)
            w_out = even_w_out[e]
        else:
            o = l // 2
            mix = odd_mixer(h, n_ctx, odd_w_in[o], lower_bounds[l].reshape(C_HEADS, C_KEY_DIM), hgrn_norm_w[o])
            w_out = odd_w_out[o]
        x1, f_in, f3 = outproj_ln(mix, w_out.astype(BF16), xs, mod, ln_g[l, 0], ln_b[l, 0], n_ctx)
        mod_next = mods[l + 1] if l + 1 < depth else None
        res = moe_layer(f_in, f3, x1, mod, ln_g[l, 1], ln_b[l, 1], moe_router_w[l], moe_router_b[l], moe_w_gu,
                        moe_w_down, l, moe_shared_gu[l], moe_shared_down[l], n_ctx, mod_next)
        if mod_next is not None:
            xs, h = res
        else:
            (xs,) = res
    return xs[:n_lat].reshape(bsz, n_lat, d)
```

```python
import functools
import math

import numpy as np
import jax
import jax.numpy as jnp
from jax import lax
from jax.experimental import pallas as pl
from jax.experimental.pallas import tpu as pltpu

F32 = jnp.float32
BF16 = jnp.bfloat16
HIGHEST = lax.Precision.HIGHEST

GRID_W = 64
A_HEADS = 8
A_QK_DIM = 64
A_V_DIM = 128
ROPE_BASE = 10000.0
B_HEADS = 16
B_HEAD_DIM = 64
B_WIDTH = 1024
B_DECAY_RANK = 64
B_ICL_RANK = 64
B_GATE_RANK = 160
RWKV_LN_EPS = 64e-5
C_HEADS = 16
C_KEY_DIM = 128
N_EXPERTS = 64
TOP_K = 6
EXPERT_FF = 512
SHARED_FF = 1024
ROUTE_SCALE = 2.5
DEPTH = 2
DEEPNORM_ALPHA = (2 * DEPTH) ** 0.25

ROW_TILE = 256
RWKV_STEP_HEADS = 8
RWKV_CHUNK = 64
EXPERT_BLOCK = 256
LANE = 128
TOK_ROWS = 16
COMBINE_TILE = 128
VMEM_LIMIT = 56 * 1024 * 1024


def _cparams(sem):
    return pltpu.CompilerParams(dimension_semantics=sem, vmem_limit_bytes=VMEM_LIMIT)


def _mm(a, b):
    return jnp.dot(a.astype(BF16), b.astype(BF16), preferred_element_type=F32)


def _mm_nt(a, b):
    return lax.dot_general(a.astype(BF16), b.astype(BF16), (((1,), (1,)), ((), ())),
                           preferred_element_type=F32)


def _mm_tn(a, b):
    return lax.dot_general(a.astype(BF16), b.astype(BF16), (((0,), (0,)), ((), ())),
                           preferred_element_type=F32)


def _split3(x):
    hi = x.astype(BF16)
    r1 = x - hi.astype(F32)
    mid = r1.astype(BF16)
    lo = (r1 - mid.astype(F32)).astype(BF16)
    return hi, mid, lo


def _mm_exact_lhs(m_bf16, x):
    hi, mid, lo = _split3(x)
    d = lambda t: jnp.dot(m_bf16, t, preferred_element_type=F32)
    return d(hi) + d(mid) + d(lo)


def _mm_exact_rhs(x, m_bf16):
    hi, mid, lo = _split3(x)
    d = lambda t: jnp.dot(t, m_bf16, preferred_element_type=F32)
    return d(hi) + d(mid) + d(lo)


def _silu(x):
    return x * jax.nn.sigmoid(x)


def _softplus(x):
    return jnp.maximum(x, 0.0) + jnp.log1p(jnp.exp(-jnp.abs(x)))


def _row_tile(m, cands=(1280, 640, 512, 256)):
    for t in cands:
        if m % t == 0:
            return t
    raise ValueError(f"no row tile for {m}")


def _adaln_kernel(c_ref, w_ref, b_ref, o_ref):
    s = _silu(c_ref[...])
    o_ref[...] = _mm(s, w_ref[...]) + b_ref[...]


def adaln(c8, ada_w, ada_b):
    depth, d, n6 = ada_w.shape
    tn = 512
    return pl.pallas_call(
        _adaln_kernel,
        grid=(depth, n6 // tn),
        in_specs=[pl.BlockSpec((8, d), lambda l, j: (0, 0)),
                  pl.BlockSpec((None, d, tn), lambda l, j: (l, 0, j)),
                  pl.BlockSpec((None, 1, tn), lambda l, j: (l, 0, j))],
        out_specs=pl.BlockSpec((None, 8, tn), lambda l, j: (l, 0, j)),
        out_shape=jax.ShapeDtypeStruct((depth, 8, n6), F32),
        compiler_params=_cparams(("parallel", "parallel")),
        name="adaln",
    )(c8, ada_w, ada_b.reshape(depth, 1, n6))


def _mod_rows(mod_ref, i, n_ctx_tiles, idx, d):
    row = jnp.where(i >= n_ctx_tiles, 1, 0)
    return mod_ref[pl.ds(row, 1), idx * d:(idx + 1) * d]


def _modulate_kernel(x_ref, mod_ref, o_ref, *, n_ctx_tiles, d):
    i = pl.program_id(0)
    sh = _mod_rows(mod_ref, i, n_ctx_tiles, 0, d)
    sc = _mod_rows(mod_ref, i, n_ctx_tiles, 1, d)
    o_ref[...] = (x_ref[...] * (1.0 + sc) + sh).astype(o_ref.dtype)


def modulate(x, mod, n_ctx):
    n, d = x.shape
    return pl.pallas_call(
        functools.partial(_modulate_kernel, n_ctx_tiles=(n - n_ctx) // ROW_TILE, d=d),
        grid=(n // ROW_TILE,),
        in_specs=[pl.BlockSpec((ROW_TILE, d), lambda i: (i, 0)),
                  pl.BlockSpec(mod.shape, lambda i: (0, 0))],
        out_specs=pl.BlockSpec((ROW_TILE, d), lambda i: (i, 0)),
        out_shape=jax.ShapeDtypeStruct((n, d), BF16),
        compiler_params=_cparams(("parallel",)),
        name="modulate",
    )(x, mod)


def _mm_kernel(x_ref, w_ref, o_ref):
    o_ref[...] = jnp.dot(x_ref[...], w_ref[...], preferred_element_type=F32).astype(o_ref.dtype)


def matmul(x, w, out_dtype, tn=512):
    m, k = x.shape
    n = w.shape[1]
    tm = _row_tile(m)
    return pl.pallas_call(
        _mm_kernel,
        grid=(m // tm, n // tn),
        in_specs=[pl.BlockSpec((tm, k), lambda i, j: (i, 0)),
                  pl.BlockSpec((k, tn), lambda i, j: (0, j))],
        out_specs=pl.BlockSpec((tm, tn), lambda i, j: (i, j)),
        out_shape=jax.ShapeDtypeStruct((m, n), out_dtype),
        compiler_params=_cparams(("parallel", "parallel")),
        name="matmul",
    )(x, w)


def _mm_rope_kernel(x_ref, w_ref, cos_ref, sin_ref, o_ref, *, tn):
    acc = jnp.dot(x_ref[...], w_ref[...], preferred_element_type=F32)
    cos = cos_ref[...]
    sin = sin_ref[...]
    lane = lax.broadcasted_iota(jnp.int32, cos.shape, 1)
    first = ((lane // 16) % 2) == 0
    for j in range(tn // LANE):
        blk = acc[:, j * LANE:(j + 1) * LANE]
        partner = jnp.where(first, pltpu.roll(blk, LANE - 16, 1), pltpu.roll(blk, 16, 1))
        o_ref[:, j * LANE:(j + 1) * LANE] = (blk * cos + partner * sin).astype(o_ref.dtype)


def matmul_rope(x, w, cos_t, sin_t, tn=512):
    m, k = x.shape
    n = w.shape[1]
    tm = _row_tile(m)
    return pl.pallas_call(
        functools.partial(_mm_rope_kernel, tn=tn),
        grid=(m // tm, n // tn),
        in_specs=[pl.BlockSpec((tm, k), lambda i, j: (i, 0)),
                  pl.BlockSpec((k, tn), lambda i, j: (0, j)),
                  pl.BlockSpec((tm, LANE), lambda i, j: (i, 0)),
                  pl.BlockSpec((tm, LANE), lambda i, j: (i, 0))],
        out_specs=pl.BlockSpec((tm, tn), lambda i, j: (i, j)),
        out_shape=jax.ShapeDtypeStruct((m, n), BF16),
        compiler_params=_cparams(("parallel", "parallel")),
        name="matmul_rope",
    )(x, w, cos_t, sin_t)


def rope_tables(n_ctx, n_lat):
    t = jnp.arange(n_lat)
    row = (t // GRID_W).astype(F32)
    col = (t % GRID_W).astype(F32)
    half = A_QK_DIM // 2
    inv = ROPE_BASE ** (-jnp.arange(0, half, 2, dtype=F32) / half)
    ang_r = row[:, None] * inv
    ang_c = col[:, None] * inv
    cos64 = jnp.concatenate([jnp.cos(ang_r), jnp.cos(ang_r), jnp.cos(ang_c), jnp.cos(ang_c)], axis=-1)
    sin64 = jnp.concatenate([-jnp.sin(ang_r), jnp.sin(ang_r), -jnp.sin(ang_c), jnp.sin(ang_c)], axis=-1)
    cos_l = jnp.concatenate([cos64, cos64], axis=-1)
    sin_l = jnp.concatenate([sin64, sin64], axis=-1)
    cos_t = jnp.concatenate([cos_l, jnp.ones((n_ctx, LANE), F32)], axis=0)
    sin_t = jnp.concatenate([sin_l, jnp.zeros((n_ctx, LANE), F32)], axis=0)
    return cos_t, sin_t


ATTN_ROW_SPLITS = 2


def _attn_kernel(lam_ref, q_ref, k_ref, v_ref, w_ref, o_ref, qq_s, m_s, acc_s, *, nk, tk, out_scale):
    ki = pl.program_id(2)

    @pl.when(ki == 0)
    def _():
        q = q_ref[...]
        lane = lax.broadcasted_iota(jnp.int32, q.shape, 1)
        zero = jnp.zeros_like(q)
        qq_s[0] = jnp.where(lane < A_QK_DIM, q, zero)
        qq_s[1] = jnp.where(lane >= A_QK_DIM, q, zero)
        m_s[...] = jnp.full(m_s.shape, -jnp.inf, F32)
        acc_s[...] = jnp.zeros(acc_s.shape, F32)

    k = k_ref[...]
    v = v_ref[...]
    v_ext = jnp.concatenate([v, jnp.ones_like(v)], axis=1)
    n_col = tk // LANE
    part = qq_s.shape[1] // ATTN_ROW_SPLITS
    chains = [(mi, slice(rp * part, (rp + 1) * part)) for mi in range(2) for rp in range(ATTN_ROW_SPLITS)]

    def scores(c):
        mi, rows = chains[c]
        return lax.dot_general(qq_s[mi, rows, :], k, (((1,), (1,)), ((), ())),
                               preferred_element_type=F32).astype(BF16)

    def softmax(c, s):
        mi, rows = chains[c]
        cm = s[:, 0:LANE]
        for cc in range(1, n_col):
            cm = jnp.maximum(cm, s[:, cc * LANE:(cc + 1) * LANE])
        m_old = m_s[mi, rows, :]
        m_new = jnp.maximum(m_old, jnp.max(cm.astype(F32), axis=1, keepdims=True))
        m_s[mi, rows, :] = m_new
        mb = m_new.astype(BF16)
        return jnp.exp2(s - jnp.concatenate([mb] * n_col, axis=1)), jnp.exp2(m_old - m_new)

    def accumulate(c, p, alpha):
        mi, rows = chains[c]
        acc_s[mi, rows, :] = (jnp.concatenate([alpha, alpha], axis=1) * acc_s[mi, rows, :]
                              + jnp.dot(p, v_ext, preferred_element_type=F32))

    n_ch = len(chains)
    s, pa = {}, {}
    for step in range(n_ch + 2):
        if step < n_ch:
            s[step] = scores(step)
        if 1 <= step <= n_ch:
            pa[step - 1] = softmax(step - 1, s.pop(step - 1))
        if 2 <= step:
            accumulate(step - 2, *pa.pop(step - 2))

    @pl.when(ki == nk - 1)
    def _():
        lam = lam_ref[0]
        o0 = acc_s[0, :, :LANE] / acc_s[0, :, LANE:LANE + 1]
        o1 = acc_s[1, :, :LANE] / acc_s[1, :, LANE:LANE + 1]
        o = o0 - lam * o1
        ms = jnp.mean(o * o, axis=-1, keepdims=True)
        o_ref[...] = (o * lax.rsqrt(ms + 1e-5) * w_ref[...] * out_scale).astype(o_ref.dtype)


def diff_attention(qk, v, lam, subln_w, lam_init, tq, tk, nq_tiles, q_tile0, nk, k_tile0):
    return pl.pallas_call(
        functools.partial(_attn_kernel, nk=nk, tk=tk, out_scale=1.0 - lam_init),
        grid=(A_HEADS, nq_tiles, nk),
        in_specs=[pl.BlockSpec(memory_space=pltpu.SMEM),
                  pl.BlockSpec((tq, LANE), lambda h, i, j: (q_tile0 + i, h)),
                  pl.BlockSpec((tk, LANE), lambda h, i, j: (k_tile0 + j, A_HEADS + h)),
                  pl.BlockSpec((tk, LANE), lambda h, i, j: (k_tile0 + j, h)),
                  pl.BlockSpec((1, LANE), lambda h, i, j: (0, 0))],
        out_specs=pl.BlockSpec((tq, LANE), lambda h, i, j: (i, h)),
        out_shape=jax.ShapeDtypeStruct((nq_tiles * tq, A_HEADS * A_V_DIM), BF16),
        scratch_shapes=[pltpu.VMEM((2, tq, LANE), BF16), pltpu.VMEM((2, tq, LANE), F32),
                        pltpu.VMEM((2, tq, 2 * LANE), F32)],
        compiler_params=_cparams(("parallel", "parallel", "arbitrary")),
        name="diff_attention",
    )(lam.reshape(1), qk, qk, v, subln_w.reshape(1, LANE))


def _rwkv_prep_kernel(z_ref, zp_ref, zn_ref, mu_ref, w2_ref, w0_ref, a2_ref, a0_ref, g2_ref, kk_ref, ka_ref,
                      eh_ref, r_o, v_o, kkn_o, g_o, kd_o, lw_o, bd_o, *, n_first, n_tot):
    i = pl.program_id(0)
    z = z_ref[...]
    tm = z.shape[0]
    grow = i * tm + lax.broadcasted_iota(jnp.int32, (tm, 1), 0)
    lrow = lax.broadcasted_iota(jnp.int32, (tm, 1), 0)
    prev = jnp.where(lrow == 0, zp_ref[7:8, :], pltpu.roll(z, 1, 0))
    nxt = jnp.where(lrow == tm - 1, zn_ref[0:1, :], pltpu.roll(z, tm - 1, 0))
    has_prev = jnp.logical_and(grow != 0, grow != n_first)
    has_next = jnp.logical_and(grow != n_first - 1, grow != n_tot - 1)
    prev = jnp.where(has_prev, prev, 0.0)
    nxt = jnp.where(has_next, nxt, 0.0)
    zm = z + (0.5 * (prev + nxt) - z) * mu_ref[...]
    c = B_WIDTH
    r = zm[:, 0:c]
    k = zm[:, c:2 * c]
    v = zm[:, 2 * c:3 * c]
    wd = zm[:, 3 * c:3 * c + 128]
    ad = zm[:, 3 * c + 128:3 * c + 256]
    gd = zm[:, 3 * c + 256:3 * c + 512]
    wl = w0_ref[...] + jnp.dot(jnp.tanh(wd), w2_ref[...], precision=HIGHEST, preferred_element_type=F32)
    al = a0_ref[...] + jnp.dot(ad, a2_ref[...], precision=HIGHEST, preferred_element_type=F32)
    g = _mm(jax.nn.sigmoid(gd), g2_ref[...])
    kk = k * kk_ref[...]
    ss = _mm_exact_rhs(kk * kk, eh_ref[...])
    kkn = kk / jnp.maximum(jnp.sqrt(ss), 1e-12)
    r_o[...] = r
    v_o[...] = v
    kkn_o[...] = kkn
    g_o[...] = g
    for d in range(2):
        w_log = -_softplus(-wl[:, d * c:(d + 1) * c]) - 0.5
        lw_o[d] = -jnp.exp(w_log)
        icl = jax.nn.sigmoid(al[:, d * c:(d + 1) * c])
        kd_o[d] = k * (1.0 + (icl - 1.0) * ka_ref[...])
        bd_o[d] = kkn * icl


def rwkv_prep(zb, mu, w2cat, w0cat, a2cat, a0cat, g2pad, k_k, k_a, ehead, n_ctx):
    n, wz = zb.shape
    tm = ROW_TILE
    nt = n // tm
    c = B_WIDTH
    full = lambda a: pl.BlockSpec(a.shape, lambda i: (0,) * a.ndim)
    tok = pl.BlockSpec((tm, c), lambda i: (i, 0))
    tok2 = pl.BlockSpec((2, tm, c), lambda i: (0, i, 0))
    sd = jax.ShapeDtypeStruct
    return pl.pallas_call(
        functools.partial(_rwkv_prep_kernel, n_first=n - n_ctx, n_tot=n),
        grid=(nt,),
        in_specs=[pl.BlockSpec((tm, wz), lambda i: (i, 0)),
                  pl.BlockSpec((8, wz), lambda i: (jnp.maximum(i * (tm // 8) - 1, 0), 0)),
                  pl.BlockSpec((8, wz), lambda i: (jnp.minimum((i + 1) * (tm // 8), n // 8 - 1), 0)),
                  full(mu), full(w2cat), full(w0cat), full(a2cat), full(a0cat), full(g2pad),
                  full(k_k), full(k_a), full(ehead)],
        out_specs=[tok, tok, tok, tok, tok2, tok2, tok2],
        out_shape=[sd((n, c), F32)] * 4 + [sd((2, n, c), F32)] * 3,
        compiler_params=_cparams(("parallel",)),
        name="rwkv_prep",
    )(zb, zb, zb, mu, w2cat, w0cat, a2cat, a0cat, g2pad, k_k, k_a, ehead)


def _rwkv_chunk_kernel(r_ref, v_ref, kk_ref, kd_ref, lw_ref, bd_ref, mincl_ref, mbef_ref, msame_ref,
                       gh_ref, qy_ref):
    lw = lw_ref[...]
    mincl = mincl_ref[...]
    cum_i = _mm_exact_lhs(mincl, lw)
    cum_e = cum_i - lw
    cum_l = _mm_exact_lhs(msame_ref[...], lw)
    r = r_ref[...]
    v = v_ref[...]
    a = -kk_ref[...]
    k = kd_ref[...]
    b = bd_ref[...]
    at = a * jnp.exp(cum_e)
    rt = r * jnp.exp(cum_i)
    einv = jnp.exp(-cum_i)
    bt = b * einv
    kt = k * einv
    e_l = jnp.exp(cum_l - cum_i)
    bh = b * e_l
    kh = k * e_l
    p_l = jnp.exp(cum_l)
    incl = mincl > 0.5
    bef = mbef_ref[...] > 0.5
    hd = B_HEAD_DIM
    nch = ROW_TILE // RWKV_CHUNK
    zeros_h = jnp.zeros((ROW_TILE, hd), F32)
    eye = (lax.broadcasted_iota(jnp.int32, (RWKV_CHUNK, LANE), 0)
           == lax.broadcasted_iota(jnp.int32, (RWKV_CHUNK, LANE), 1))
    ri = lax.broadcasted_iota(jnp.int32, (ROW_TILE, ROW_TILE), 0)
    ci = lax.broadcasted_iota(jnp.int32, (ROW_TILE, ROW_TILE), 1)
    xr = ri ^ ci
    eye_t = jnp.where(ri == ci, 1.0, 0.0)
    heads = range(RWKV_STEP_HEADS)
    sls = [slice(h * hd, (h + 1) * hd) for h in heads]
    a_ab = [jnp.where(bef, _mm_nt(at[:, sl], bt[:, sl]), 0.0) for sl in sls]
    a_ak = [jnp.where(bef, _mm_nt(at[:, sl], kt[:, sl]), 0.0) for sl in sls]
    m_rb = [jnp.where(incl, _mm_nt(rt[:, sl], bt[:, sl]), 0.0) for sl in sls]
    m_rk = [jnp.where(incl, _mm_nt(rt[:, sl], kt[:, sl]), 0.0) for sl in sls]
    lev0 = xr == 1
    t_inv = [eye_t + jnp.where(lev0, a_ab[h], 0.0) for h in heads]
    for sh in range(1, 6):
        lev = (xr >> sh) == 1
        inner = [_mm(jnp.where(lev, a_ab[h], 0.0), t_inv[h]) for h in heads]
        t_inv = [t_inv[h] + _mm(t_inv[h], inner[h]) for h in heads]
    x0 = [jnp.concatenate([at[:, sls[h]], _mm(a_ak[h], v[:, sls[h]])], axis=1) for h in heads]
    x = [_mm(t_inv[h], x0[h]) for h in heads]
    v0 = [jnp.concatenate([zeros_h, v[:, sls[h]]], axis=1) for h in heads]
    for h in heads:
        qy_ref[h] = (jnp.concatenate([rt[:, sls[h]], zeros_h], axis=1) + _mm(m_rb[h], x[h])
                     + _mm(m_rk[h], v0[h]))
    for c in range(nch):
        rows = slice(c * RWKV_CHUNK, (c + 1) * RWKV_CHUNK)
        for h in heads:
            sl = sls[h]
            gh = _mm_tn(bh[rows, sl], x[h][rows]) + _mm_tn(kh[rows, sl], v0[h][rows])
            plr = jnp.concatenate([p_l[c * RWKV_CHUNK:c * RWKV_CHUNK + 1, sl], jnp.zeros((1, hd), F32)], axis=1)
            gh_ref[h, rows, :] = gh + jnp.where(eye, jnp.broadcast_to(plr, (RWKV_CHUNK, LANE)), 0.0)


def rwkv_chunk(r, v, kkn, kd, lw, bd, mincl, mbef, msame):
    n, c = r.shape
    tm = ROW_TILE
    wide = RWKV_STEP_HEADS * B_HEAD_DIM
    tokb = pl.BlockSpec((tm, wide), lambda d, s, p: (s, p))
    tokd = pl.BlockSpec((None, tm, wide), lambda d, s, p: (d, s, p))
    mdir = pl.BlockSpec((None, tm, tm), lambda d, s, p: (d, 0, 0))
    outb = pl.BlockSpec((None, RWKV_STEP_HEADS, tm, LANE), lambda d, s, p: (d, p, s, 0))
    sd = jax.ShapeDtypeStruct((2, B_HEADS, n, LANE), F32)
    return pl.pallas_call(
        _rwkv_chunk_kernel,
        grid=(2, n // tm, B_HEADS // RWKV_STEP_HEADS),
        in_specs=[tokb, tokb, tokb, tokd, tokd, tokd, mdir, mdir,
                  pl.BlockSpec((tm, tm), lambda d, s, p: (0, 0))],
        out_specs=[outb, outb],
        out_shape=[sd, sd],
        compiler_params=_cparams(("parallel", "parallel", "parallel")),
        name="rwkv_chunk",
    )(r, v, kkn, kd, lw, bd, mincl, mbef, msame)


def _scan_tile(d, j, nt, n_ctx_tiles):
    n_lat_tiles = nt - n_ctx_tiles
    fwd = jnp.where(j < n_ctx_tiles, n_lat_tiles + j, j - n_ctx_tiles)
    return jnp.where(d == 0, fwd, nt - 1 - j)


def _rwkv_seq_kernel(gh_ref, qy_ref, y_ref, z_s):
    d = pl.program_id(0)
    j = pl.program_id(1)

    @pl.when(j == 0)
    def _():
        z_s[...] = jnp.zeros(z_s.shape, F32)

    hd = B_HEAD_DIM
    nch = ROW_TILE // RWKV_CHUNK
    for ci in range(nch):
        c = jnp.where(d == 0, ci, nch - 1 - ci)
        off = pl.multiple_of(c * RWKV_CHUNK, RWKV_CHUNK)
        ys = []
        for h in range(B_HEADS):
            z = z_s[h]
            gh = gh_ref[h, pl.ds(off, RWKV_CHUNK), :]
            qy = qy_ref[h, pl.ds(off, RWKV_CHUNK), :]
            ys.append(_mm(qy[:, :hd], z) + qy[:, hd:])
            z_s[h] = _mm(gh[:, :hd], z) + gh[:, hd:]
        y_ref[pl.ds(off, RWKV_CHUNK), :] = jnp.concatenate(ys, axis=1)


def rwkv_seq(gh, qy, n_ctx):
    _, hh, n, _ = gh.shape
    tm = ROW_TILE
    nt = n // tm
    nct = n_ctx // tm
    inb = pl.BlockSpec((None, hh, tm, LANE), lambda d, j: (d, 0, _scan_tile(d, j, nt, nct), 0))
    return pl.pallas_call(
        _rwkv_seq_kernel,
        grid=(2, nt),
        in_specs=[inb, inb],
        out_specs=pl.BlockSpec((None, tm, B_WIDTH), lambda d, j: (d, _scan_tile(d, j, nt, nct), 0)),
        out_shape=jax.ShapeDtypeStruct((2, n, B_WIDTH), F32),
        scratch_shapes=[pltpu.VMEM((hh, B_HEAD_DIM, B_HEAD_DIM), F32)],
        compiler_params=_cparams(("parallel", "arbitrary")),
        name="rwkv_seq",
    )(gh, qy)


def _rwkv_finish_kernel(y_ref, r_ref, v_ref, kd_ref, g_ref, rk_ref, lw_ref, lb_ref, eh_ref, o_ref):
    y = y_ref[0] + y_ref[1]
    eh = eh_ref[...]
    inv = 1.0 / B_HEAD_DIM
    mu = _mm_exact_rhs(y, eh) * inv
    yc = y - mu
    var = _mm_exact_rhs(yc * yc, eh) * inv
    yn = yc * lax.rsqrt(var + RWKV_LN_EPS) * lw_ref[...] + lb_ref[...]
    r = r_ref[...]
    rk = (r * kd_ref[0] + r * kd_ref[1]) * rk_ref[...]
    bonus = _mm_exact_rhs(rk, eh) * v_ref[...]
    o_ref[...] = ((yn + bonus) * g_ref[...]).astype(o_ref.dtype)


def rwkv_finish(y, r, v, kd, g, r_k, lnx_w, lnx_b, ehead):
    n, c = r.shape
    tm = ROW_TILE
    tok = pl.BlockSpec((tm, c), lambda i: (i, 0))
    tok2 = pl.BlockSpec((2, tm, c), lambda i: (0, i, 0))
    full = lambda a: pl.BlockSpec(a.shape, lambda i: (0,) * a.ndim)
    return pl.pallas_call(
        _rwkv_finish_kernel,
        grid=(n // tm,),
        in_specs=[tok2, tok, tok, tok2, tok, full(r_k), full(lnx_w), full(lnx_b), full(ehead)],
        out_specs=tok,
        out_shape=jax.ShapeDtypeStruct((n, c), BF16),
        compiler_params=_cparams(("parallel",)),
        name="rwkv_finish",
    )(y, r, v, kd, g, r_k, lnx_w, lnx_b, ehead)


def _layer_norm(x, g, b):
    mu = jnp.mean(x, axis=-1, keepdims=True)
    xc = x - mu
    var = jnp.mean(xc * xc, axis=-1, keepdims=True)
    return xc * lax.rsqrt(var + 1e-5) * g + b


def _store_token_rows(ref, val):
    tm = val.shape[0]
    for s in range(TOK_ROWS):
        ref[pl.ds(s, tm, stride=TOK_ROWS), :] = val[:, s * LANE:(s + 1) * LANE]


def _load_token_rows(ref, tm):
    return jnp.concatenate([ref[pl.ds(s, tm, stride=TOK_ROWS), :] for s in range(TOK_ROWS)], axis=1)


def _outproj_kernel(a_ref, w_ref, x_ref, mod_ref, g_ref, b_ref, x1_ref, f_ref, f3_ref, *, n_ctx_tiles, d):
    i = pl.program_id(0)
    m = jnp.dot(a_ref[...], w_ref[...], preferred_element_type=F32)
    gate = _mod_rows(mod_ref, i, n_ctx_tiles, 2, d)
    x1 = _layer_norm(DEEPNORM_ALPHA * x_ref[...] + gate * m, g_ref[...], b_ref[...])
    x1_ref[...] = x1
    sh = _mod_rows(mod_ref, i, n_ctx_tiles, 3, d)
    sc = _mod_rows(mod_ref, i, n_ctx_tiles, 4, d)
    f = x1 * (1.0 + sc) + sh
    f_ref[...] = f
    _store_token_rows(f3_ref, f)


def outproj_ln(a, w_out, x, mod, ln_g, ln_b, n_ctx):
    n, kdim = a.shape
    d = x.shape[1]
    tm = ROW_TILE
    tok = pl.BlockSpec((tm, d), lambda i: (i, 0))
    vec = pl.BlockSpec((1, d), lambda i: (0, 0))
    return pl.pallas_call(
        functools.partial(_outproj_kernel, n_ctx_tiles=(n - n_ctx) // tm, d=d),
        grid=(n // tm,),
        in_specs=[pl.BlockSpec((tm, kdim), lambda i: (i, 0)),
                  pl.BlockSpec((kdim, d), lambda i: (0, 0)),
                  tok, pl.BlockSpec(mod.shape, lambda i: (0, 0)), vec, vec],
        out_specs=[tok, tok, pl.BlockSpec((tm * TOK_ROWS, LANE), lambda i: (i, 0))],
        out_shape=[jax.ShapeDtypeStruct((n, d), F32)] * 2 + [jax.ShapeDtypeStruct((n * TOK_ROWS, LANE), F32)],
        compiler_params=_cparams(("parallel",)),
        name="outproj_ln",
    )(a, w_out, x, mod, ln_g.reshape(1, d), ln_b.reshape(1, d))


PICK_IDX, PICK_RANK, PICK_GATE = 0, 8, 16


def _router_kernel(f_ref, wr_hi_ref, wr_lo_ref, br_ref, wsg_ref, wsd_ref, ltri_ref, pick_ref, sh_ref, cnt_ref,
                   carry_s):
    i = pl.program_id(0)

    @pl.when(i == 0)
    def _():
        carry_s[...] = jnp.zeros(carry_s.shape, F32)

    f = f_ref[...]
    f_hi = f.astype(BF16)
    f_lo = (f - f_hi.astype(F32)).astype(BF16)
    dd = lambda a, b: jnp.dot(a, b[...], preferred_element_type=F32)
    logits = dd(f_hi, wr_hi_ref) + dd(f_lo, wr_hi_ref) + dd(f_hi, wr_lo_ref)
    scores = jax.nn.sigmoid(logits)
    lane = lax.broadcasted_iota(jnp.int32, scores.shape, 1).astype(F32)
    neg = jnp.float32(-jnp.inf)
    sel = jnp.where(lane < N_EXPERTS, scores + br_ref[...], neg)
    top = jnp.zeros(scores.shape, F32)
    chosen = jnp.zeros(scores.shape, F32)
    hits, firsts = [], []
    for j in range(TOP_K):
        mx = jnp.max(sel, axis=1, keepdims=True)
        first = jnp.min(jnp.where(sel == mx, lane, float(LANE)), axis=1, keepdims=True)
        hit = lane == first
        top = jnp.where(hit, scores, top)
        chosen = jnp.where(hit, 1.0, chosen)
        sel = jnp.where(hit, neg, sel)
        hits.append(hit)
        firsts.append(first)
    gates = top / jnp.sum(top, axis=1, keepdims=True) * ROUTE_SCALE
    carry = carry_s[0:1, :]
    rank = dd(ltri_ref[...], chosen.astype(BF16)) + carry
    pick = jnp.zeros(scores.shape, F32)
    for j in range(TOP_K):
        pick = jnp.where(lane == PICK_IDX + j, firsts[j], pick)
        rank_j = jnp.sum(jnp.where(hits[j], rank, 0.0), axis=1, keepdims=True)
        pick = jnp.where(lane == PICK_RANK + j, rank_j, pick)
        gate_j = jnp.sum(jnp.where(hits[j], gates, 0.0), axis=1, keepdims=True)
        pick = jnp.where(lane == PICK_GATE + j, gate_j, pick)
    pick_ref[...] = pick
    carry_s[0:1, :] = carry + jnp.sum(chosen, axis=0, keepdims=True)
    cnt_ref[...] = carry_s[...]
    ab = dd(f_hi, wsg_ref)
    hdn = _silu(ab[:, :SHARED_FF]) * ab[:, SHARED_FF:]
    sh_ref[...] = dd(hdn.astype(BF16), wsd_ref)


def router_shared(f_in, wr_hi, wr_lo, br, ws_gu, ws_down):
    n, d = f_in.shape
    tm = ROW_TILE
    full = lambda a: pl.BlockSpec(a.shape, lambda i: (0,) * a.ndim)
    sd = jax.ShapeDtypeStruct
    ltri = jnp.asarray(np.tril(np.ones((tm, tm), np.float32), -1), BF16)
    return pl.pallas_call(
        _router_kernel,
        grid=(n // tm,),
        in_specs=[pl.BlockSpec((tm, d), lambda i: (i, 0)), full(wr_hi), full(wr_lo), full(br),
                  full(ws_gu), full(ws_down), full(ltri)],
        out_specs=[pl.BlockSpec((tm, LANE), lambda i: (i, 0)), pl.BlockSpec((tm, d), lambda i: (i, 0)),
                   pl.BlockSpec((8, LANE), lambda i: (0, 0))],
        out_shape=[sd((n, LANE), F32), sd((n, d), F32), sd((8, LANE), F32)],
        scratch_shapes=[pltpu.VMEM((8, LANE), F32)],
        compiler_params=_cparams(("arbitrary",)),
        name="router_shared",
    )(f_in, wr_hi, wr_lo, br, ws_gu, ws_down, ltri)


def _slab(ref, row):
    return ref.at[pl.ds(pl.multiple_of(row * TOK_ROWS, TOK_ROWS), TOK_ROWS)]


def _dispatch_kernel(slots_ref, padlo_ref, padhi_ref, f3_ref, xs_hbm, zrow, sem, *, tm, n_tiles):
    i = pl.program_id(0)

    def body(t, carry):
        for j in range(TOP_K):
            slot = slots_ref[(i * tm + t) * TOP_K + j]
            pltpu.make_async_copy(_slab(f3_ref, t), _slab(xs_hbm, slot), sem.at[0]).start()
        return carry
    lax.fori_loop(0, tm, body, 0)
    for _ in range(TOP_K):
        pltpu.make_async_copy(f3_ref, xs_hbm.at[pl.ds(0, tm * TOK_ROWS)], sem.at[0]).wait()

    @pl.when(i == n_tiles - 1)
    def _():
        zrow[...] = jnp.zeros(zrow.shape, F32)

        def per_expert(e, carry):
            lo = padlo_ref[e]
            hi = padhi_ref[e]

            def fill(s, c):
                pltpu.make_async_copy(zrow, _slab(xs_hbm, s), sem.at[1]).start()
                return c
            lax.fori_loop(lo, hi, fill, 0)

            def drain(s, c):
                pltpu.make_async_copy(zrow, _slab(xs_hbm, 0), sem.at[1]).wait()
                return c
            lax.fori_loop(lo, hi, drain, 0)
            return carry
        lax.fori_loop(0, N_EXPERTS + 1, per_expert, 0)


def moe_dispatch_rows(f3, tok_slots, pad_lo, pad_hi, n_slots):
    tm = ROW_TILE
    n = f3.shape[0] // TOK_ROWS
    grid_spec = pltpu.PrefetchScalarGridSpec(
        num_scalar_prefetch=3,
        grid=(n // tm,),
        in_specs=[pl.BlockSpec((tm * TOK_ROWS, LANE), lambda i, sl, lo, hi: (i, 0))],
        out_specs=pl.BlockSpec(memory_space=pl.ANY),
        scratch_shapes=[pltpu.VMEM((TOK_ROWS, LANE), F32), pltpu.SemaphoreType.DMA((2,))],
    )
    return pl.pallas_call(
        functools.partial(_dispatch_kernel, tm=tm, n_tiles=n // tm),
        grid_spec=grid_spec,
        out_shape=jax.ShapeDtypeStruct((n_slots * TOK_ROWS, LANE), F32),
        compiler_params=_cparams(("arbitrary",)),
        name="moe_dispatch",
    )(tok_slots, pad_lo, pad_hi, f3)


def _expert_kernel(be_ref, nu_ref, xs_ref, wgu_ref, wdn_ref, ys_ref, wgu_bf, wdn_bf):
    i = pl.program_id(0)

    @pl.when(jnp.logical_or(i == 0, be_ref[i] != be_ref[jnp.maximum(i - 1, 0)]))
    def _():
        wgu_bf[...] = wgu_ref[...].astype(BF16)
        wdn_bf[...] = wdn_ref[...].astype(BF16)

    @pl.when(i < nu_ref[0])
    def _():
        x = _load_token_rows(xs_ref, EXPERT_BLOCK).astype(BF16)
        ab = jnp.dot(x, wgu_bf[...], preferred_element_type=F32)
        hdn = _silu(ab[:, :EXPERT_FF]) * ab[:, EXPERT_FF:]
        y = jnp.dot(hdn.astype(BF16), wdn_bf[...], preferred_element_type=F32)
        _store_token_rows(ys_ref, y)

    @pl.when(i >= nu_ref[0])
    def _():
        ys_ref[...] = jnp.zeros(ys_ref.shape, F32)


def routed_experts(xs, block_expert, n_used, w_gu_all, w_down_all, layer):
    d = w_gu_all.shape[2]
    n_blocks = block_expert.shape[0]
    rows = EXPERT_BLOCK * TOK_ROWS
    last_used = lambda i, nu: jnp.minimum(i, jnp.maximum(nu[0] - 1, 0))
    grid_spec = pltpu.PrefetchScalarGridSpec(
        num_scalar_prefetch=2,
        grid=(n_blocks,),
        in_specs=[pl.BlockSpec((rows, LANE), lambda i, be, nu: (last_used(i, nu), 0)),
                  pl.BlockSpec((None, None, d, 2 * EXPERT_FF), lambda i, be, nu: (layer, be[i], 0, 0)),
                  pl.BlockSpec((None, None, EXPERT_FF, d), lambda i, be, nu: (layer, be[i], 0, 0))],
        out_specs=pl.BlockSpec((rows, LANE), lambda i, be, nu: (i, 0)),
        scratch_shapes=[pltpu.VMEM((d, 2 * EXPERT_FF), BF16), pltpu.VMEM((EXPERT_FF, d), BF16)],
    )
    return pl.pallas_call(
        _expert_kernel,
        grid_spec=grid_spec,
        out_shape=jax.ShapeDtypeStruct((n_blocks * rows, LANE), F32),
        compiler_params=_cparams(("arbitrary",)),
        name="routed_experts",
    )(block_expert, n_used, xs, w_gu_all, w_down_all)


def moe_slots(pick, counts, n_blocks):
    eb = EXPERT_BLOCK
    idx6 = pick[:, PICK_IDX:PICK_IDX + TOP_K].astype(jnp.int32)
    rank6 = pick[:, PICK_RANK:PICK_RANK + TOP_K].astype(jnp.int32)
    padded = (counts + eb - 1) // eb * eb
    pad_end = jnp.cumsum(padded)
    pad_start = pad_end - padded
    experts = jnp.arange(N_EXPERTS, dtype=jnp.int32)
    start6 = jnp.sum(jnp.where(idx6[:, :, None] == experts, pad_start, 0), axis=-1)
    tok_slots = (start6 + rank6).reshape(-1).astype(jnp.int32)
    blk_start = jnp.arange(n_blocks, dtype=jnp.int32) * eb
    block_expert = jnp.minimum(jnp.sum(pad_end[None, :] <= blk_start[:, None], axis=1), N_EXPERTS - 1)
    block_expert = block_expert.astype(jnp.int32)
    n_used = (pad_end[-1:] // eb).astype(jnp.int32)
    fill_lo = jnp.concatenate([pad_start + counts, pad_end[-1:]]).astype(jnp.int32)
    fill_hi = jnp.concatenate([pad_end, jnp.full((1,), n_blocks * eb, pad_end.dtype)]).astype(jnp.int32)
    return tok_slots, block_expert, n_used, fill_lo, fill_hi


def _combine_kernel(slots_ref, ys_hbm, pick_ref, sh_ref, x1_ref, mod_ref, g_ref, b_ref, *rest,
                    tm, n_tiles, n_ctx_tiles, d, with_next):
    if with_next:
        modn_ref, x2_ref, h_ref, buf, sem = rest
    else:
        x2_ref, buf, sem = rest
    i = pl.program_id(0)
    s = i % 2
    tok_rows = TOP_K * TOK_ROWS

    def start_gather(tile, slot):
        def body(t, carry):
            for j in range(TOP_K):
                src = slots_ref[(tile * tm + t) * TOP_K + j]
                pltpu.make_async_copy(_slab(ys_hbm, src), _slab(buf.at[slot], t * TOP_K + j), sem.at[slot]).start()
            return carry
        lax.fori_loop(0, tm, body, 0)

    @pl.when(i == 0)
    def _():
        start_gather(0, 0)

    @pl.when(i + 1 < n_tiles)
    def _():
        start_gather(i + 1, 1 - s)

    pltpu.make_async_copy(ys_hbm.at[pl.ds(0, tm * tok_rows)], buf.at[s], sem.at[s]).wait()
    pieces = []
    for c in range(TOK_ROWS):
        acc = sh_ref[:, c * LANE:(c + 1) * LANE]
        for j in range(TOP_K):
            gate_j = pick_ref[:, PICK_GATE + j:PICK_GATE + j + 1]
            acc = acc + gate_j * buf[s, pl.ds(j * TOK_ROWS + c, tm, stride=tok_rows), :]
        pieces.append(acc)
    f = jnp.concatenate(pieces, axis=1)
    gate = _mod_rows(mod_ref, i, n_ctx_tiles, 5, d)
    x2 = _layer_norm(DEEPNORM_ALPHA * x1_ref[...] + gate * f, g_ref[...], b_ref[...])
    x2_ref[...] = x2
    if with_next:
        sh = _mod_rows(modn_ref, i, n_ctx_tiles, 0, d)
        sc = _mod_rows(modn_ref, i, n_ctx_tiles, 1, d)
        h_ref[...] = (x2 * (1.0 + sc) + sh).astype(h_ref.dtype)


def moe_combine(ys, tok_slots, pick, shared, x1, mod, ln_g, ln_b, n_ctx, mod_next=None):
    n, d = x1.shape
    tm = COMBINE_TILE
    tok = pl.BlockSpec((tm, d), lambda i, sl: (i, 0))
    vec = pl.BlockSpec((1, d), lambda i, sl: (0, 0))
    with_next = mod_next is not None
    in_specs = [pl.BlockSpec(memory_space=pl.ANY), pl.BlockSpec((tm, LANE), lambda i, sl: (i, 0)), tok, tok,
                pl.BlockSpec(mod.shape, lambda i, sl: (0, 0)), vec, vec]
    args = [ys, pick, shared, x1, mod, ln_g.reshape(1, d), ln_b.reshape(1, d)]
    out_specs = [tok]
    out_shape = [jax.ShapeDtypeStruct((n, d), F32)]
    if with_next:
        in_specs.append(pl.BlockSpec(mod_next.shape, lambda i, sl: (0, 0)))
        args.append(mod_next)
        out_specs.append(tok)
        out_shape.append(jax.ShapeDtypeStruct((n, d), BF16))
    grid_spec = pltpu.PrefetchScalarGridSpec(
        num_scalar_prefetch=1,
        grid=(n // tm,),
        in_specs=in_specs, out_specs=out_specs,
        scratch_shapes=[pltpu.VMEM((2, tm * TOP_K * TOK_ROWS, LANE), F32), pltpu.SemaphoreType.DMA((2,))],
    )
    return pl.pallas_call(
        functools.partial(_combine_kernel, tm=tm, n_tiles=n // tm, n_ctx_tiles=(n - n_ctx) // tm, d=d,
                          with_next=with_next),
        grid_spec=grid_spec,
        out_shape=out_shape,
        compiler_params=_cparams(("arbitrary",)),
        name="moe_combine",
    )(tok_slots, *args)


def moe_layer(f_in, f3, x1, mod, ln_g, ln_b, w_router, b_router, w_gu_all, w_down_all, layer, ws_gu, ws_down,
              n_ctx, mod_next=None):
    n, d = f_in.shape
    wr = jnp.pad(w_router, ((0, 0), (0, LANE - N_EXPERTS)))
    wr_hi = wr.astype(BF16)
    wr_lo = (wr - wr_hi.astype(F32)).astype(BF16)
    br = jnp.pad(b_router.astype(F32), (0, LANE - N_EXPERTS)).reshape(1, LANE)
    pick, shared, cnt = router_shared(f_in, wr_hi, wr_lo, br, ws_gu.astype(BF16), ws_down.astype(BF16))
    counts = cnt[0, :N_EXPERTS].astype(jnp.int32)
    n_blocks = -(-(n * TOP_K + N_EXPERTS * (EXPERT_BLOCK - 1)) // EXPERT_BLOCK)
    tok_slots, block_expert, n_used, pad_lo, pad_hi = moe_slots(pick, counts, n_blocks)
    xs = moe_dispatch_rows(f3, tok_slots, pad_lo, pad_hi, n_blocks * EXPERT_BLOCK)
    ys = routed_experts(xs, block_expert, n_used, w_gu_all, w_down_all, layer)
    return moe_combine(ys, tok_slots, pick, shared, x1, mod, ln_g, ln_b, n_ctx, mod_next)


HGRN_SMALL_LEVELS = (4, 2)
HGRN_STEP_HEADS = 2


def _hgrn_levels():
    t = ROW_TILE
    levels = []
    b = t // 2
    while b >= 1:
        levels.append(b)
        b //= 2
    return levels


def _hgrn_kernel(q_ref, f_ref, i_ref, lb_ref, mlev_ref, mincl_ref, o_ref, st_s, cum_s):
    d = pl.program_id(0)
    j = pl.program_id(2)

    @pl.when(j == 0)
    def _():
        st_s[...] = jnp.zeros(st_s.shape, F32)

    t = ROW_TILE
    lb = lb_ref[...]
    log_lb = jnp.log(lb)
    log_rest = jnp.log1p(-lb)
    fz = f_ref[...]
    q = _silu(q_ref[...])
    v = i_ref[...]
    log_sig = jnp.minimum(fz, 0.0) - jnp.log1p(jnp.exp(-jnp.abs(fz)))
    u = log_rest + log_sig
    mx = jnp.maximum(log_lb, u)
    logf = mx + jnp.log1p(jnp.exp(-jnp.abs(log_lb - u)))
    key = (1.0 - lb) * jax.nn.sigmoid(-fz)

    cum = _mm_exact_lhs(mincl_ref[...], logf)
    cum_l = jnp.where(d == 0, cum[t - 1:t, :], cum[0:1, :])
    cum_s[...] = cum
    d_small = _mm_exact_lhs(mlev_ref[...], logf)

    wide = HGRN_STEP_HEADS * LANE
    heads = [slice(h * LANE, (h + 1) * LANE) for h in range(HGRN_STEP_HEADS)]
    row = lax.broadcasted_iota(jnp.int32, (t, 1), 0)
    xr = (lax.broadcasted_iota(jnp.int32, (t, t), 0) ^ lax.broadcasted_iota(jnp.int32, (t, t), 1))
    a = [jnp.zeros((t, t), F32) for _ in heads]
    for li, b in enumerate(_hgrn_levels()):
        q_side = ((row // b) % 2) == jnp.where(d == 0, 1, 0)
        if b >= 8:
            piv = jnp.concatenate(
                [jnp.broadcast_to(cum_s[pl.ds(g * 2 * b + b - 1 + d, 1), :], (2 * b, wide))
                 for g in range(t // (2 * b))], axis=0)
            dlev = jnp.where(q_side, cum - piv, piv - cum)
        elif b > 1:
            k = HGRN_SMALL_LEVELS.index(b)
            dlev = d_small[k * t:(k + 1) * t]
        else:
            dlev = jnp.where(q_side, logf, 0.0)
        e = jnp.exp(dlev)
        qt = jnp.where(q_side, q * e, 0.0)
        kt = jnp.where(q_side, 0.0, key * e)
        same_group = (xr >> (b.bit_length())) == 0
        for h, sl in enumerate(heads):
            a[h] = a[h] + jnp.where(same_group, _mm_nt(qt[:, sl], kt[:, sl]), 0.0)
    q_in = q * jnp.exp(cum)
    k_out = key * jnp.exp(cum_l - cum)
    decay_l = jnp.exp(cum_l)
    qk = q * key
    for h, sl in enumerate(heads):
        st = st_s[h]
        o_ref[:, sl] = (_mm(a[h], v[:, sl]) + _mm_nt(q_in[:, sl], st)
                        + jnp.sum(qk[:, sl], axis=1, keepdims=True) * v[:, sl])
        st_s[h] = st * decay_l[:, sl] + _mm_tn(v[:, sl], k_out[:, sl])


def hgrn_matrices():
    t = ROW_TILE
    ti = np.arange(t)[:, None]
    ii = np.arange(t)[None, :]
    mlev = np.zeros((2, len(HGRN_SMALL_LEVELS), t, t), np.float32)
    for li, b in enumerate(HGRN_SMALL_LEVELS):
        bnd = (ti // (2 * b)) * (2 * b) + b - 1
        upper = (ti // b) % 2 == 1
        mlev[0, li] = np.where(upper, (ii > bnd) & (ii <= ti), (ii > ti) & (ii <= bnd))
        mlev[1, li] = np.where(upper, (ii > bnd) & (ii < ti), (ii >= ti) & (ii <= bnd))
    mincl = np.stack([(ii <= ti), (ii >= ti)]).astype(np.float32)
    return (jnp.asarray(mlev.reshape(2, -1, t), BF16), jnp.asarray(mincl, BF16))


def hgrn_scan(z, lb, mlev, mincl, n_ctx):
    n = z.shape[0]
    t = ROW_TILE
    nt = n // t
    nct = n_ctx // t
    hh = C_HEADS
    sh = HGRN_STEP_HEADS
    wide = sh * LANE
    hb = hh // sh
    srow = lambda d, h, j: _scan_tile(d, j, nt, nct)
    return pl.pallas_call(
        _hgrn_kernel,
        grid=(2, hb, nt),
        in_specs=[pl.BlockSpec((t, wide), lambda d, h, j: (srow(d, h, j), h)),
                  pl.BlockSpec((t, wide), lambda d, h, j: (srow(d, h, j), hb * (1 + d) + h)),
                  pl.BlockSpec((t, wide), lambda d, h, j: (srow(d, h, j), 3 * hb + h)),
                  pl.BlockSpec((None, 1, wide), lambda d, h, j: (h, 0, 0)),
                  pl.BlockSpec((None,) + mlev.shape[1:], lambda d, h, j: (d, 0, 0)),
                  pl.BlockSpec((None, t, t), lambda d, h, j: (d, 0, 0))],
        out_specs=pl.BlockSpec((None, t, wide), lambda d, h, j: (d, srow(d, h, j), h)),
        out_shape=jax.ShapeDtypeStruct((2, n, hh * LANE), F32),
        scratch_shapes=[pltpu.VMEM((sh, LANE, LANE), F32), pltpu.VMEM((t, wide), F32)],
        compiler_params=_cparams(("parallel", "parallel", "arbitrary")),
        name="hgrn_scan",
    )(z, z, z, lb.reshape(hb, 1, wide), mlev, mincl)


def _hgrn_finish_kernel(o_ref, g_ref, w_ref, out_ref):
    w = w_ref[...]
    for h in range(C_HEADS):
        sl = slice(h * LANE, (h + 1) * LANE)
        o = o_ref[0, :, sl] + o_ref[1, :, sl]
        ms = jnp.mean(o * o, axis=-1, keepdims=True)
        out_ref[:, sl] = (o * lax.rsqrt(ms + 1e-5) * w * _silu(g_ref[:, sl])).astype(out_ref.dtype)


def hgrn_finish(o, z, norm_w):
    n = z.shape[0]
    t = ROW_TILE
    dm = C_HEADS * LANE
    return pl.pallas_call(
        _hgrn_finish_kernel,
        grid=(n // t,),
        in_specs=[pl.BlockSpec((2, t, dm), lambda i: (0, i, 0)),
                  pl.BlockSpec((t, dm), lambda i: (i, 4)),
                  pl.BlockSpec((1, LANE), lambda i: (0, 0))],
        out_specs=pl.BlockSpec((t, dm), lambda i: (i, 0)),
        out_shape=jax.ShapeDtypeStruct((n, dm), BF16),
        compiler_params=_cparams(("parallel",)),
        name="hgrn_finish",
    )(o, z, norm_w.reshape(1, LANE))


def rwkv_matrices():
    t = ROW_TILE
    ti = np.arange(t)[:, None]
    ii = np.arange(t)[None, :]
    same = (ti // RWKV_CHUNK) == (ii // RWKV_CHUNK)
    mincl = np.stack([same & (ii <= ti), same & (ii >= ti)]).astype(np.float32)
    mbef = np.stack([same & (ii < ti), same & (ii > ti)]).astype(np.float32)
    return (jnp.asarray(mincl, BF16), jnp.asarray(mbef, BF16), jnp.asarray(same.astype(np.float32), BF16))


def even_mixer(h, n_ctx, w_in, lam_params, subln_w, mu, w0, w2, a0, a2, g2, k_k, k_a, r_k, lnx_w, lnx_b,
               layer_idx, rope):
    n, d = h.shape
    n_lat = n - n_ctx
    hq = A_HEADS * 2 * A_QK_DIM
    a_in = 3 * hq
    cos_t, sin_t = rope
    q_scale = (A_QK_DIM ** -0.5) * math.log2(math.e)
    w_qk = jnp.concatenate([w_in[:, :hq] * q_scale, w_in[:, hq:2 * hq]], axis=1).astype(BF16)
    qk = matmul_rope(h, w_qk, cos_t, sin_t)
    v = matmul(h, w_in[:, 2 * hq:a_in].astype(BF16), BF16)
    b_in = w_in.shape[1] - a_in
    wz = 3 * B_WIDTH + 512
    w_b = jnp.pad(w_in[:, a_in:], ((0, 0), (0, wz - b_in))).astype(BF16)
    zb = matmul(h, w_b, F32)

    lam_f = lam_params.astype(F32)
    lam_init = 0.8 - 0.6 * math.exp(-0.3 * layer_idx)
    lam = jnp.exp(jnp.sum(lam_f[0] * lam_f[1])) - jnp.exp(jnp.sum(lam_f[2] * lam_f[3])) + lam_init
    tq = _row_tile(n_lat, (2048, 1024, 512, 256))
    tk = _row_tile(n, (1280, 640, 256))
    o_lat = diff_attention(qk, v, lam, subln_w, lam_init, tq, tk, n_lat // tq, 0, n // tk, 0)
    o_ctx = diff_attention(qk, v, lam, subln_w, lam_init, n_ctx, n_ctx, 1, n_lat // n_ctx, 1, n_lat // n_ctx)
    a_out = jnp.concatenate([o_lat, o_ctx], axis=0)

    c = B_WIDTH
    mu_p = jnp.pad(mu, (0, wz - b_in)).reshape(1, wz)
    zc = jnp.zeros((B_DECAY_RANK, c), F32)
    w2cat = jnp.concatenate([jnp.concatenate([w2[0], zc], 1), jnp.concatenate([zc, w2[1]], 1)], 0)
    a2cat = jnp.concatenate([jnp.concatenate([a2[0], zc], 1), jnp.concatenate([zc, a2[1]], 1)], 0)
    w0cat = w0.reshape(1, 2 * c)
    a0cat = a0.reshape(1, 2 * c)
    g2pad = jnp.pad(g2, ((0, 256 - B_GATE_RANK), (0, 0))).astype(BF16)
    hid = np.arange(c) // B_HEAD_DIM
    ehead = jnp.asarray((hid[:, None] == hid[None, :]).astype(np.float32), BF16)
    r, vv, kkn, g, kd, lw, bd = rwkv_prep(zb, mu_p, w2cat, w0cat, a2cat, a0cat, g2pad, k_k.reshape(1, c),
                                          k_a.reshape(1, c), ehead, n_ctx)
    mincl, mbef, msame = rwkv_matrices()
    gh, qy = rwkv_chunk(r, vv, kkn, kd, lw, bd, mincl, mbef, msame)
    y = rwkv_seq(gh, qy, n_ctx)
    b_out = rwkv_finish(y, r, vv, kd, g, r_k.reshape(1, c), lnx_w.reshape(1, c), lnx_b.reshape(1, c), ehead)
    return jnp.concatenate([a_out, b_out], axis=1)


def odd_mixer(h, n_ctx, w_in, lb, norm_w):
    z = matmul(h, w_in.astype(BF16), F32)
    mlev, mincl = hgrn_matrices()
    o = hgrn_scan(z, lb, mlev, mincl, n_ctx)
    return hgrn_finish(o, z, norm_w)


def kernel(x, c, ctx, c_ctx, ada_w, ada_b, ln_g, ln_b, even_w_in, even_w_out, diff_lambda, diff_subln_w, rwkv_mu, rwkv_w0, rwkv_w2, rwkv_a0, rwkv_a2, rwkv_g2, rwkv_k_k, rwkv_k_a, rwkv_r_k, rwkv_lnx_w, rwkv_lnx_b, odd_w_in, odd_w_out, hgrn_lb_raw, hgrn_norm_w, moe_router_w, moe_router_b, moe_w_gu, moe_w_down, moe_shared_gu, moe_shared_down):
    bsz, n_lat, d = x.shape
    n_ctx = ctx.shape[1]
    assert bsz == 1 and n_ctx % ROW_TILE == 0 and n_lat % ROW_TILE == 0
    depth = ada_w.shape[0]
    xs = jnp.concatenate([x[0], ctx[0]], axis=0)
    c8 = jnp.zeros((8, d), F32).at[0].set(c[0]).at[1].set(c_ctx)
    mods = adaln(c8, ada_w, ada_b)
    lb_soft = jax.nn.softmax(hgrn_lb_raw.astype(F32), axis=0)
    lower_bounds = jnp.cumsum(lb_soft, axis=0) - lb_soft[0]
    rope = rope_tables(n_ctx, n_lat)

    h = modulate(xs, mods[0], n_ctx)
    for l in range(depth):
        mod = mods[l]
        if l % 2 == 0:
            e = l // 2
            mix = even_mixer(h, n_ctx, even_w_in[e], diff_lambda[e], diff_subln_w[e], rwkv_mu[e], rwkv_w0[e],
                             rwkv_w2[e], rwkv_a0[e], rwkv_a2[e], rwkv_g2[e], rwkv_k_k[e], rwkv_k_a[e],
                             rwkv_r_k[e], rwkv_lnx_w[e], rwkv_lnx_b[e], l, rope)
            w_out = even_w_out[e]
        else:
            o = l // 2
            mix = odd_mixer(h, n_ctx, odd_w_in[o], lower_bounds[l].reshape(C_HEADS, C_KEY_DIM), hgrn_norm_w[o])
            w_out = odd_w_out[o]
        x1, f_in, f3 = outproj_ln(mix, w_out.astype(BF16), xs, mod, ln_g[l, 0], ln_b[l, 0], n_ctx)
        mod_next = mods[l + 1] if l + 1 < depth else None
        res = moe_layer(f_in, f3, x1, mod, ln_g[l, 1], ln_b[l, 1], moe_router_w[l], moe_router_b[l], moe_w_gu,
                        moe_w_down, l, moe_shared_gu[l], moe_shared_down[l], n_ctx, mod_next)
        if mod_next is not None:
            xs, h = res
        else:
            (xs,) = res
    return xs[:n_lat].reshape(bsz, n_lat, d)
```

```python
import functools
import math

import numpy as np
import jax
import jax.numpy as jnp
from jax import lax
from jax.experimental import pallas as pl
from jax.experimental.pallas import tpu as pltpu

F32 = jnp.float32
BF16 = jnp.bfloat16
HIGHEST = lax.Precision.HIGHEST

GRID_W = 64
A_HEADS = 8
A_QK_DIM = 64
A_V_DIM = 128
ROPE_BASE = 10000.0
B_HEADS = 16
B_HEAD_DIM = 64
B_WIDTH = 1024
B_DECAY_RANK = 64
B_ICL_RANK = 64
B_GATE_RANK = 160
RWKV_LN_EPS = 64e-5
C_HEADS = 16
C_KEY_DIM = 128
N_EXPERTS = 64
TOP_K = 6
EXPERT_FF = 512
SHARED_FF = 1024
ROUTE_SCALE = 2.5
DEPTH = 2
DEEPNORM_ALPHA = (2 * DEPTH) ** 0.25

ROW_TILE = 256
RWKV_STEP_HEADS = 8
RWKV_CHUNK = 64
EXPERT_BLOCK = 256
LANE = 128
TOK_ROWS = 16
COMBINE_TILE = 128
DMA_ISSUE_UNROLL = 4
VMEM_LIMIT = 56 * 1024 * 1024


def _cparams(sem):
    return pltpu.CompilerParams(dimension_semantics=sem, vmem_limit_bytes=VMEM_LIMIT)


def _mm(a, b):
    return jnp.dot(a.astype(BF16), b.astype(BF16), preferred_element_type=F32)


def _mm_nt(a, b):
    return lax.dot_general(a.astype(BF16), b.astype(BF16), (((1,), (1,)), ((), ())),
                           preferred_element_type=F32)


def _mm_tn(a, b):
    return lax.dot_general(a.astype(BF16), b.astype(BF16), (((0,), (0,)), ((), ())),
                           preferred_element_type=F32)


def _split3(x):
    hi = x.astype(BF16)
    r1 = x - hi.astype(F32)
    mid = r1.astype(BF16)
    lo = (r1 - mid.astype(F32)).astype(BF16)
    return hi, mid, lo


def _mm_exact_lhs(m_bf16, x):
    hi, mid, lo = _split3(x)
    d = lambda t: jnp.dot(m_bf16, t, preferred_element_type=F32)
    return d(hi) + d(mid) + d(lo)


def _mm_exact_rhs(x, m_bf16):
    hi, mid, lo = _split3(x)
    d = lambda t: jnp.dot(t, m_bf16, preferred_element_type=F32)
    return d(hi) + d(mid) + d(lo)


def _silu(x):
    return x * jax.nn.sigmoid(x)


def _softplus(x):
    return jnp.maximum(x, 0.0) + jnp.log1p(jnp.exp(-jnp.abs(x)))


def _row_tile(m, cands=(1280, 640, 512, 256)):
    for t in cands:
        if m % t == 0:
            return t
    raise ValueError(f"no row tile for {m}")


def _adaln_kernel(c_ref, w_ref, b_ref, o_ref):
    s = _silu(c_ref[...])
    o_ref[...] = _mm(s, w_ref[...]) + b_ref[...]


def adaln(c8, ada_w, ada_b):
    depth, d, n6 = ada_w.shape
    tn = 512
    return pl.pallas_call(
        _adaln_kernel,
        grid=(depth, n6 // tn),
        in_specs=[pl.BlockSpec((8, d), lambda l, j: (0, 0)),
                  pl.BlockSpec((None, d, tn), lambda l, j: (l, 0, j)),
                  pl.BlockSpec((None, 1, tn), lambda l, j: (l, 0, j))],
        out_specs=pl.BlockSpec((None, 8, tn), lambda l, j: (l, 0, j)),
        out_shape=jax.ShapeDtypeStruct((depth, 8, n6), F32),
        compiler_params=_cparams(("parallel", "parallel")),
        name="adaln",
    )(c8, ada_w, ada_b.reshape(depth, 1, n6))


def _mod_rows(mod_ref, i, n_ctx_tiles, idx, d):
    row = jnp.where(i >= n_ctx_tiles, 1, 0)
    return mod_ref[pl.ds(row, 1), idx * d:(idx + 1) * d]


def _modulate_kernel(x_ref, mod_ref, o_ref, *, n_ctx_tiles, d):
    i = pl.program_id(0)
    sh = _mod_rows(mod_ref, i, n_ctx_tiles, 0, d)
    sc = _mod_rows(mod_ref, i, n_ctx_tiles, 1, d)
    o_ref[...] = (x_ref[...] * (1.0 + sc) + sh).astype(o_ref.dtype)


def modulate(x, mod, n_ctx):
    n, d = x.shape
    return pl.pallas_call(
        functools.partial(_modulate_kernel, n_ctx_tiles=(n - n_ctx) // ROW_TILE, d=d),
        grid=(n // ROW_TILE,),
        in_specs=[pl.BlockSpec((ROW_TILE, d), lambda i: (i, 0)),
                  pl.BlockSpec(mod.shape, lambda i: (0, 0))],
        out_specs=pl.BlockSpec((ROW_TILE, d), lambda i: (i, 0)),
        out_shape=jax.ShapeDtypeStruct((n, d), BF16),
        compiler_params=_cparams(("parallel",)),
        name="modulate",
    )(x, mod)


def _mm_kernel(x_ref, w_ref, o_ref):
    o_ref[...] = jnp.dot(x_ref[...], w_ref[...], preferred_element_type=F32).astype(o_ref.dtype)


def matmul(x, w, out_dtype, tn=512):
    m, k = x.shape
    n = w.shape[1]
    tm = _row_tile(m)
    return pl.pallas_call(
        _mm_kernel,
        grid=(m // tm, n // tn),
        in_specs=[pl.BlockSpec((tm, k), lambda i, j: (i, 0)),
                  pl.BlockSpec((k, tn), lambda i, j: (0, j))],
        out_specs=pl.BlockSpec((tm, tn), lambda i, j: (i, j)),
        out_shape=jax.ShapeDtypeStruct((m, n), out_dtype),
        compiler_params=_cparams(("parallel", "parallel")),
        name="matmul",
    )(x, w)


def _mm_rope_kernel(x_ref, w_ref, cos_ref, sin_ref, o_ref, *, tn):
    acc = jnp.dot(x_ref[...], w_ref[...], preferred_element_type=F32)
    cos = cos_ref[...]
    sin = sin_ref[...]
    lane = lax.broadcasted_iota(jnp.int32, cos.shape, 1)
    first = ((lane // 16) % 2) == 0
    for j in range(tn // LANE):
        blk = acc[:, j * LANE:(j + 1) * LANE]
        partner = jnp.where(first, pltpu.roll(blk, LANE - 16, 1), pltpu.roll(blk, 16, 1))
        o_ref[:, j * LANE:(j + 1) * LANE] = (blk * cos + partner * sin).astype(o_ref.dtype)


def matmul_rope(x, w, cos_t, sin_t, tn=512):
    m, k = x.shape
    n = w.shape[1]
    tm = _row_tile(m)
    return pl.pallas_call(
        functools.partial(_mm_rope_kernel, tn=tn),
        grid=(m // tm, n // tn),
        in_specs=[pl.BlockSpec((tm, k), lambda i, j: (i, 0)),
                  pl.BlockSpec((k, tn), lambda i, j: (0, j)),
                  pl.BlockSpec((tm, LANE), lambda i, j: (i, 0)),
                  pl.BlockSpec((tm, LANE), lambda i, j: (i, 0))],
        out_specs=pl.BlockSpec((tm, tn), lambda i, j: (i, j)),
        out_shape=jax.ShapeDtypeStruct((m, n), BF16),
        compiler_params=_cparams(("parallel", "parallel")),
        name="matmul_rope",
    )(x, w, cos_t, sin_t)


def rope_tables(n_ctx, n_lat):
    t = jnp.arange(n_lat)
    row = (t // GRID_W).astype(F32)
    col = (t % GRID_W).astype(F32)
    half = A_QK_DIM // 2
    inv = ROPE_BASE ** (-jnp.arange(0, half, 2, dtype=F32) / half)
    ang_r = row[:, None] * inv
    ang_c = col[:, None] * inv
    cos64 = jnp.concatenate([jnp.cos(ang_r), jnp.cos(ang_r), jnp.cos(ang_c), jnp.cos(ang_c)], axis=-1)
    sin64 = jnp.concatenate([-jnp.sin(ang_r), jnp.sin(ang_r), -jnp.sin(ang_c), jnp.sin(ang_c)], axis=-1)
    cos_l = jnp.concatenate([cos64, cos64], axis=-1)
    sin_l = jnp.concatenate([sin64, sin64], axis=-1)
    cos_t = jnp.concatenate([cos_l, jnp.ones((n_ctx, LANE), F32)], axis=0)
    sin_t = jnp.concatenate([sin_l, jnp.zeros((n_ctx, LANE), F32)], axis=0)
    return cos_t, sin_t


ATTN_ROW_SPLITS = 2


def _attn_kernel(lam_ref, q_ref, k_ref, v_ref, w_ref, o_ref, qq_s, m_s, acc_s, *, nk, tk, out_scale):
    ki = pl.program_id(2)

    @pl.when(ki == 0)
    def _():
        q = q_ref[...]
        lane = lax.broadcasted_iota(jnp.int32, q.shape, 1)
        zero = jnp.zeros_like(q)
        qq_s[0] = jnp.where(lane < A_QK_DIM, q, zero)
        qq_s[1] = jnp.where(lane >= A_QK_DIM, q, zero)
        m_s[...] = jnp.full(m_s.shape, -jnp.inf, F32)
        acc_s[...] = jnp.zeros(acc_s.shape, F32)

    k = k_ref[...]
    v = v_ref[...]
    v_ext = jnp.concatenate([v, jnp.ones_like(v)], axis=1)
    n_col = tk // LANE
    part = qq_s.shape[1] // ATTN_ROW_SPLITS
    chains = [(mi, slice(rp * part, (rp + 1) * part)) for mi in range(2) for rp in range(ATTN_ROW_SPLITS)]

    def scores(c):
        mi, rows = chains[c]
        return lax.dot_general(qq_s[mi, rows, :], k, (((1,), (1,)), ((), ())),
                               preferred_element_type=F32).astype(BF16)

    def softmax(c, s):
        mi, rows = chains[c]
        cm = s[:, 0:LANE]
        for cc in range(1, n_col):
            cm = jnp.maximum(cm, s[:, cc * LANE:(cc + 1) * LANE])
        m_old = m_s[mi, rows, :]
        m_new = jnp.maximum(m_old, jnp.max(cm.astype(F32), axis=1, keepdims=True))
        m_s[mi, rows, :] = m_new
        mb = m_new.astype(BF16)
        return jnp.exp2(s - jnp.concatenate([mb] * n_col, axis=1)), jnp.exp2(m_old - m_new)

    def accumulate(c, p, alpha):
        mi, rows = chains[c]
        acc_s[mi, rows, :] = (jnp.concatenate([alpha, alpha], axis=1) * acc_s[mi, rows, :]
                              + jnp.dot(p, v_ext, preferred_element_type=F32))

    n_ch = len(chains)
    s, pa = {}, {}
    for step in range(n_ch + 2):
        if step < n_ch:
            s[step] = scores(step)
        if 1 <= step <= n_ch:
            pa[step - 1] = softmax(step - 1, s.pop(step - 1))
        if 2 <= step:
            accumulate(step - 2, *pa.pop(step - 2))

    @pl.when(ki == nk - 1)
    def _():
        lam = lam_ref[0]
        o0 = acc_s[0, :, :LANE] / acc_s[0, :, LANE:LANE + 1]
        o1 = acc_s[1, :, :LANE] / acc_s[1, :, LANE:LANE + 1]
        o = o0 - lam * o1
        ms = jnp.mean(o * o, axis=-1, keepdims=True)
        o_ref[...] = (o * lax.rsqrt(ms + 1e-5) * w_ref[...] * out_scale).astype(o_ref.dtype)


def diff_attention(qk, v, lam, subln_w, lam_init, tq, tk, nq_tiles, q_tile0, nk, k_tile0):
    return pl.pallas_call(
        functools.partial(_attn_kernel, nk=nk, tk=tk, out_scale=1.0 - lam_init),
        grid=(A_HEADS, nq_tiles, nk),
        in_specs=[pl.BlockSpec(memory_space=pltpu.SMEM),
                  pl.BlockSpec((tq, LANE), lambda h, i, j: (q_tile0 + i, h)),
                  pl.BlockSpec((tk, LANE), lambda h, i, j: (k_tile0 + j, A_HEADS + h)),
                  pl.BlockSpec((tk, LANE), lambda h, i, j: (k_tile0 + j, h)),
                  pl.BlockSpec((1, LANE), lambda h, i, j: (0, 0))],
        out_specs=pl.BlockSpec((tq, LANE), lambda h, i, j: (i, h)),
        out_shape=jax.ShapeDtypeStruct((nq_tiles * tq, A_HEADS * A_V_DIM), BF16),
        scratch_shapes=[pltpu.VMEM((2, tq, LANE), BF16), pltpu.VMEM((2, tq, LANE), F32),
                        pltpu.VMEM((2, tq, 2 * LANE), F32)],
        compiler_params=_cparams(("parallel", "parallel", "arbitrary")),
        name="diff_attention",
    )(lam.reshape(1), qk, qk, v, subln_w.reshape(1, LANE))


def _rwkv_prep_kernel(z_ref, zp_ref, zn_ref, mu_ref, w2_ref, w0_ref, a2_ref, a0_ref, g2_ref, kk_ref, ka_ref,
                      eh_ref, r_o, v_o, kkn_o, g_o, kd_o, lw_o, bd_o, *, n_first, n_tot):
    i = pl.program_id(0)
    z = z_ref[...]
    tm = z.shape[0]
    grow = i * tm + lax.broadcasted_iota(jnp.int32, (tm, 1), 0)
    lrow = lax.broadcasted_iota(jnp.int32, (tm, 1), 0)
    prev = jnp.where(lrow == 0, zp_ref[7:8, :], pltpu.roll(z, 1, 0))
    nxt = jnp.where(lrow == tm - 1, zn_ref[0:1, :], pltpu.roll(z, tm - 1, 0))
    has_prev = jnp.logical_and(grow != 0, grow != n_first)
    has_next = jnp.logical_and(grow != n_first - 1, grow != n_tot - 1)
    prev = jnp.where(has_prev, prev, 0.0)
    nxt = jnp.where(has_next, nxt, 0.0)
    zm = z + (0.5 * (prev + nxt) - z) * mu_ref[...]
    c = B_WIDTH
    r = zm[:, 0:c]
    k = zm[:, c:2 * c]
    v = zm[:, 2 * c:3 * c]
    wd = zm[:, 3 * c:3 * c + 128]
    ad = zm[:, 3 * c + 128:3 * c + 256]
    gd = zm[:, 3 * c + 256:3 * c + 512]
    wl = w0_ref[...] + jnp.dot(jnp.tanh(wd), w2_ref[...], precision=HIGHEST, preferred_element_type=F32)
    al = a0_ref[...] + jnp.dot(ad, a2_ref[...], precision=HIGHEST, preferred_element_type=F32)
    g = _mm(jax.nn.sigmoid(gd), g2_ref[...])
    kk = k * kk_ref[...]
    ss = _mm_exact_rhs(kk * kk, eh_ref[...])
    kkn = kk / jnp.maximum(jnp.sqrt(ss), 1e-12)
    r_o[...] = r
    v_o[...] = v
    kkn_o[...] = kkn
    g_o[...] = g
    for d in range(2):
        w_log = -_softplus(-wl[:, d * c:(d + 1) * c]) - 0.5
        lw_o[d] = -jnp.exp(w_log)
        icl = jax.nn.sigmoid(al[:, d * c:(d + 1) * c])
        kd_o[d] = k * (1.0 + (icl - 1.0) * ka_ref[...])
        bd_o[d] = kkn * icl


def rwkv_prep(zb, mu, w2cat, w0cat, a2cat, a0cat, g2pad, k_k, k_a, ehead, n_ctx):
    n, wz = zb.shape
    tm = ROW_TILE
    nt = n // tm
    c = B_WIDTH
    full = lambda a: pl.BlockSpec(a.shape, lambda i: (0,) * a.ndim)
    tok = pl.BlockSpec((tm, c), lambda i: (i, 0))
    tok2 = pl.BlockSpec((2, tm, c), lambda i: (0, i, 0))
    sd = jax.ShapeDtypeStruct
    return pl.pallas_call(
        functools.partial(_rwkv_prep_kernel, n_first=n - n_ctx, n_tot=n),
        grid=(nt,),
        in_specs=[pl.BlockSpec((tm, wz), lambda i: (i, 0)),
                  pl.BlockSpec((8, wz), lambda i: (jnp.maximum(i * (tm // 8) - 1, 0), 0)),
                  pl.BlockSpec((8, wz), lambda i: (jnp.minimum((i + 1) * (tm // 8), n // 8 - 1), 0)),
                  full(mu), full(w2cat), full(w0cat), full(a2cat), full(a0cat), full(g2pad),
                  full(k_k), full(k_a), full(ehead)],
        out_specs=[tok, tok, tok, tok, tok2, tok2, tok2],
        out_shape=[sd((n, c), F32)] * 4 + [sd((2, n, c), F32)] * 3,
        compiler_params=_cparams(("parallel",)),
        name="rwkv_prep",
    )(zb, zb, zb, mu, w2cat, w0cat, a2cat, a0cat, g2pad, k_k, k_a, ehead)


def _rwkv_chunk_kernel(r_ref, v_ref, kk_ref, kd_ref, lw_ref, bd_ref, mincl_ref, mbef_ref, msame_ref,
                       gh_ref, qy_ref):
    lw = lw_ref[...]
    mincl = mincl_ref[...]
    cum_i = _mm_exact_lhs(mincl, lw)
    cum_e = cum_i - lw
    cum_l = _mm_exact_lhs(msame_ref[...], lw)
    r = r_ref[...]
    v = v_ref[...]
    a = -kk_ref[...]
    k = kd_ref[...]
    b = bd_ref[...]
    at = a * jnp.exp(cum_e)
    rt = r * jnp.exp(cum_i)
    einv = jnp.exp(-cum_i)
    bt = b * einv
    kt = k * einv
    e_l = jnp.exp(cum_l - cum_i)
    bh = b * e_l
    kh = k * e_l
    p_l = jnp.exp(cum_l)
    incl = mincl > 0.5
    bef = mbef_ref[...] > 0.5
    hd = B_HEAD_DIM
    nch = ROW_TILE // RWKV_CHUNK
    zeros_h = jnp.zeros((ROW_TILE, hd), F32)
    eye = (lax.broadcasted_iota(jnp.int32, (RWKV_CHUNK, LANE), 0)
           == lax.broadcasted_iota(jnp.int32, (RWKV_CHUNK, LANE), 1))
    ri = lax.broadcasted_iota(jnp.int32, (ROW_TILE, ROW_TILE), 0)
    ci = lax.broadcasted_iota(jnp.int32, (ROW_TILE, ROW_TILE), 1)
    xr = ri ^ ci
    eye_t = jnp.where(ri == ci, 1.0, 0.0)
    heads = range(RWKV_STEP_HEADS)
    sls = [slice(h * hd, (h + 1) * hd) for h in heads]
    a_ab = [jnp.where(bef, _mm_nt(at[:, sl], bt[:, sl]), 0.0) for sl in sls]
    a_ak = [jnp.where(bef, _mm_nt(at[:, sl], kt[:, sl]), 0.0) for sl in sls]
    m_rb = [jnp.where(incl, _mm_nt(rt[:, sl], bt[:, sl]), 0.0) for sl in sls]
    m_rk = [jnp.where(incl, _mm_nt(rt[:, sl], kt[:, sl]), 0.0) for sl in sls]
    lev0 = xr == 1
    t_inv = [eye_t + jnp.where(lev0, a_ab[h], 0.0) for h in heads]
    for sh in range(1, 6):
        lev = (xr >> sh) == 1
        inner = [_mm(jnp.where(lev, a_ab[h], 0.0), t_inv[h]) for h in heads]
        t_inv = [t_inv[h] + _mm(t_inv[h], inner[h]) for h in heads]
    x0 = [jnp.concatenate([at[:, sls[h]], _mm(a_ak[h], v[:, sls[h]])], axis=1) for h in heads]
    x = [_mm(t_inv[h], x0[h]) for h in heads]
    v0 = [jnp.concatenate([zeros_h, v[:, sls[h]]], axis=1) for h in heads]
    for h in heads:
        qy_ref[h] = (jnp.concatenate([rt[:, sls[h]], zeros_h], axis=1) + _mm(m_rb[h], x[h])
                     + _mm(m_rk[h], v0[h]))
    for c in range(nch):
        rows = slice(c * RWKV_CHUNK, (c + 1) * RWKV_CHUNK)
        for h in heads:
            sl = sls[h]
            gh = _mm_tn(bh[rows, sl], x[h][rows]) + _mm_tn(kh[rows, sl], v0[h][rows])
            plr = jnp.concatenate([p_l[c * RWKV_CHUNK:c * RWKV_CHUNK + 1, sl], jnp.zeros((1, hd), F32)], axis=1)
            gh_ref[h, rows, :] = gh + jnp.where(eye, jnp.broadcast_to(plr, (RWKV_CHUNK, LANE)), 0.0)


def rwkv_chunk(r, v, kkn, kd, lw, bd, mincl, mbef, msame):
    n, c = r.shape
    tm = ROW_TILE
    wide = RWKV_STEP_HEADS * B_HEAD_DIM
    tokb = pl.BlockSpec((tm, wide), lambda d, s, p: (s, p))
    tokd = pl.BlockSpec((None, tm, wide), lambda d, s, p: (d, s, p))
    mdir = pl.BlockSpec((None, tm, tm), lambda d, s, p: (d, 0, 0))
    outb = pl.BlockSpec((None, RWKV_STEP_HEADS, tm, LANE), lambda d, s, p: (d, p, s, 0))
    sd = jax.ShapeDtypeStruct((2, B_HEADS, n, LANE), F32)
    return pl.pallas_call(
        _rwkv_chunk_kernel,
        grid=(2, n // tm, B_HEADS // RWKV_STEP_HEADS),
        in_specs=[tokb, tokb, tokb, tokd, tokd, tokd, mdir, mdir,
                  pl.BlockSpec((tm, tm), lambda d, s, p: (0, 0))],
        out_specs=[outb, outb],
        out_shape=[sd, sd],
        compiler_params=_cparams(("parallel", "parallel", "parallel")),
        name="rwkv_chunk",
    )(r, v, kkn, kd, lw, bd, mincl, mbef, msame)


def _scan_tile(d, j, nt, n_ctx_tiles):
    n_lat_tiles = nt - n_ctx_tiles
    fwd = jnp.where(j < n_ctx_tiles, n_lat_tiles + j, j - n_ctx_tiles)
    return jnp.where(d == 0, fwd, nt - 1 - j)


def _rwkv_seq_kernel(gh_ref, qy_ref, y_ref, z_s):
    d = pl.program_id(0)
    j = pl.program_id(1)

    @pl.when(j == 0)
    def _():
        z_s[...] = jnp.zeros(z_s.shape, F32)

    hd = B_HEAD_DIM
    nch = ROW_TILE // RWKV_CHUNK
    for ci in range(nch):
        c = jnp.where(d == 0, ci, nch - 1 - ci)
        off = pl.multiple_of(c * RWKV_CHUNK, RWKV_CHUNK)
        ys = []
        for h in range(B_HEADS):
            z = z_s[h]
            gh = gh_ref[h, pl.ds(off, RWKV_CHUNK), :]
            qy = qy_ref[h, pl.ds(off, RWKV_CHUNK), :]
            ys.append(_mm(qy[:, :hd], z) + qy[:, hd:])
            z_s[h] = _mm(gh[:, :hd], z) + gh[:, hd:]
        y_ref[pl.ds(off, RWKV_CHUNK), :] = jnp.concatenate(ys, axis=1)


def rwkv_seq(gh, qy, n_ctx):
    _, hh, n, _ = gh.shape
    tm = ROW_TILE
    nt = n // tm
    nct = n_ctx // tm
    inb = pl.BlockSpec((None, hh, tm, LANE), lambda d, j: (d, 0, _scan_tile(d, j, nt, nct), 0))
    return pl.pallas_call(
        _rwkv_seq_kernel,
        grid=(2, nt),
        in_specs=[inb, inb],
        out_specs=pl.BlockSpec((None, tm, B_WIDTH), lambda d, j: (d, _scan_tile(d, j, nt, nct), 0)),
        out_shape=jax.ShapeDtypeStruct((2, n, B_WIDTH), F32),
        scratch_shapes=[pltpu.VMEM((hh, B_HEAD_DIM, B_HEAD_DIM), F32)],
        compiler_params=_cparams(("parallel", "arbitrary")),
        name="rwkv_seq",
    )(gh, qy)


def _rwkv_finish_kernel(y_ref, r_ref, v_ref, kd_ref, g_ref, rk_ref, lw_ref, lb_ref, eh_ref, o_ref):
    y = y_ref[0] + y_ref[1]
    eh = eh_ref[...]
    inv = 1.0 / B_HEAD_DIM
    mu = _mm_exact_rhs(y, eh) * inv
    yc = y - mu
    var = _mm_exact_rhs(yc * yc, eh) * inv
    yn = yc * lax.rsqrt(var + RWKV_LN_EPS) * lw_ref[...] + lb_ref[...]
    r = r_ref[...]
    rk = (r * kd_ref[0] + r * kd_ref[1]) * rk_ref[...]
    bonus = _mm_exact_rhs(rk, eh) * v_ref[...]
    o_ref[...] = ((yn + bonus) * g_ref[...]).astype(o_ref.dtype)


def rwkv_finish(y, r, v, kd, g, r_k, lnx_w, lnx_b, ehead):
    n, c = r.shape
    tm = ROW_TILE
    tok = pl.BlockSpec((tm, c), lambda i: (i, 0))
    tok2 = pl.BlockSpec((2, tm, c), lambda i: (0, i, 0))
    full = lambda a: pl.BlockSpec(a.shape, lambda i: (0,) * a.ndim)
    return pl.pallas_call(
        _rwkv_finish_kernel,
        grid=(n // tm,),
        in_specs=[tok2, tok, tok, tok2, tok, full(r_k), full(lnx_w), full(lnx_b), full(ehead)],
        out_specs=tok,
        out_shape=jax.ShapeDtypeStruct((n, c), BF16),
        compiler_params=_cparams(("parallel",)),
        name="rwkv_finish",
    )(y, r, v, kd, g, r_k, lnx_w, lnx_b, ehead)


def _layer_norm(x, g, b):
    mu = jnp.mean(x, axis=-1, keepdims=True)
    xc = x - mu
    var = jnp.mean(xc * xc, axis=-1, keepdims=True)
    return xc * lax.rsqrt(var + 1e-5) * g + b


def _store_token_rows(ref, val):
    tm = val.shape[0]
    for s in range(TOK_ROWS):
        ref[pl.ds(s, tm, stride=TOK_ROWS), :] = val[:, s * LANE:(s + 1) * LANE]


def _load_token_rows(ref, tm):
    return jnp.concatenate([ref[pl.ds(s, tm, stride=TOK_ROWS), :] for s in range(TOK_ROWS)], axis=1)


def _outproj_kernel(a_ref, w_ref, x_ref, mod_ref, g_ref, b_ref, x1_ref, f_ref, f3_ref, *, n_ctx_tiles, d):
    i = pl.program_id(0)
    m = jnp.dot(a_ref[...], w_ref[...], preferred_element_type=F32)
    gate = _mod_rows(mod_ref, i, n_ctx_tiles, 2, d)
    x1 = _layer_norm(DEEPNORM_ALPHA * x_ref[...] + gate * m, g_ref[...], b_ref[...])
    x1_ref[...] = x1
    sh = _mod_rows(mod_ref, i, n_ctx_tiles, 3, d)
    sc = _mod_rows(mod_ref, i, n_ctx_tiles, 4, d)
    f = x1 * (1.0 + sc) + sh
    f_ref[...] = f
    _store_token_rows(f3_ref, f)


def outproj_ln(a, w_out, x, mod, ln_g, ln_b, n_ctx):
    n, kdim = a.shape
    d = x.shape[1]
    tm = ROW_TILE
    tok = pl.BlockSpec((tm, d), lambda i: (i, 0))
    vec = pl.BlockSpec((1, d), lambda i: (0, 0))
    return pl.pallas_call(
        functools.partial(_outproj_kernel, n_ctx_tiles=(n - n_ctx) // tm, d=d),
        grid=(n // tm,),
        in_specs=[pl.BlockSpec((tm, kdim), lambda i: (i, 0)),
                  pl.BlockSpec((kdim, d), lambda i: (0, 0)),
                  tok, pl.BlockSpec(mod.shape, lambda i: (0, 0)), vec, vec],
        out_specs=[tok, tok, pl.BlockSpec((tm * TOK_ROWS, LANE), lambda i: (i, 0))],
        out_shape=[jax.ShapeDtypeStruct((n, d), F32)] * 2 + [jax.ShapeDtypeStruct((n * TOK_ROWS, LANE), F32)],
        compiler_params=_cparams(("parallel",)),
        name="outproj_ln",
    )(a, w_out, x, mod, ln_g.reshape(1, d), ln_b.reshape(1, d))


PICK_IDX, PICK_RANK, PICK_GATE = 0, 8, 16


def _router_kernel(f_ref, wr_hi_ref, wr_lo_ref, br_ref, wsg_ref, wsd_ref, ltri_ref, pick_ref, sh_ref, cnt_ref,
                   carry_s):
    i = pl.program_id(0)

    @pl.when(i == 0)
    def _():
        carry_s[...] = jnp.zeros(carry_s.shape, F32)

    f = f_ref[...]
    f_hi = f.astype(BF16)
    f_lo = (f - f_hi.astype(F32)).astype(BF16)
    dd = lambda a, b: jnp.dot(a, b[...], preferred_element_type=F32)
    logits = dd(f_hi, wr_hi_ref) + dd(f_lo, wr_hi_ref) + dd(f_hi, wr_lo_ref)
    scores = jax.nn.sigmoid(logits)
    lane = lax.broadcasted_iota(jnp.int32, scores.shape, 1).astype(F32)
    neg = jnp.float32(-jnp.inf)
    sel = jnp.where(lane < N_EXPERTS, scores + br_ref[...], neg)
    top = jnp.zeros(scores.shape, F32)
    chosen = jnp.zeros(scores.shape, F32)
    hits, firsts = [], []
    for j in range(TOP_K):
        mx = jnp.max(sel, axis=1, keepdims=True)
        first = jnp.min(jnp.where(sel == mx, lane, float(LANE)), axis=1, keepdims=True)
        hit = lane == first
        top = jnp.where(hit, scores, top)
        chosen = jnp.where(hit, 1.0, chosen)
        sel = jnp.where(hit, neg, sel)
        hits.append(hit)
        firsts.append(first)
    gates = top / jnp.sum(top, axis=1, keepdims=True) * ROUTE_SCALE
    carry = carry_s[0:1, :]
    rank = dd(ltri_ref[...], chosen.astype(BF16)) + carry
    pick = jnp.zeros(scores.shape, F32)
    for j in range(TOP_K):
        pick = jnp.where(lane == PICK_IDX + j, firsts[j], pick)
        rank_j = jnp.sum(jnp.where(hits[j], rank, 0.0), axis=1, keepdims=True)
        pick = jnp.where(lane == PICK_RANK + j, rank_j, pick)
        gate_j = jnp.sum(jnp.where(hits[j], gates, 0.0), axis=1, keepdims=True)
        pick = jnp.where(lane == PICK_GATE + j, gate_j, pick)
    pick_ref[...] = pick
    carry_s[0:1, :] = carry + jnp.sum(chosen, axis=0, keepdims=True)
    cnt_ref[...] = carry_s[...]
    ab = dd(f_hi, wsg_ref)
    hdn = _silu(ab[:, :SHARED_FF]) * ab[:, SHARED_FF:]
    sh_ref[...] = dd(hdn.astype(BF16), wsd_ref)


def router_shared(f_in, wr_hi, wr_lo, br, ws_gu, ws_down):
    n, d = f_in.shape
    tm = ROW_TILE
    full = lambda a: pl.BlockSpec(a.shape, lambda i: (0,) * a.ndim)
    sd = jax.ShapeDtypeStruct
    ltri = jnp.asarray(np.tril(np.ones((tm, tm), np.float32), -1), BF16)
    return pl.pallas_call(
        _router_kernel,
        grid=(n // tm,),
        in_specs=[pl.BlockSpec((tm, d), lambda i: (i, 0)), full(wr_hi), full(wr_lo), full(br),
                  full(ws_gu), full(ws_down), full(ltri)],
        out_specs=[pl.BlockSpec((tm, LANE), lambda i: (i, 0)), pl.BlockSpec((tm, d), lambda i: (i, 0)),
                   pl.BlockSpec((8, LANE), lambda i: (0, 0))],
        out_shape=[sd((n, LANE), F32), sd((n, d), F32), sd((8, LANE), F32)],
        scratch_shapes=[pltpu.VMEM((8, LANE), F32)],
        compiler_params=_cparams(("arbitrary",)),
        name="router_shared",
    )(f_in, wr_hi, wr_lo, br, ws_gu, ws_down, ltri)


def _slab(ref, row):
    return ref.at[pl.ds(pl.multiple_of(row * TOK_ROWS, TOK_ROWS), TOK_ROWS)]


def _dispatch_kernel(slots_ref, padlo_ref, padhi_ref, f3_ref, xs_hbm, zrow, sem, *, tm, n_tiles):
    i = pl.program_id(0)

    def body(t, carry):
        for j in range(TOP_K):
            slot = slots_ref[(i * tm + t) * TOP_K + j]
            pltpu.make_async_copy(_slab(f3_ref, t), _slab(xs_hbm, slot), sem.at[0]).start()
        return carry
    lax.fori_loop(0, tm, body, 0, unroll=DMA_ISSUE_UNROLL)
    for _ in range(TOP_K):
        pltpu.make_async_copy(f3_ref, xs_hbm.at[pl.ds(0, tm * TOK_ROWS)], sem.at[0]).wait()

    @pl.when(i == n_tiles - 1)
    def _():
        zrow[...] = jnp.zeros(zrow.shape, F32)

        def per_expert(e, carry):
            lo = padlo_ref[e]
            hi = padhi_ref[e]

            def fill(s, c):
                pltpu.make_async_copy(zrow, _slab(xs_hbm, s), sem.at[1]).start()
                return c
            lax.fori_loop(lo, hi, fill, 0)

            def drain(s, c):
                pltpu.make_async_copy(zrow, _slab(xs_hbm, 0), sem.at[1]).wait()
                return c
            lax.fori_loop(lo, hi, drain, 0)
            return carry
        lax.fori_loop(0, N_EXPERTS + 1, per_expert, 0)


def moe_dispatch_rows(f3, tok_slots, pad_lo, pad_hi, n_slots):
    tm = ROW_TILE
    n = f3.shape[0] // TOK_ROWS
    grid_spec = pltpu.PrefetchScalarGridSpec(
        num_scalar_prefetch=3,
        grid=(n // tm,),
        in_specs=[pl.BlockSpec((tm * TOK_ROWS, LANE), lambda i, sl, lo, hi: (i, 0))],
        out_specs=pl.BlockSpec(memory_space=pl.ANY),
        scratch_shapes=[pltpu.VMEM((TOK_ROWS, LANE), F32), pltpu.SemaphoreType.DMA((2,))],
    )
    return pl.pallas_call(
        functools.partial(_dispatch_kernel, tm=tm, n_tiles=n // tm),
        grid_spec=grid_spec,
        out_shape=jax.ShapeDtypeStruct((n_slots * TOK_ROWS, LANE), F32),
        compiler_params=_cparams(("arbitrary",)),
        name="moe_dispatch",
    )(tok_slots, pad_lo, pad_hi, f3)


def _expert_kernel(be_ref, nu_ref, xs_ref, wgu_ref, wdn_ref, ys_ref, wgu_bf, wdn_bf):
    i = pl.program_id(0)

    @pl.when(jnp.logical_or(i == 0, be_ref[i] != be_ref[jnp.maximum(i - 1, 0)]))
    def _():
        wgu_bf[...] = wgu_ref[...].astype(BF16)
        wdn_bf[...] = wdn_ref[...].astype(BF16)

    @pl.when(i < nu_ref[0])
    def _():
        x = _load_token_rows(xs_ref, EXPERT_BLOCK).astype(BF16)
        ab = jnp.dot(x, wgu_bf[...], preferred_element_type=F32)
        hdn = _silu(ab[:, :EXPERT_FF]) * ab[:, EXPERT_FF:]
        y = jnp.dot(hdn.astype(BF16), wdn_bf[...], preferred_element_type=F32)
        _store_token_rows(ys_ref, y)

    @pl.when(i >= nu_ref[0])
    def _():
        ys_ref[...] = jnp.zeros(ys_ref.shape, F32)


def routed_experts(xs, block_expert, n_used, w_gu_all, w_down_all, layer):
    d = w_gu_all.shape[2]
    n_blocks = block_expert.shape[0]
    rows = EXPERT_BLOCK * TOK_ROWS
    last_used = lambda i, nu: jnp.minimum(i, jnp.maximum(nu[0] - 1, 0))
    grid_spec = pltpu.PrefetchScalarGridSpec(
        num_scalar_prefetch=2,
        grid=(n_blocks,),
        in_specs=[pl.BlockSpec((rows, LANE), lambda i, be, nu: (last_used(i, nu), 0)),
                  pl.BlockSpec((None, None, d, 2 * EXPERT_FF), lambda i, be, nu: (layer, be[i], 0, 0)),
                  pl.BlockSpec((None, None, EXPERT_FF, d), lambda i, be, nu: (layer, be[i], 0, 0))],
        out_specs=pl.BlockSpec((rows, LANE), lambda i, be, nu: (i, 0)),
        scratch_shapes=[pltpu.VMEM((d, 2 * EXPERT_FF), BF16), pltpu.VMEM((EXPERT_FF, d), BF16)],
    )
    return pl.pallas_call(
        _expert_kernel,
        grid_spec=grid_spec,
        out_shape=jax.ShapeDtypeStruct((n_blocks * rows, LANE), F32),
        compiler_params=_cparams(("arbitrary",)),
        name="routed_experts",
    )(block_expert, n_used, xs, w_gu_all, w_down_all)


def moe_slots(pick, counts, n_blocks):
    eb = EXPERT_BLOCK
    idx6 = pick[:, PICK_IDX:PICK_IDX + TOP_K].astype(jnp.int32)
    rank6 = pick[:, PICK_RANK:PICK_RANK + TOP_K].astype(jnp.int32)
    padded = (counts + eb - 1) // eb * eb
    pad_end = jnp.cumsum(padded)
    pad_start = pad_end - padded
    experts = jnp.arange(N_EXPERTS, dtype=jnp.int32)
    start6 = jnp.sum(jnp.where(idx6[:, :, None] == experts, pad_start, 0), axis=-1)
    tok_slots = (start6 + rank6).reshape(-1).astype(jnp.int32)
    blk_start = jnp.arange(n_blocks, dtype=jnp.int32) * eb
    block_expert = jnp.minimum(jnp.sum(pad_end[None, :] <= blk_start[:, None], axis=1), N_EXPERTS - 1)
    block_expert = block_expert.astype(jnp.int32)
    n_used = (pad_end[-1:] // eb).astype(jnp.int32)
    fill_lo = jnp.concatenate([pad_start + counts, pad_end[-1:]]).astype(jnp.int32)
    fill_hi = jnp.concatenate([pad_end, jnp.full((1,), n_blocks * eb, pad_end.dtype)]).astype(jnp.int32)
    return tok_slots, block_expert, n_used, fill_lo, fill_hi


def _combine_kernel(slots_ref, ys_hbm, pick_ref, sh_ref, x1_ref, mod_ref, g_ref, b_ref, *rest,
                    tm, n_tiles, n_ctx_tiles, d, with_next):
    if with_next:
        modn_ref, x2_ref, h_ref, buf, sem = rest
    else:
        x2_ref, buf, sem = rest
    i = pl.program_id(0)
    s = i % 2
    tok_rows = TOP_K * TOK_ROWS

    def start_gather(tile, slot):
        def body(t, carry):
            for j in range(TOP_K):
                src = slots_ref[(tile * tm + t) * TOP_K + j]
                pltpu.make_async_copy(_slab(ys_hbm, src), _slab(buf.at[slot], t * TOP_K + j), sem.at[slot]).start()
            return carry
        lax.fori_loop(0, tm, body, 0, unroll=DMA_ISSUE_UNROLL)

    @pl.when(i == 0)
    def _():
        start_gather(0, 0)

    @pl.when(i + 1 < n_tiles)
    def _():
        start_gather(i + 1, 1 - s)

    pltpu.make_async_copy(ys_hbm.at[pl.ds(0, tm * tok_rows)], buf.at[s], sem.at[s]).wait()
    pieces = []
    for c in range(TOK_ROWS):
        acc = sh_ref[:, c * LANE:(c + 1) * LANE]
        for j in range(TOP_K):
            gate_j = pick_ref[:, PICK_GATE + j:PICK_GATE + j + 1]
            acc = acc + gate_j * buf[s, pl.ds(j * TOK_ROWS + c, tm, stride=tok_rows), :]
        pieces.append(acc)
    f = jnp.concatenate(pieces, axis=1)
    gate = _mod_rows(mod_ref, i, n_ctx_tiles, 5, d)
    x2 = _layer_norm(DEEPNORM_ALPHA * x1_ref[...] + gate * f, g_ref[...], b_ref[...])
    x2_ref[...] = x2
    if with_next:
        sh = _mod_rows(modn_ref, i, n_ctx_tiles, 0, d)
        sc = _mod_rows(modn_ref, i, n_ctx_tiles, 1, d)
        h_ref[...] = (x2 * (1.0 + sc) + sh).astype(h_ref.dtype)


def moe_combine(ys, tok_slots, pick, shared, x1, mod, ln_g, ln_b, n_ctx, mod_next=None):
    n, d = x1.shape
    tm = COMBINE_TILE
    tok = pl.BlockSpec((tm, d), lambda i, sl: (i, 0))
    vec = pl.BlockSpec((1, d), lambda i, sl: (0, 0))
    with_next = mod_next is not None
    in_specs = [pl.BlockSpec(memory_space=pl.ANY), pl.BlockSpec((tm, LANE), lambda i, sl: (i, 0)), tok, tok,
                pl.BlockSpec(mod.shape, lambda i, sl: (0, 0)), vec, vec]
    args = [ys, pick, shared, x1, mod, ln_g.reshape(1, d), ln_b.reshape(1, d)]
    out_specs = [tok]
    out_shape = [jax.ShapeDtypeStruct((n, d), F32)]
    if with_next:
        in_specs.append(pl.BlockSpec(mod_next.shape, lambda i, sl: (0, 0)))
        args.append(mod_next)
        out_specs.append(tok)
        out_shape.append(jax.ShapeDtypeStruct((n, d), BF16))
    grid_spec = pltpu.PrefetchScalarGridSpec(
        num_scalar_prefetch=1,
        grid=(n // tm,),
        in_specs=in_specs, out_specs=out_specs,
        scratch_shapes=[pltpu.VMEM((2, tm * TOP_K * TOK_ROWS, LANE), F32), pltpu.SemaphoreType.DMA((2,))],
    )
    return pl.pallas_call(
        functools.partial(_combine_kernel, tm=tm, n_tiles=n // tm, n_ctx_tiles=(n - n_ctx) // tm, d=d,
                          with_next=with_next),
        grid_spec=grid_spec,
        out_shape=out_shape,
        compiler_params=_cparams(("arbitrary",)),
        name="moe_combine",
    )(tok_slots, *args)


def moe_layer(f_in, f3, x1, mod, ln_g, ln_b, w_router, b_router, w_gu_all, w_down_all, layer, ws_gu, ws_down,
              n_ctx, mod_next=None):
    n, d = f_in.shape
    wr = jnp.pad(w_router, ((0, 0), (0, LANE - N_EXPERTS)))
    wr_hi = wr.astype(BF16)
    wr_lo = (wr - wr_hi.astype(F32)).astype(BF16)
    br = jnp.pad(b_router.astype(F32), (0, LANE - N_EXPERTS)).reshape(1, LANE)
    pick, shared, cnt = router_shared(f_in, wr_hi, wr_lo, br, ws_gu.astype(BF16), ws_down.astype(BF16))
    counts = cnt[0, :N_EXPERTS].astype(jnp.int32)
    n_blocks = -(-(n * TOP_K + N_EXPERTS * (EXPERT_BLOCK - 1)) // EXPERT_BLOCK)
    tok_slots, block_expert, n_used, pad_lo, pad_hi = moe_slots(pick, counts, n_blocks)
    xs = moe_dispatch_rows(f3, tok_slots, pad_lo, pad_hi, n_blocks * EXPERT_BLOCK)
    ys = routed_experts(xs, block_expert, n_used, w_gu_all, w_down_all, layer)
    return moe_combine(ys, tok_slots, pick, shared, x1, mod, ln_g, ln_b, n_ctx, mod_next)


HGRN_SMALL_LEVELS = (4, 2)
HGRN_STEP_HEADS = 2


def _hgrn_levels():
    t = ROW_TILE
    levels = []
    b = t // 2
    while b >= 1:
        levels.append(b)
        b //= 2
    return levels


def _hgrn_kernel(q_ref, f_ref, i_ref, lb_ref, mlev_ref, mincl_ref, o_ref, st_s, cum_s):
    d = pl.program_id(0)
    j = pl.program_id(2)

    @pl.when(j == 0)
    def _():
        st_s[...] = jnp.zeros(st_s.shape, F32)

    t = ROW_TILE
    lb = lb_ref[...]
    log_lb = jnp.log(lb)
    log_rest = jnp.log1p(-lb)
    fz = f_ref[...]
    q = _silu(q_ref[...])
    v = i_ref[...]
    log_sig = jnp.minimum(fz, 0.0) - jnp.log1p(jnp.exp(-jnp.abs(fz)))
    u = log_rest + log_sig
    mx = jnp.maximum(log_lb, u)
    logf = mx + jnp.log1p(jnp.exp(-jnp.abs(log_lb - u)))
    key = (1.0 - lb) * jax.nn.sigmoid(-fz)

    cum = _mm_exact_lhs(mincl_ref[...], logf)
    cum_l = jnp.where(d == 0, cum[t - 1:t, :], cum[0:1, :])
    cum_s[...] = cum
    d_small = _mm_exact_lhs(mlev_ref[...], logf)

    wide = HGRN_STEP_HEADS * LANE
    heads = [slice(h * LANE, (h + 1) * LANE) for h in range(HGRN_STEP_HEADS)]
    row = lax.broadcasted_iota(jnp.int32, (t, 1), 0)
    xr = (lax.broadcasted_iota(jnp.int32, (t, t), 0) ^ lax.broadcasted_iota(jnp.int32, (t, t), 1))
    a = [jnp.zeros((t, t), F32) for _ in heads]
    for li, b in enumerate(_hgrn_levels()):
        q_side = ((row // b) % 2) == jnp.where(d == 0, 1, 0)
        if b >= 8:
            piv = jnp.concatenate(
                [jnp.broadcast_to(cum_s[pl.ds(g * 2 * b + b - 1 + d, 1), :], (2 * b, wide))
                 for g in range(t // (2 * b))], axis=0)
            dlev = jnp.where(q_side, cum - piv, piv - cum)
        elif b > 1:
            k = HGRN_SMALL_LEVELS.index(b)
            dlev = d_small[k * t:(k + 1) * t]
        else:
            dlev = jnp.where(q_side, logf, 0.0)
        e = jnp.exp(dlev)
        qt = jnp.where(q_side, q * e, 0.0)
        kt = jnp.where(q_side, 0.0, key * e)
        same_group = (xr >> (b.bit_length())) == 0
        for h, sl in enumerate(heads):
            prod = _mm_nt(qt[:, sl], kt[:, sl])
            a[h] = a[h] + (prod if 2 * b == t else jnp.where(same_group, prod, 0.0))
    q_in = q * jnp.exp(cum)
    k_out = key * jnp.exp(cum_l - cum)
    decay_l = jnp.exp(cum_l)
    qk = q * key
    for h, sl in enumerate(heads):
        st = st_s[h]
        o_ref[:, sl] = (_mm(a[h], v[:, sl]) + _mm_nt(q_in[:, sl], st)
                        + jnp.sum(qk[:, sl], axis=1, keepdims=True) * v[:, sl])
        st_s[h] = st * decay_l[:, sl] + _mm_tn(v[:, sl], k_out[:, sl])


def hgrn_matrices():
    t = ROW_TILE
    ti = np.arange(t)[:, None]
    ii = np.arange(t)[None, :]
    mlev = np.zeros((2, len(HGRN_SMALL_LEVELS), t, t), np.float32)
    for li, b in enumerate(HGRN_SMALL_LEVELS):
        bnd = (ti // (2 * b)) * (2 * b) + b - 1
        upper = (ti // b) % 2 == 1
        mlev[0, li] = np.where(upper, (ii > bnd) & (ii <= ti), (ii > ti) & (ii <= bnd))
        mlev[1, li] = np.where(upper, (ii > bnd) & (ii < ti), (ii >= ti) & (ii <= bnd))
    mincl = np.stack([(ii <= ti), (ii >= ti)]).astype(np.float32)
    return (jnp.asarray(mlev.reshape(2, -1, t), BF16), jnp.asarray(mincl, BF16))


def hgrn_scan(z, lb, mlev, mincl, n_ctx):
    n = z.shape[0]
    t = ROW_TILE
    nt = n // t
    nct = n_ctx // t
    hh = C_HEADS
    sh = HGRN_STEP_HEADS
    wide = sh * LANE
    hb = hh // sh
    srow = lambda d, h, j: _scan_tile(d, j, nt, nct)
    return pl.pallas_call(
        _hgrn_kernel,
        grid=(2, hb, nt),
        in_specs=[pl.BlockSpec((t, wide), lambda d, h, j: (srow(d, h, j), h)),
                  pl.BlockSpec((t, wide), lambda d, h, j: (srow(d, h, j), hb * (1 + d) + h)),
                  pl.BlockSpec((t, wide), lambda d, h, j: (srow(d, h, j), 3 * hb + h)),
                  pl.BlockSpec((None, 1, wide), lambda d, h, j: (h, 0, 0)),
                  pl.BlockSpec((None,) + mlev.shape[1:], lambda d, h, j: (d, 0, 0)),
                  pl.BlockSpec((None, t, t), lambda d, h, j: (d, 0, 0))],
        out_specs=pl.BlockSpec((None, t, wide), lambda d, h, j: (d, srow(d, h, j), h)),
        out_shape=jax.ShapeDtypeStruct((2, n, hh * LANE), F32),
        scratch_shapes=[pltpu.VMEM((sh, LANE, LANE), F32), pltpu.VMEM((t, wide), F32)],
        compiler_params=_cparams(("parallel", "parallel", "arbitrary")),
        name="hgrn_scan",
    )(z, z, z, lb.reshape(hb, 1, wide), mlev, mincl)


def _hgrn_finish_kernel(o_ref, g_ref, w_ref, out_ref):
    w = w_ref[...]
    for h in range(C_HEADS):
        sl = slice(h * LANE, (h + 1) * LANE)
        o = o_ref[0, :, sl] + o_ref[1, :, sl]
        ms = jnp.mean(o * o, axis=-1, keepdims=True)
        out_ref[:, sl] = (o * lax.rsqrt(ms + 1e-5) * w * _silu(g_ref[:, sl])).astype(out_ref.dtype)


def hgrn_finish(o, z, norm_w):
    n = z.shape[0]
    t = ROW_TILE
    dm = C_HEADS * LANE
    return pl.pallas_call(
        _hgrn_finish_kernel,
        grid=(n // t,),
        in_specs=[pl.BlockSpec((2, t, dm), lambda i: (0, i, 0)),
                  pl.BlockSpec((t, dm), lambda i: (i, 4)),
                  pl.BlockSpec((1, LANE), lambda i: (0, 0))],
        out_specs=pl.BlockSpec((t, dm), lambda i: (i, 0)),
        out_shape=jax.ShapeDtypeStruct((n, dm), BF16),
        compiler_params=_cparams(("parallel",)),
        name="hgrn_finish",
    )(o, z, norm_w.reshape(1, LANE))


def rwkv_matrices():
    t = ROW_TILE
    ti = np.arange(t)[:, None]
    ii = np.arange(t)[None, :]
    same = (ti // RWKV_CHUNK) == (ii // RWKV_CHUNK)
    mincl = np.stack([same & (ii <= ti), same & (ii >= ti)]).astype(np.float32)
    mbef = np.stack([same & (ii < ti), same & (ii > ti)]).astype(np.float32)
    return (jnp.asarray(mincl, BF16), jnp.asarray(mbef, BF16), jnp.asarray(same.astype(np.float32), BF16))


def even_mixer(h, n_ctx, w_in, lam_params, subln_w, mu, w0, w2, a0, a2, g2, k_k, k_a, r_k, lnx_w, lnx_b,
               layer_idx, rope):
    n, d = h.shape
    n_lat = n - n_ctx
    hq = A_HEADS * 2 * A_QK_DIM
    a_in = 3 * hq
    cos_t, sin_t = rope
    q_scale = (A_QK_DIM ** -0.5) * math.log2(math.e)
    w_qk = jnp.concatenate([w_in[:, :hq] * q_scale, w_in[:, hq:2 * hq]], axis=1).astype(BF16)
    qk = matmul_rope(h, w_qk, cos_t, sin_t)
    v = matmul(h, w_in[:, 2 * hq:a_in].astype(BF16), BF16)
    b_in = w_in.shape[1] - a_in
    wz = 3 * B_WIDTH + 512
    w_b = jnp.pad(w_in[:, a_in:], ((0, 0), (0, wz - b_in))).astype(BF16)
    zb = matmul(h, w_b, F32)

    lam_f = lam_params.astype(F32)
    lam_init = 0.8 - 0.6 * math.exp(-0.3 * layer_idx)
    lam = jnp.exp(jnp.sum(lam_f[0] * lam_f[1])) - jnp.exp(jnp.sum(lam_f[2] * lam_f[3])) + lam_init
    tq = _row_tile(n_lat, (2048, 1024, 512, 256))
    tk = _row_tile(n, (1280, 640, 256))
    o_lat = diff_attention(qk, v, lam, subln_w, lam_init, tq, tk, n_lat // tq, 0, n // tk, 0)
    o_ctx = diff_attention(qk, v, lam, subln_w, lam_init, n_ctx, n_ctx, 1, n_lat // n_ctx, 1, n_lat // n_ctx)
    a_out = jnp.concatenate([o_lat, o_ctx], axis=0)

    c = B_WIDTH
    mu_p = jnp.pad(mu, (0, wz - b_in)).reshape(1, wz)
    zc = jnp.zeros((B_DECAY_RANK, c), F32)
    w2cat = jnp.concatenate([jnp.concatenate([w2[0], zc], 1), jnp.concatenate([zc, w2[1]], 1)], 0)
    a2cat = jnp.concatenate([jnp.concatenate([a2[0], zc], 1), jnp.concatenate([zc, a2[1]], 1)], 0)
    w0cat = w0.reshape(1, 2 * c)
    a0cat = a0.reshape(1, 2 * c)
    g2pad = jnp.pad(g2, ((0, 256 - B_GATE_RANK), (0, 0))).astype(BF16)
    hid = np.arange(c) // B_HEAD_DIM
    ehead = jnp.asarray((hid[:, None] == hid[None, :]).astype(np.float32), BF16)
    r, vv, kkn, g, kd, lw, bd = rwkv_prep(zb, mu_p, w2cat, w0cat, a2cat, a0cat, g2pad, k_k.reshape(1, c),
                                          k_a.reshape(1, c), ehead, n_ctx)
    mincl, mbef, msame = rwkv_matrices()
    gh, qy = rwkv_chunk(r, vv, kkn, kd, lw, bd, mincl, mbef, msame)
    y = rwkv_seq(gh, qy, n_ctx)
    b_out = rwkv_finish(y, r, vv, kd, g, r_k.reshape(1, c), lnx_w.reshape(1, c), lnx_b.reshape(1, c), ehead)
    return jnp.concatenate([a_out, b_out], axis=1)


def odd_mixer(h, n_ctx, w_in, lb, norm_w):
    z = matmul(h, w_in.astype(BF16), F32)
    mlev, mincl = hgrn_matrices()
    o = hgrn_scan(z, lb, mlev, mincl, n_ctx)
    return hgrn_finish(o, z, norm_w)


def kernel(x, c, ctx, c_ctx, ada_w, ada_b, ln_g, ln_b, even_w_in, even_w_out, diff_lambda, diff_subln_w, rwkv_mu, rwkv_w0, rwkv_w2, rwkv_a0, rwkv_a2, rwkv_g2, rwkv_k_k, rwkv_k_a, rwkv_r_k, rwkv_lnx_w, rwkv_lnx_b, odd_w_in, odd_w_out, hgrn_lb_raw, hgrn_norm_w, moe_router_w, moe_router_b, moe_w_gu, moe_w_down, moe_shared_gu, moe_shared_down):
    bsz, n_lat, d = x.shape
    n_ctx = ctx.shape[1]
    assert bsz == 1 and n_ctx % ROW_TILE == 0 and n_lat % ROW_TILE == 0
    depth = ada_w.shape[0]
    xs = jnp.concatenate([x[0], ctx[0]], axis=0)
    c8 = jnp.zeros((8, d), F32).at[0].set(c[0]).at[1].set(c_ctx)
    mods = adaln(c8, ada_w, ada_b)
    lb_soft = jax.nn.softmax(hgrn_lb_raw.astype(F32), axis=0)
    lower_bounds = jnp.cumsum(lb_soft, axis=0) - lb_soft[0]
    rope = rope_tables(n_ctx, n_lat)

    h = modulate(xs, mods[0], n_ctx)
    for l in range(depth):
        mod = mods[l]
        if l % 2 == 0:
            e = l // 2
            mix = even_mixer(h, n_ctx, even_w_in[e], diff_lambda[e], diff_subln_w[e], rwkv_mu[e], rwkv_w0[e],
                             rwkv_w2[e], rwkv_a0[e], rwkv_a2[e], rwkv_g2[e], rwkv_k_k[e], rwkv_k_a[e],
                             rwkv_r_k[e], rwkv_lnx_w[e], rwkv_lnx_b[e], l, rope)
            w_out = even_w_out[e]
        else:
            o = l // 2
            mix = odd_mixer(h, n_ctx, odd_w_in[o], lower_bounds[l].reshape(C_HEADS, C_KEY_DIM), hgrn_norm_w[o])
            w_out = odd_w_out[o]
        x1, f_in, f3 = outproj_ln(mix, w_out.astype(BF16), xs, mod, ln_g[l, 0], ln_b[l, 0], n_ctx)
        mod_next = mods[l + 1] if l + 1 < depth else None
        res = moe_layer(f_in, f3, x1, mod, ln_g[l, 1], ln_b[l, 1], moe_router_w[l], moe_router_b[l], moe_w_gu,
                        moe_w_down, l, moe_shared_gu[l], moe_shared_down[l], n_ctx, mod_next)
        if mod_next is not None:
            xs, h = res
        else:
            (xs,) = res
    return xs[:n_lat].reshape(bsz, n_lat, d)
```

```python
import functools
import math

import numpy as np
import jax
import jax.numpy as jnp
from jax import lax
from jax.experimental import pallas as pl
from jax.experimental.pallas import tpu as pltpu

F32 = jnp.float32
BF16 = jnp.bfloat16
HIGHEST = lax.Precision.HIGHEST

GRID_W = 64
A_HEADS = 8
A_QK_DIM = 64
A_V_DIM = 128
ROPE_BASE = 10000.0
B_HEADS = 16
B_HEAD_DIM = 64
B_WIDTH = 1024
B_DECAY_RANK = 64
B_ICL_RANK = 64
B_GATE_RANK = 160
RWKV_LN_EPS = 64e-5
C_HEADS = 16
C_KEY_DIM = 128
N_EXPERTS = 64
TOP_K = 6
EXPERT_FF = 512
SHARED_FF = 1024
ROUTE_SCALE = 2.5
DEPTH = 2
DEEPNORM_ALPHA = (2 * DEPTH) ** 0.25

ROW_TILE = 256
RWKV_STEP_HEADS = 8
RWKV_CHUNK = 64
EXPERT_BLOCK = 256
LANE = 128
TOK_ROWS = 16
COMBINE_TILE = 128
DMA_ISSUE_UNROLL = 4
VMEM_LIMIT = 56 * 1024 * 1024


def _cparams(sem):
    return pltpu.CompilerParams(dimension_semantics=sem, vmem_limit_bytes=VMEM_LIMIT)


def _mm(a, b):
    return jnp.dot(a.astype(BF16), b.astype(BF16), preferred_element_type=F32)


def _mm_nt(a, b):
    return lax.dot_general(a.astype(BF16), b.astype(BF16), (((1,), (1,)), ((), ())),
                           preferred_element_type=F32)


def _mm_tn(a, b):
    return lax.dot_general(a.astype(BF16), b.astype(BF16), (((0,), (0,)), ((), ())),
                           preferred_element_type=F32)


def _split3(x):
    hi = x.astype(BF16)
    r1 = x - hi.astype(F32)
    mid = r1.astype(BF16)
    lo = (r1 - mid.astype(F32)).astype(BF16)
    return hi, mid, lo


def _mm_exact_lhs(m_bf16, x):
    hi, mid, lo = _split3(x)
    d = lambda t: jnp.dot(m_bf16, t, preferred_element_type=F32)
    return d(hi) + d(mid) + d(lo)


def _mm_exact_rhs(x, m_bf16):
    hi, mid, lo = _split3(x)
    d = lambda t: jnp.dot(t, m_bf16, preferred_element_type=F32)
    return d(hi) + d(mid) + d(lo)


def _silu(x):
    return x * jax.nn.sigmoid(x)


def _softplus(x):
    return jnp.maximum(x, 0.0) + jnp.log1p(jnp.exp(-jnp.abs(x)))


def _row_tile(m, cands=(1280, 640, 512, 256)):
    for t in cands:
        if m % t == 0:
            return t
    raise ValueError(f"no row tile for {m}")


def _adaln_kernel(c_ref, w_ref, b_ref, o_ref):
    s = _silu(c_ref[...])
    o_ref[...] = _mm(s, w_ref[...]) + b_ref[...]


def adaln(c8, ada_w, ada_b):
    depth, d, n6 = ada_w.shape
    tn = 512
    return pl.pallas_call(
        _adaln_kernel,
        grid=(depth, n6 // tn),
        in_specs=[pl.BlockSpec((8, d), lambda l, j: (0, 0)),
                  pl.BlockSpec((None, d, tn), lambda l, j: (l, 0, j)),
                  pl.BlockSpec((None, 1, tn), lambda l, j: (l, 0, j))],
        out_specs=pl.BlockSpec((None, 8, tn), lambda l, j: (l, 0, j)),
        out_shape=jax.ShapeDtypeStruct((depth, 8, n6), F32),
        compiler_params=_cparams(("parallel", "parallel")),
        name="adaln",
    )(c8, ada_w, ada_b.reshape(depth, 1, n6))


def _mod_rows(mod_ref, i, n_ctx_tiles, idx, d):
    row = jnp.where(i >= n_ctx_tiles, 1, 0)
    return mod_ref[pl.ds(row, 1), idx * d:(idx + 1) * d]


def _modulate_kernel(x_ref, mod_ref, o_ref, *, n_ctx_tiles, d):
    i = pl.program_id(0)
    sh = _mod_rows(mod_ref, i, n_ctx_tiles, 0, d)
    sc = _mod_rows(mod_ref, i, n_ctx_tiles, 1, d)
    o_ref[...] = (x_ref[...] * (1.0 + sc) + sh).astype(o_ref.dtype)


def modulate(x, mod, n_ctx):
    n, d = x.shape
    return pl.pallas_call(
        functools.partial(_modulate_kernel, n_ctx_tiles=(n - n_ctx) // ROW_TILE, d=d),
        grid=(n // ROW_TILE,),
        in_specs=[pl.BlockSpec((ROW_TILE, d), lambda i: (i, 0)),
                  pl.BlockSpec(mod.shape, lambda i: (0, 0))],
        out_specs=pl.BlockSpec((ROW_TILE, d), lambda i: (i, 0)),
        out_shape=jax.ShapeDtypeStruct((n, d), BF16),
        compiler_params=_cparams(("parallel",)),
        name="modulate",
    )(x, mod)


def _mm_kernel(x_ref, w_ref, o_ref):
    o_ref[...] = jnp.dot(x_ref[...], w_ref[...], preferred_element_type=F32).astype(o_ref.dtype)


def matmul(x, w, out_dtype, tn=512):
    m, k = x.shape
    n = w.shape[1]
    tm = _row_tile(m)
    return pl.pallas_call(
        _mm_kernel,
        grid=(m // tm, n // tn),
        in_specs=[pl.BlockSpec((tm, k), lambda i, j: (i, 0)),
                  pl.BlockSpec((k, tn), lambda i, j: (0, j))],
        out_specs=pl.BlockSpec((tm, tn), lambda i, j: (i, j)),
        out_shape=jax.ShapeDtypeStruct((m, n), out_dtype),
        compiler_params=_cparams(("parallel", "parallel")),
        name="matmul",
    )(x, w)


def _mm_rope_kernel(x_ref, w_ref, cos_ref, sin_ref, o_ref, *, tn):
    acc = jnp.dot(x_ref[...], w_ref[...], preferred_element_type=F32)
    cos = cos_ref[...]
    sin = sin_ref[...]
    lane = lax.broadcasted_iota(jnp.int32, cos.shape, 1)
    first = ((lane // 16) % 2) == 0
    for j in range(tn // LANE):
        blk = acc[:, j * LANE:(j + 1) * LANE]
        partner = jnp.where(first, pltpu.roll(blk, LANE - 16, 1), pltpu.roll(blk, 16, 1))
        o_ref[:, j * LANE:(j + 1) * LANE] = (blk * cos + partner * sin).astype(o_ref.dtype)


def matmul_rope(x, w, cos_t, sin_t, tn=512):
    m, k = x.shape
    n = w.shape[1]
    tm = _row_tile(m)
    return pl.pallas_call(
        functools.partial(_mm_rope_kernel, tn=tn),
        grid=(m // tm, n // tn),
        in_specs=[pl.BlockSpec((tm, k), lambda i, j: (i, 0)),
                  pl.BlockSpec((k, tn), lambda i, j: (0, j)),
                  pl.BlockSpec((tm, LANE), lambda i, j: (i, 0)),
                  pl.BlockSpec((tm, LANE), lambda i, j: (i, 0))],
        out_specs=pl.BlockSpec((tm, tn), lambda i, j: (i, j)),
        out_shape=jax.ShapeDtypeStruct((m, n), BF16),
        compiler_params=_cparams(("parallel", "parallel")),
        name="matmul_rope",
    )(x, w, cos_t, sin_t)


def rope_tables(n_ctx, n_lat):
    t = jnp.arange(n_lat)
    row = (t // GRID_W).astype(F32)
    col = (t % GRID_W).astype(F32)
    half = A_QK_DIM // 2
    inv = ROPE_BASE ** (-jnp.arange(0, half, 2, dtype=F32) / half)
    ang_r = row[:, None] * inv
    ang_c = col[:, None] * inv
    cos64 = jnp.concatenate([jnp.cos(ang_r), jnp.cos(ang_r), jnp.cos(ang_c), jnp.cos(ang_c)], axis=-1)
    sin64 = jnp.concatenate([-jnp.sin(ang_r), jnp.sin(ang_r), -jnp.sin(ang_c), jnp.sin(ang_c)], axis=-1)
    cos_l = jnp.concatenate([cos64, cos64], axis=-1)
    sin_l = jnp.concatenate([sin64, sin64], axis=-1)
    cos_t = jnp.concatenate([cos_l, jnp.ones((n_ctx, LANE), F32)], axis=0)
    sin_t = jnp.concatenate([sin_l, jnp.zeros((n_ctx, LANE), F32)], axis=0)
    return cos_t, sin_t


ATTN_ROW_SPLITS = 2


def _attn_kernel(lam_ref, q_ref, k_ref, v_ref, w_ref, o_ref, qq_s, m_s, acc_s, *, nk, tk, out_scale):
    ki = pl.program_id(2)

    @pl.when(ki == 0)
    def _():
        q = q_ref[...]
        lane = lax.broadcasted_iota(jnp.int32, q.shape, 1)
        zero = jnp.zeros_like(q)
        qq_s[0] = jnp.where(lane < A_QK_DIM, q, zero)
        qq_s[1] = jnp.where(lane >= A_QK_DIM, q, zero)
        m_s[...] = jnp.full(m_s.shape, -jnp.inf, F32)
        acc_s[...] = jnp.zeros(acc_s.shape, F32)

    k = k_ref[...]
    v = v_ref[...]
    v_ext = jnp.concatenate([v, jnp.ones_like(v)], axis=1)
    n_col = tk // LANE
    part = qq_s.shape[1] // ATTN_ROW_SPLITS
    chains = [(mi, slice(rp * part, (rp + 1) * part)) for mi in range(2) for rp in range(ATTN_ROW_SPLITS)]

    def scores(c):
        mi, rows = chains[c]
        return lax.dot_general(qq_s[mi, rows, :], k, (((1,), (1,)), ((), ())),
                               preferred_element_type=F32).astype(BF16)

    def softmax(c, s):
        mi, rows = chains[c]
        cm = s[:, 0:LANE]
        for cc in range(1, n_col):
            cm = jnp.maximum(cm, s[:, cc * LANE:(cc + 1) * LANE])
        m_old = m_s[mi, rows, :]
        m_new = jnp.maximum(m_old, jnp.max(cm.astype(F32), axis=1, keepdims=True))
        m_s[mi, rows, :] = m_new
        mb = m_new.astype(BF16)
        return jnp.exp2(s - jnp.concatenate([mb] * n_col, axis=1)), jnp.exp2(m_old - m_new)

    def accumulate(c, p, alpha):
        mi, rows = chains[c]
        acc_s[mi, rows, :] = (jnp.concatenate([alpha, alpha], axis=1) * acc_s[mi, rows, :]
                              + jnp.dot(p, v_ext, preferred_element_type=F32))

    n_ch = len(chains)
    s, pa = {}, {}
    for step in range(n_ch + 2):
        if step < n_ch:
            s[step] = scores(step)
        if 1 <= step <= n_ch:
            pa[step - 1] = softmax(step - 1, s.pop(step - 1))
        if 2 <= step:
            accumulate(step - 2, *pa.pop(step - 2))

    @pl.when(ki == nk - 1)
    def _():
        lam = lam_ref[0]
        o0 = acc_s[0, :, :LANE] / acc_s[0, :, LANE:LANE + 1]
        o1 = acc_s[1, :, :LANE] / acc_s[1, :, LANE:LANE + 1]
        o = o0 - lam * o1
        ms = jnp.mean(o * o, axis=-1, keepdims=True)
        o_ref[...] = (o * lax.rsqrt(ms + 1e-5) * w_ref[...] * out_scale).astype(o_ref.dtype)


def diff_attention(qk, v, lam, subln_w, lam_init, tq, tk, nq_tiles, q_tile0, nk, k_tile0):
    return pl.pallas_call(
        functools.partial(_attn_kernel, nk=nk, tk=tk, out_scale=1.0 - lam_init),
        grid=(A_HEADS, nq_tiles, nk),
        in_specs=[pl.BlockSpec(memory_space=pltpu.SMEM),
                  pl.BlockSpec((tq, LANE), lambda h, i, j: (q_tile0 + i, h)),
                  pl.BlockSpec((tk, LANE), lambda h, i, j: (k_tile0 + j, A_HEADS + h)),
                  pl.BlockSpec((tk, LANE), lambda h, i, j: (k_tile0 + j, h)),
                  pl.BlockSpec((1, LANE), lambda h, i, j: (0, 0))],
        out_specs=pl.BlockSpec((tq, LANE), lambda h, i, j: (i, h)),
        out_shape=jax.ShapeDtypeStruct((nq_tiles * tq, A_HEADS * A_V_DIM), BF16),
        scratch_shapes=[pltpu.VMEM((2, tq, LANE), BF16), pltpu.VMEM((2, tq, LANE), F32),
                        pltpu.VMEM((2, tq, 2 * LANE), F32)],
        compiler_params=_cparams(("parallel", "parallel", "arbitrary")),
        name="diff_attention",
    )(lam.reshape(1), qk, qk, v, subln_w.reshape(1, LANE))


def _rwkv_prep_kernel(z_ref, zp_ref, zn_ref, mu_ref, w2_ref, w0_ref, a2_ref, a0_ref, g2_ref, kk_ref, ka_ref,
                      eh_ref, r_o, v_o, kkn_o, g_o, kd_o, lw_o, bd_o, *, n_first, n_tot):
    i = pl.program_id(0)
    z = z_ref[...]
    tm = z.shape[0]
    grow = i * tm + lax.broadcasted_iota(jnp.int32, (tm, 1), 0)
    lrow = lax.broadcasted_iota(jnp.int32, (tm, 1), 0)
    prev = jnp.where(lrow == 0, zp_ref[7:8, :], pltpu.roll(z, 1, 0))
    nxt = jnp.where(lrow == tm - 1, zn_ref[0:1, :], pltpu.roll(z, tm - 1, 0))
    has_prev = jnp.logical_and(grow != 0, grow != n_first)
    has_next = jnp.logical_and(grow != n_first - 1, grow != n_tot - 1)
    prev = jnp.where(has_prev, prev, 0.0)
    nxt = jnp.where(has_next, nxt, 0.0)
    zm = z + (0.5 * (prev + nxt) - z) * mu_ref[...]
    c = B_WIDTH
    r = zm[:, 0:c]
    k = zm[:, c:2 * c]
    v = zm[:, 2 * c:3 * c]
    wd = zm[:, 3 * c:3 * c + 128]
    ad = zm[:, 3 * c + 128:3 * c + 256]
    gd = zm[:, 3 * c + 256:3 * c + 512]
    wl = w0_ref[...] + jnp.dot(jnp.tanh(wd), w2_ref[...], precision=HIGHEST, preferred_element_type=F32)
    al = a0_ref[...] + jnp.dot(ad, a2_ref[...], precision=HIGHEST, preferred_element_type=F32)
    g = _mm(jax.nn.sigmoid(gd), g2_ref[...])
    kk = k * kk_ref[...]
    ss = _mm_exact_rhs(kk * kk, eh_ref[...])
    kkn = kk / jnp.maximum(jnp.sqrt(ss), 1e-12)
    r_o[...] = r
    v_o[...] = v
    kkn_o[...] = kkn
    g_o[...] = g
    for d in range(2):
        w_log = -_softplus(-wl[:, d * c:(d + 1) * c]) - 0.5
        lw_o[d] = -jnp.exp(w_log)
        icl = jax.nn.sigmoid(al[:, d * c:(d + 1) * c])
        kd_o[d] = k * (1.0 + (icl - 1.0) * ka_ref[...])
        bd_o[d] = kkn * icl


def rwkv_prep(zb, mu, w2cat, w0cat, a2cat, a0cat, g2pad, k_k, k_a, ehead, n_ctx):
    n, wz = zb.shape
    tm = ROW_TILE
    nt = n // tm
    c = B_WIDTH
    full = lambda a: pl.BlockSpec(a.shape, lambda i: (0,) * a.ndim)
    tok = pl.BlockSpec((tm, c), lambda i: (i, 0))
    tok2 = pl.BlockSpec((2, tm, c), lambda i: (0, i, 0))
    sd = jax.ShapeDtypeStruct
    return pl.pallas_call(
        functools.partial(_rwkv_prep_kernel, n_first=n - n_ctx, n_tot=n),
        grid=(nt,),
        in_specs=[pl.BlockSpec((tm, wz), lambda i: (i, 0)),
                  pl.BlockSpec((8, wz), lambda i: (jnp.maximum(i * (tm // 8) - 1, 0), 0)),
                  pl.BlockSpec((8, wz), lambda i: (jnp.minimum((i + 1) * (tm // 8), n // 8 - 1), 0)),
                  full(mu), full(w2cat), full(w0cat), full(a2cat), full(a0cat), full(g2pad),
                  full(k_k), full(k_a), full(ehead)],
        out_specs=[tok, tok, tok, tok, tok2, tok2, tok2],
        out_shape=[sd((n, c), F32)] * 4 + [sd((2, n, c), F32)] * 3,
        compiler_params=_cparams(("parallel",)),
        name="rwkv_prep",
    )(zb, zb, zb, mu, w2cat, w0cat, a2cat, a0cat, g2pad, k_k, k_a, ehead)


def _rwkv_chunk_kernel(r_ref, v_ref, kk_ref, kd_ref, lw_ref, bd_ref, mincl_ref, mbef_ref, msame_ref,
                       gh_ref, qy_ref):
    lw = lw_ref[...]
    mincl = mincl_ref[...]
    cum_i = _mm_exact_lhs(mincl, lw)
    cum_e = cum_i - lw
    cum_l = _mm_exact_lhs(msame_ref[...], lw)
    r = r_ref[...]
    v = v_ref[...]
    a = -kk_ref[...]
    k = kd_ref[...]
    b = bd_ref[...]
    at = a * jnp.exp(cum_e)
    rt = r * jnp.exp(cum_i)
    einv = jnp.exp(-cum_i)
    bt = b * einv
    kt = k * einv
    e_l = jnp.exp(cum_l - cum_i)
    bh = b * e_l
    kh = k * e_l
    p_l = jnp.exp(cum_l)
    incl = mincl > 0.5
    bef = mbef_ref[...] > 0.5
    hd = B_HEAD_DIM
    nch = ROW_TILE // RWKV_CHUNK
    zeros_h = jnp.zeros((ROW_TILE, hd), F32)
    eye = (lax.broadcasted_iota(jnp.int32, (RWKV_CHUNK, LANE), 0)
           == lax.broadcasted_iota(jnp.int32, (RWKV_CHUNK, LANE), 1))
    ri = lax.broadcasted_iota(jnp.int32, (ROW_TILE, ROW_TILE), 0)
    ci = lax.broadcasted_iota(jnp.int32, (ROW_TILE, ROW_TILE), 1)
    xr = ri ^ ci
    eye_t = jnp.where(ri == ci, 1.0, 0.0)
    heads = range(RWKV_STEP_HEADS)
    sls = [slice(h * hd, (h + 1) * hd) for h in heads]
    a_ab = [jnp.where(bef, _mm_nt(at[:, sl], bt[:, sl]), 0.0) for sl in sls]
    a_ak = [jnp.where(bef, _mm_nt(at[:, sl], kt[:, sl]), 0.0) for sl in sls]
    m_rb = [jnp.where(incl, _mm_nt(rt[:, sl], bt[:, sl]), 0.0) for sl in sls]
    m_rk = [jnp.where(incl, _mm_nt(rt[:, sl], kt[:, sl]), 0.0) for sl in sls]
    lev0 = xr == 1
    t_inv = [eye_t + jnp.where(lev0, a_ab[h], 0.0) for h in heads]
    for sh in range(1, 6):
        lev = (xr >> sh) == 1
        inner = [_mm(jnp.where(lev, a_ab[h], 0.0), t_inv[h]) for h in heads]
        t_inv = [t_inv[h] + _mm(t_inv[h], inner[h]) for h in heads]
    x0 = [jnp.concatenate([at[:, sls[h]], _mm(a_ak[h], v[:, sls[h]])], axis=1) for h in heads]
    x = [_mm(t_inv[h], x0[h]) for h in heads]
    v0 = [jnp.concatenate([zeros_h, v[:, sls[h]]], axis=1) for h in heads]
    for h in heads:
        qy_ref[h] = (jnp.concatenate([rt[:, sls[h]], zeros_h], axis=1) + _mm(m_rb[h], x[h])
                     + _mm(m_rk[h], v0[h]))
    for c in range(nch):
        rows = slice(c * RWKV_CHUNK, (c + 1) * RWKV_CHUNK)
        for h in heads:
            sl = sls[h]
            gh = _mm_tn(bh[rows, sl], x[h][rows]) + _mm_tn(kh[rows, sl], v0[h][rows])
            plr = jnp.concatenate([p_l[c * RWKV_CHUNK:c * RWKV_CHUNK + 1, sl], jnp.zeros((1, hd), F32)], axis=1)
            gh_ref[h, rows, :] = gh + jnp.where(eye, jnp.broadcast_to(plr, (RWKV_CHUNK, LANE)), 0.0)


def rwkv_chunk(r, v, kkn, kd, lw, bd, mincl, mbef, msame):
    n, c = r.shape
    tm = ROW_TILE
    wide = RWKV_STEP_HEADS * B_HEAD_DIM
    tokb = pl.BlockSpec((tm, wide), lambda d, s, p: (s, p))
    tokd = pl.BlockSpec((None, tm, wide), lambda d, s, p: (d, s, p))
    mdir = pl.BlockSpec((None, tm, tm), lambda d, s, p: (d, 0, 0))
    outb = pl.BlockSpec((None, RWKV_STEP_HEADS, tm, LANE), lambda d, s, p: (d, p, s, 0))
    sd = jax.ShapeDtypeStruct((2, B_HEADS, n, LANE), F32)
    return pl.pallas_call(
        _rwkv_chunk_kernel,
        grid=(2, n // tm, B_HEADS // RWKV_STEP_HEADS),
        in_specs=[tokb, tokb, tokb, tokd, tokd, tokd, mdir, mdir,
                  pl.BlockSpec((tm, tm), lambda d, s, p: (0, 0))],
        out_specs=[outb, outb],
        out_shape=[sd, sd],
        compiler_params=_cparams(("parallel", "parallel", "parallel")),
        name="rwkv_chunk",
    )(r, v, kkn, kd, lw, bd, mincl, mbef, msame)


def _scan_tile(d, j, nt, n_ctx_tiles):
    n_lat_tiles = nt - n_ctx_tiles
    fwd = jnp.where(j < n_ctx_tiles, n_lat_tiles + j, j - n_ctx_tiles)
    return jnp.where(d == 0, fwd, nt - 1 - j)


def _rwkv_seq_kernel(gh_ref, qy_ref, y_ref, z_s):
    d = pl.program_id(0)
    j = pl.program_id(1)

    @pl.when(j == 0)
    def _():
        z_s[...] = jnp.zeros(z_s.shape, F32)

    hd = B_HEAD_DIM
    nch = ROW_TILE // RWKV_CHUNK
    for ci in range(nch):
        c = jnp.where(d == 0, ci, nch - 1 - ci)
        off = pl.multiple_of(c * RWKV_CHUNK, RWKV_CHUNK)
        ys = []
        for h in range(B_HEADS):
            z = z_s[h]
            gh = gh_ref[h, pl.ds(off, RWKV_CHUNK), :]
            qy = qy_ref[h, pl.ds(off, RWKV_CHUNK), :]
            ys.append(_mm(qy[:, :hd], z) + qy[:, hd:])
            z_s[h] = _mm(gh[:, :hd], z) + gh[:, hd:]
        y_ref[pl.ds(off, RWKV_CHUNK), :] = jnp.concatenate(ys, axis=1)


def rwkv_seq(gh, qy, n_ctx):
    _, hh, n, _ = gh.shape
    tm = ROW_TILE
    nt = n // tm
    nct = n_ctx // tm
    inb = pl.BlockSpec((None, hh, tm, LANE), lambda d, j: (d, 0, _scan_tile(d, j, nt, nct), 0))
    return pl.pallas_call(
        _rwkv_seq_kernel,
        grid=(2, nt),
        in_specs=[inb, inb],
        out_specs=pl.BlockSpec((None, tm, B_WIDTH), lambda d, j: (d, _scan_tile(d, j, nt, nct), 0)),
        out_shape=jax.ShapeDtypeStruct((2, n, B_WIDTH), F32),
        scratch_shapes=[pltpu.VMEM((hh, B_HEAD_DIM, B_HEAD_DIM), F32)],
        compiler_params=_cparams(("parallel", "arbitrary")),
        name="rwkv_seq",
    )(gh, qy)


def _rwkv_finish_kernel(y_ref, r_ref, v_ref, kd_ref, g_ref, rk_ref, lw_ref, lb_ref, eh_ref, o_ref):
    y = y_ref[0] + y_ref[1]
    eh = eh_ref[...]
    inv = 1.0 / B_HEAD_DIM
    mu = _mm_exact_rhs(y, eh) * inv
    yc = y - mu
    var = _mm_exact_rhs(yc * yc, eh) * inv
    yn = yc * lax.rsqrt(var + RWKV_LN_EPS) * lw_ref[...] + lb_ref[...]
    r = r_ref[...]
    rk = (r * kd_ref[0] + r * kd_ref[1]) * rk_ref[...]
    bonus = _mm_exact_rhs(rk, eh) * v_ref[...]
    o_ref[...] = ((yn + bonus) * g_ref[...]).astype(o_ref.dtype)


def rwkv_finish(y, r, v, kd, g, r_k, lnx_w, lnx_b, ehead):
    n, c = r.shape
    tm = ROW_TILE
    tok = pl.BlockSpec((tm, c), lambda i: (i, 0))
    tok2 = pl.BlockSpec((2, tm, c), lambda i: (0, i, 0))
    full = lambda a: pl.BlockSpec(a.shape, lambda i: (0,) * a.ndim)
    return pl.pallas_call(
        _rwkv_finish_kernel,
        grid=(n // tm,),
        in_specs=[tok2, tok, tok, tok2, tok, full(r_k), full(lnx_w), full(lnx_b), full(ehead)],
        out_specs=tok,
        out_shape=jax.ShapeDtypeStruct((n, c), BF16),
        compiler_params=_cparams(("parallel",)),
        name="rwkv_finish",
    )(y, r, v, kd, g, r_k, lnx_w, lnx_b, ehead)


def _layer_norm(x, g, b):
    mu = jnp.mean(x, axis=-1, keepdims=True)
    xc = x - mu
    var = jnp.mean(xc * xc, axis=-1, keepdims=True)
    return xc * lax.rsqrt(var + 1e-5) * g + b


def _store_token_rows(ref, val):
    tm = val.shape[0]
    for s in range(TOK_ROWS):
        ref[pl.ds(s, tm, stride=TOK_ROWS), :] = val[:, s * LANE:(s + 1) * LANE]


def _load_token_rows(ref, tm):
    return jnp.concatenate([ref[pl.ds(s, tm, stride=TOK_ROWS), :] for s in range(TOK_ROWS)], axis=1)


def _outproj_kernel(a_ref, w_ref, x_ref, mod_ref, g_ref, b_ref, x1_ref, f_ref, f3_ref, *, n_ctx_tiles, d):
    i = pl.program_id(0)
    m = jnp.dot(a_ref[...], w_ref[...], preferred_element_type=F32)
    gate = _mod_rows(mod_ref, i, n_ctx_tiles, 2, d)
    x1 = _layer_norm(DEEPNORM_ALPHA * x_ref[...] + gate * m, g_ref[...], b_ref[...])
    x1_ref[...] = x1
    sh = _mod_rows(mod_ref, i, n_ctx_tiles, 3, d)
    sc = _mod_rows(mod_ref, i, n_ctx_tiles, 4, d)
    f = x1 * (1.0 + sc) + sh
    f_ref[...] = f
    _store_token_rows(f3_ref, f)


def outproj_ln(a, w_out, x, mod, ln_g, ln_b, n_ctx):
    n, kdim = a.shape
    d = x.shape[1]
    tm = ROW_TILE
    tok = pl.BlockSpec((tm, d), lambda i: (i, 0))
    vec = pl.BlockSpec((1, d), lambda i: (0, 0))
    return pl.pallas_call(
        functools.partial(_outproj_kernel, n_ctx_tiles=(n - n_ctx) // tm, d=d),
        grid=(n // tm,),
        in_specs=[pl.BlockSpec((tm, kdim), lambda i: (i, 0)),
                  pl.BlockSpec((kdim, d), lambda i: (0, 0)),
                  tok, pl.BlockSpec(mod.shape, lambda i: (0, 0)), vec, vec],
        out_specs=[tok, tok, pl.BlockSpec((tm * TOK_ROWS, LANE), lambda i: (i, 0))],
        out_shape=[jax.ShapeDtypeStruct((n, d), F32)] * 2 + [jax.ShapeDtypeStruct((n * TOK_ROWS, LANE), F32)],
        compiler_params=_cparams(("parallel",)),
        name="outproj_ln",
    )(a, w_out, x, mod, ln_g.reshape(1, d), ln_b.reshape(1, d))


PICK_IDX, PICK_RANK, PICK_GATE = 0, 8, 16


def _router_kernel(f_ref, wr_hi_ref, wr_lo_ref, br_ref, wsg_ref, wsd_ref, ltri_ref, pick_ref, sh_ref, cnt_ref,
                   carry_s):
    i = pl.program_id(0)

    @pl.when(i == 0)
    def _():
        carry_s[...] = jnp.zeros(carry_s.shape, F32)

    f = f_ref[...]
    f_hi = f.astype(BF16)
    f_lo = (f - f_hi.astype(F32)).astype(BF16)
    dd = lambda a, b: jnp.dot(a, b[...], preferred_element_type=F32)
    logits = dd(f_hi, wr_hi_ref) + dd(f_lo, wr_hi_ref) + dd(f_hi, wr_lo_ref)
    scores = jax.nn.sigmoid(logits)
    lane = lax.broadcasted_iota(jnp.int32, scores.shape, 1).astype(F32)
    neg = jnp.float32(-jnp.inf)
    sel = jnp.where(lane < N_EXPERTS, scores + br_ref[...], neg)
    top = jnp.zeros(scores.shape, F32)
    chosen = jnp.zeros(scores.shape, F32)
    hits, firsts = [], []
    for j in range(TOP_K):
        mx = jnp.max(sel, axis=1, keepdims=True)
        first = jnp.min(jnp.where(sel == mx, lane, float(LANE)), axis=1, keepdims=True)
        hit = lane == first
        top = jnp.where(hit, scores, top)
        chosen = jnp.where(hit, 1.0, chosen)
        sel = jnp.where(hit, neg, sel)
        hits.append(hit)
        firsts.append(first)
    gates = top / jnp.sum(top, axis=1, keepdims=True) * ROUTE_SCALE
    carry = carry_s[0:1, :]
    rank = dd(ltri_ref[...], chosen.astype(BF16)) + carry
    pick = jnp.zeros(scores.shape, F32)
    for j in range(TOP_K):
        pick = jnp.where(lane == PICK_IDX + j, firsts[j], pick)
        rank_j = jnp.sum(jnp.where(hits[j], rank, 0.0), axis=1, keepdims=True)
        pick = jnp.where(lane == PICK_RANK + j, rank_j, pick)
        gate_j = jnp.sum(jnp.where(hits[j], gates, 0.0), axis=1, keepdims=True)
        pick = jnp.where(lane == PICK_GATE + j, gate_j, pick)
    pick_ref[...] = pick
    carry_s[0:1, :] = carry + jnp.sum(chosen, axis=0, keepdims=True)
    cnt_ref[...] = carry_s[...]
    ab = dd(f_hi, wsg_ref)
    hdn = _silu(ab[:, :SHARED_FF]) * ab[:, SHARED_FF:]
    sh_ref[...] = dd(hdn.astype(BF16), wsd_ref)


def router_shared(f_in, wr_hi, wr_lo, br, ws_gu, ws_down):
    n, d = f_in.shape
    tm = ROW_TILE
    full = lambda a: pl.BlockSpec(a.shape, lambda i: (0,) * a.ndim)
    sd = jax.ShapeDtypeStruct
    ltri = jnp.asarray(np.tril(np.ones((tm, tm), np.float32), -1), BF16)
    return pl.pallas_call(
        _router_kernel,
        grid=(n // tm,),
        in_specs=[pl.BlockSpec((tm, d), lambda i: (i, 0)), full(wr_hi), full(wr_lo), full(br),
                  full(ws_gu), full(ws_down), full(ltri)],
        out_specs=[pl.BlockSpec((tm, LANE), lambda i: (i, 0)), pl.BlockSpec((tm, d), lambda i: (i, 0)),
                   pl.BlockSpec((8, LANE), lambda i: (0, 0))],
        out_shape=[sd((n, LANE), F32), sd((n, d), F32), sd((8, LANE), F32)],
        scratch_shapes=[pltpu.VMEM((8, LANE), F32)],
        compiler_params=_cparams(("arbitrary",)),
        name="router_shared",
    )(f_in, wr_hi, wr_lo, br, ws_gu, ws_down, ltri)


def _slab(ref, row):
    return ref.at[pl.ds(pl.multiple_of(row * TOK_ROWS, TOK_ROWS), TOK_ROWS)]


def _dispatch_kernel(slots_ref, padlo_ref, padhi_ref, f3_ref, xs_hbm, zrow, sem, *, tm, n_tiles):
    i = pl.program_id(0)

    def body(t, carry):
        for j in range(TOP_K):
            slot = slots_ref[(i * tm + t) * TOP_K + j]
            pltpu.make_async_copy(_slab(f3_ref, t), _slab(xs_hbm, slot), sem.at[0]).start()
        return carry
    lax.fori_loop(0, tm, body, 0, unroll=DMA_ISSUE_UNROLL)
    for _ in range(TOP_K):
        pltpu.make_async_copy(f3_ref, xs_hbm.at[pl.ds(0, tm * TOK_ROWS)], sem.at[0]).wait()

    @pl.when(i == n_tiles - 1)
    def _():
        zrow[...] = jnp.zeros(zrow.shape, F32)

        def per_expert(e, carry):
            lo = padlo_ref[e]
            hi = padhi_ref[e]

            def fill(s, c):
                pltpu.make_async_copy(zrow, _slab(xs_hbm, s), sem.at[1]).start()
                return c
            lax.fori_loop(lo, hi, fill, 0)

            def drain(s, c):
                pltpu.make_async_copy(zrow, _slab(xs_hbm, 0), sem.at[1]).wait()
                return c
            lax.fori_loop(lo, hi, drain, 0)
            return carry
        lax.fori_loop(0, N_EXPERTS + 1, per_expert, 0)


def moe_dispatch_rows(f3, tok_slots, pad_lo, pad_hi, n_slots):
    tm = ROW_TILE
    n = f3.shape[0] // TOK_ROWS
    grid_spec = pltpu.PrefetchScalarGridSpec(
        num_scalar_prefetch=3,
        grid=(n // tm,),
        in_specs=[pl.BlockSpec((tm * TOK_ROWS, LANE), lambda i, sl, lo, hi: (i, 0))],
        out_specs=pl.BlockSpec(memory_space=pl.ANY),
        scratch_shapes=[pltpu.VMEM((TOK_ROWS, LANE), F32), pltpu.SemaphoreType.DMA((2,))],
    )
    return pl.pallas_call(
        functools.partial(_dispatch_kernel, tm=tm, n_tiles=n // tm),
        grid_spec=grid_spec,
        out_shape=jax.ShapeDtypeStruct((n_slots * TOK_ROWS, LANE), F32),
        compiler_params=_cparams(("arbitrary",)),
        name="moe_dispatch",
    )(tok_slots, pad_lo, pad_hi, f3)


def _expert_kernel(be_ref, nu_ref, xs_ref, wgu_ref, wdn_ref, ys_ref, wgu_bf, wdn_bf):
    i = pl.program_id(0)

    @pl.when(jnp.logical_or(i == 0, be_ref[i] != be_ref[jnp.maximum(i - 1, 0)]))
    def _():
        wgu_bf[...] = wgu_ref[...].astype(BF16)
        wdn_bf[...] = wdn_ref[...].astype(BF16)

    @pl.when(i < nu_ref[0])
    def _():
        x = _load_token_rows(xs_ref, EXPERT_BLOCK).astype(BF16)
        ab = jnp.dot(x, wgu_bf[...], preferred_element_type=F32)
        hdn = _silu(ab[:, :EXPERT_FF]) * ab[:, EXPERT_FF:]
        y = jnp.dot(hdn.astype(BF16), wdn_bf[...], preferred_element_type=F32)
        _store_token_rows(ys_ref, y)

    @pl.when(i >= nu_ref[0])
    def _():
        ys_ref[...] = jnp.zeros(ys_ref.shape, F32)


def routed_experts(xs, block_expert, n_used, w_gu_all, w_down_all, layer):
    d = w_gu_all.shape[2]
    n_blocks = block_expert.shape[0]
    rows = EXPERT_BLOCK * TOK_ROWS
    last_used = lambda i, nu: jnp.minimum(i, jnp.maximum(nu[0] - 1, 0))
    grid_spec = pltpu.PrefetchScalarGridSpec(
        num_scalar_prefetch=2,
        grid=(n_blocks,),
        in_specs=[pl.BlockSpec((rows, LANE), lambda i, be, nu: (last_used(i, nu), 0)),
                  pl.BlockSpec((None, None, d, 2 * EXPERT_FF), lambda i, be, nu: (layer, be[i], 0, 0)),
                  pl.BlockSpec((None, None, EXPERT_FF, d), lambda i, be, nu: (layer, be[i], 0, 0))],
        out_specs=pl.BlockSpec((rows, LANE), lambda i, be, nu: (i, 0)),
        scratch_shapes=[pltpu.VMEM((d, 2 * EXPERT_FF), BF16), pltpu.VMEM((EXPERT_FF, d), BF16)],
    )
    return pl.pallas_call(
        _expert_kernel,
        grid_spec=grid_spec,
        out_shape=jax.ShapeDtypeStruct((n_blocks * rows, LANE), F32),
        compiler_params=_cparams(("arbitrary",)),
        name="routed_experts",
    )(block_expert, n_used, xs, w_gu_all, w_down_all)


def moe_slots(pick, counts, n_blocks):
    eb = EXPERT_BLOCK
    idx6 = pick[:, PICK_IDX:PICK_IDX + TOP_K].astype(jnp.int32)
    rank6 = pick[:, PICK_RANK:PICK_RANK + TOP_K].astype(jnp.int32)
    padded = (counts + eb - 1) // eb * eb
    pad_end = jnp.cumsum(padded)
    pad_start = pad_end - padded
    experts = jnp.arange(N_EXPERTS, dtype=jnp.int32)
    start6 = jnp.sum(jnp.where(idx6[:, :, None] == experts, pad_start, 0), axis=-1)
    tok_slots = (start6 + rank6).reshape(-1).astype(jnp.int32)
    blk_start = jnp.arange(n_blocks, dtype=jnp.int32) * eb
    block_expert = jnp.minimum(jnp.sum(pad_end[None, :] <= blk_start[:, None], axis=1), N_EXPERTS - 1)
    block_expert = block_expert.astype(jnp.int32)
    n_used = (pad_end[-1:] // eb).astype(jnp.int32)
    fill_lo = jnp.concatenate([pad_start + counts, pad_end[-1:]]).astype(jnp.int32)
    fill_hi = jnp.concatenate([pad_end, jnp.full((1,), n_blocks * eb, pad_end.dtype)]).astype(jnp.int32)
    return tok_slots, block_expert, n_used, fill_lo, fill_hi


def _combine_kernel(slots_ref, ys_hbm, pick_ref, sh_ref, x1_ref, mod_ref, g_ref, b_ref, *rest,
                    tm, n_tiles, n_ctx_tiles, d, with_next):
    if with_next:
        modn_ref, x2_ref, h_ref, buf, sem = rest
    else:
        x2_ref, buf, sem = rest
    i = pl.program_id(0)
    s = i % 2
    tok_rows = TOP_K * TOK_ROWS

    def start_gather(tile, slot):
        def body(t, carry):
            for j in range(TOP_K):
                src = slots_ref[(tile * tm + t) * TOP_K + j]
                pltpu.make_async_copy(_slab(ys_hbm, src), _slab(buf.at[slot], t * TOP_K + j), sem.at[slot]).start()
            return carry
        lax.fori_loop(0, tm, body, 0, unroll=DMA_ISSUE_UNROLL)

    @pl.when(i == 0)
    def _():
        start_gather(0, 0)

    @pl.when(i + 1 < n_tiles)
    def _():
        start_gather(i + 1, 1 - s)

    pltpu.make_async_copy(ys_hbm.at[pl.ds(0, tm * tok_rows)], buf.at[s], sem.at[s]).wait()
    pieces = []
    for c in range(TOK_ROWS):
        acc = sh_ref[:, c * LANE:(c + 1) * LANE]
        for j in range(TOP_K):
            gate_j = pick_ref[:, PICK_GATE + j:PICK_GATE + j + 1]
            acc = acc + gate_j * buf[s, pl.ds(j * TOK_ROWS + c, tm, stride=tok_rows), :]
        pieces.append(acc)
    f = jnp.concatenate(pieces, axis=1)
    gate = _mod_rows(mod_ref, i, n_ctx_tiles, 5, d)
    x2 = _layer_norm(DEEPNORM_ALPHA * x1_ref[...] + gate * f, g_ref[...], b_ref[...])
    x2_ref[...] = x2
    if with_next:
        sh = _mod_rows(modn_ref, i, n_ctx_tiles, 0, d)
        sc = _mod_rows(modn_ref, i, n_ctx_tiles, 1, d)
        h_ref[...] = (x2 * (1.0 + sc) + sh).astype(h_ref.dtype)


def moe_combine(ys, tok_slots, pick, shared, x1, mod, ln_g, ln_b, n_ctx, mod_next=None):
    n, d = x1.shape
    tm = COMBINE_TILE
    tok = pl.BlockSpec((tm, d), lambda i, sl: (i, 0))
    vec = pl.BlockSpec((1, d), lambda i, sl: (0, 0))
    with_next = mod_next is not None
    in_specs = [pl.BlockSpec(memory_space=pl.ANY), pl.BlockSpec((tm, LANE), lambda i, sl: (i, 0)), tok, tok,
                pl.BlockSpec(mod.shape, lambda i, sl: (0, 0)), vec, vec]
    args = [ys, pick, shared, x1, mod, ln_g.reshape(1, d), ln_b.reshape(1, d)]
    out_specs = [tok]
    out_shape = [jax.ShapeDtypeStruct((n, d), F32)]
    if with_next:
        in_specs.append(pl.BlockSpec(mod_next.shape, lambda i, sl: (0, 0)))
        args.append(mod_next)
        out_specs.append(tok)
        out_shape.append(jax.ShapeDtypeStruct((n, d), BF16))
    grid_spec = pltpu.PrefetchScalarGridSpec(
        num_scalar_prefetch=1,
        grid=(n // tm,),
        in_specs=in_specs, out_specs=out_specs,
        scratch_shapes=[pltpu.VMEM((2, tm * TOP_K * TOK_ROWS, LANE), F32), pltpu.SemaphoreType.DMA((2,))],
    )
    return pl.pallas_call(
        functools.partial(_combine_kernel, tm=tm, n_tiles=n // tm, n_ctx_tiles=(n - n_ctx) // tm, d=d,
                          with_next=with_next),
        grid_spec=grid_spec,
        out_shape=out_shape,
        compiler_params=_cparams(("arbitrary",)),
        name="moe_combine",
    )(tok_slots, *args)


def moe_layer(f_in, f3, x1, mod, ln_g, ln_b, w_router, b_router, w_gu_all, w_down_all, layer, ws_gu, ws_down,
              n_ctx, mod_next=None):
    n, d = f_in.shape
    wr = jnp.pad(w_router, ((0, 0), (0, LANE - N_EXPERTS)))
    wr_hi = wr.astype(BF16)
    wr_lo = (wr - wr_hi.astype(F32)).astype(BF16)
    br = jnp.pad(b_router.astype(F32), (0, LANE - N_EXPERTS)).reshape(1, LANE)
    pick, shared, cnt = router_shared(f_in, wr_hi, wr_lo, br, ws_gu.astype(BF16), ws_down.astype(BF16))
    counts = cnt[0, :N_EXPERTS].astype(jnp.int32)
    n_blocks = -(-(n * TOP_K + N_EXPERTS * (EXPERT_BLOCK - 1)) // EXPERT_BLOCK)
    tok_slots, block_expert, n_used, pad_lo, pad_hi = moe_slots(pick, counts, n_blocks)
    xs = moe_dispatch_rows(f3, tok_slots, pad_lo, pad_hi, n_blocks * EXPERT_BLOCK)
    ys = routed_experts(xs, block_expert, n_used, w_gu_all, w_down_all, layer)
    return moe_combine(ys, tok_slots, pick, shared, x1, mod, ln_g, ln_b, n_ctx, mod_next)


HGRN_SMALL_LEVELS = (4, 2)
HGRN_STEP_HEADS = 4


def _hgrn_levels():
    t = ROW_TILE
    levels = []
    b = t // 2
    while b >= 1:
        levels.append(b)
        b //= 2
    return levels


def _hgrn_kernel(q_ref, f_ref, i_ref, lb_ref, mlev_ref, mincl_ref, o_ref, st_s, cum_s):
    d = pl.program_id(0)
    j = pl.program_id(2)

    @pl.when(j == 0)
    def _():
        st_s[...] = jnp.zeros(st_s.shape, F32)

    t = ROW_TILE
    lb = lb_ref[...]
    log_lb = jnp.log(lb)
    log_rest = jnp.log1p(-lb)
    fz = f_ref[...]
    q = _silu(q_ref[...])
    v = i_ref[...]
    log_sig = jnp.minimum(fz, 0.0) - jnp.log1p(jnp.exp(-jnp.abs(fz)))
    u = log_rest + log_sig
    mx = jnp.maximum(log_lb, u)
    logf = mx + jnp.log1p(jnp.exp(-jnp.abs(log_lb - u)))
    key = (1.0 - lb) * jax.nn.sigmoid(-fz)

    cum = _mm_exact_lhs(mincl_ref[...], logf)
    cum_l = jnp.where(d == 0, cum[t - 1:t, :], cum[0:1, :])
    cum_s[...] = cum
    d_small = _mm_exact_lhs(mlev_ref[...], logf)

    wide = HGRN_STEP_HEADS * LANE
    heads = [slice(h * LANE, (h + 1) * LANE) for h in range(HGRN_STEP_HEADS)]
    row = lax.broadcasted_iota(jnp.int32, (t, 1), 0)
    xr = (lax.broadcasted_iota(jnp.int32, (t, t), 0) ^ lax.broadcasted_iota(jnp.int32, (t, t), 1))
    a = [jnp.zeros((t, t), F32) for _ in heads]
    for li, b in enumerate(_hgrn_levels()):
        q_side = ((row // b) % 2) == jnp.where(d == 0, 1, 0)
        if b >= 8:
            piv = jnp.concatenate(
                [jnp.broadcast_to(cum_s[pl.ds(g * 2 * b + b - 1 + d, 1), :], (2 * b, wide))
                 for g in range(t // (2 * b))], axis=0)
            dlev = jnp.where(q_side, cum - piv, piv - cum)
        elif b > 1:
            k = HGRN_SMALL_LEVELS.index(b)
            dlev = d_small[k * t:(k + 1) * t]
        else:
            dlev = jnp.where(q_side, logf, 0.0)
        e = jnp.exp(dlev)
        qt = jnp.where(q_side, q * e, 0.0)
        kt = jnp.where(q_side, 0.0, key * e)
        same_group = (xr >> (b.bit_length())) == 0
        for h, sl in enumerate(heads):
            prod = _mm_nt(qt[:, sl], kt[:, sl])
            a[h] = a[h] + (prod if 2 * b == t else jnp.where(same_group, prod, 0.0))
    q_in = q * jnp.exp(cum)
    k_out = key * jnp.exp(cum_l - cum)
    decay_l = jnp.exp(cum_l)
    qk = q * key
    for h, sl in enumerate(heads):
        st = st_s[h]
        o_ref[:, sl] = (_mm(a[h], v[:, sl]) + _mm_nt(q_in[:, sl], st)
                        + jnp.sum(qk[:, sl], axis=1, keepdims=True) * v[:, sl])
        st_s[h] = st * decay_l[:, sl] + _mm_tn(v[:, sl], k_out[:, sl])


def hgrn_matrices():
    t = ROW_TILE
    ti = np.arange(t)[:, None]
    ii = np.arange(t)[None, :]
    mlev = np.zeros((2, len(HGRN_SMALL_LEVELS), t, t), np.float32)
    for li, b in enumerate(HGRN_SMALL_LEVELS):
        bnd = (ti // (2 * b)) * (2 * b) + b - 1
        upper = (ti // b) % 2 == 1
        mlev[0, li] = np.where(upper, (ii > bnd) & (ii <= ti), (ii > ti) & (ii <= bnd))
        mlev[1, li] = np.where(upper, (ii > bnd) & (ii < ti), (ii >= ti) & (ii <= bnd))
    mincl = np.stack([(ii <= ti), (ii >= ti)]).astype(np.float32)
    return (jnp.asarray(mlev.reshape(2, -1, t), BF16), jnp.asarray(mincl, BF16))


def hgrn_scan(z, lb, mlev, mincl, n_ctx):
    n = z.shape[0]
    t = ROW_TILE
    nt = n // t
    nct = n_ctx // t
    hh = C_HEADS
    sh = HGRN_STEP_HEADS
    wide = sh * LANE
    hb = hh // sh
    srow = lambda d, h, j: _scan_tile(d, j, nt, nct)
    return pl.pallas_call(
        _hgrn_kernel,
        grid=(2, hb, nt),
        in_specs=[pl.BlockSpec((t, wide), lambda d, h, j: (srow(d, h, j), h)),
                  pl.BlockSpec((t, wide), lambda d, h, j: (srow(d, h, j), hb * (1 + d) + h)),
                  pl.BlockSpec((t, wide), lambda d, h, j: (srow(d, h, j), 3 * hb + h)),
                  pl.BlockSpec((None, 1, wide), lambda d, h, j: (h, 0, 0)),
                  pl.BlockSpec((None,) + mlev.shape[1:], lambda d, h, j: (d, 0, 0)),
                  pl.BlockSpec((None, t, t), lambda d, h, j: (d, 0, 0))],
        out_specs=pl.BlockSpec((None, t, wide), lambda d, h, j: (d, srow(d, h, j), h)),
        out_shape=jax.ShapeDtypeStruct((2, n, hh * LANE), F32),
        scratch_shapes=[pltpu.VMEM((sh, LANE, LANE), F32), pltpu.VMEM((t, wide), F32)],
        compiler_params=_cparams(("parallel", "parallel", "arbitrary")),
        name="hgrn_scan",
    )(z, z, z, lb.reshape(hb, 1, wide), mlev, mincl)


def _hgrn_finish_kernel(o_ref, g_ref, w_ref, out_ref):
    w = w_ref[...]
    for h in range(C_HEADS):
        sl = slice(h * LANE, (h + 1) * LANE)
        o = o_ref[0, :, sl] + o_ref[1, :, sl]
        ms = jnp.mean(o * o, axis=-1, keepdims=True)
        out_ref[:, sl] = (o * lax.rsqrt(ms + 1e-5) * w * _silu(g_ref[:, sl])).astype(out_ref.dtype)


def hgrn_finish(o, z, norm_w):
    n = z.shape[0]
    t = ROW_TILE
    dm = C_HEADS * LANE
    return pl.pallas_call(
        _hgrn_finish_kernel,
        grid=(n // t,),
        in_specs=[pl.BlockSpec((2, t, dm), lambda i: (0, i, 0)),
                  pl.BlockSpec((t, dm), lambda i: (i, 4)),
                  pl.BlockSpec((1, LANE), lambda i: (0, 0))],
        out_specs=pl.BlockSpec((t, dm), lambda i: (i, 0)),
        out_shape=jax.ShapeDtypeStruct((n, dm), BF16),
        compiler_params=_cparams(("parallel",)),
        name="hgrn_finish",
    )(o, z, norm_w.reshape(1, LANE))


def rwkv_matrices():
    t = ROW_TILE
    ti = np.arange(t)[:, None]
    ii = np.arange(t)[None, :]
    same = (ti // RWKV_CHUNK) == (ii // RWKV_CHUNK)
    mincl = np.stack([same & (ii <= ti), same & (ii >= ti)]).astype(np.float32)
    mbef = np.stack([same & (ii < ti), same & (ii > ti)]).astype(np.float32)
    return (jnp.asarray(mincl, BF16), jnp.asarray(mbef, BF16), jnp.asarray(same.astype(np.float32), BF16))


def even_mixer(h, n_ctx, w_in, lam_params, subln_w, mu, w0, w2, a0, a2, g2, k_k, k_a, r_k, lnx_w, lnx_b,
               layer_idx, rope):
    n, d = h.shape
    n_lat = n - n_ctx
    hq = A_HEADS * 2 * A_QK_DIM
    a_in = 3 * hq
    cos_t, sin_t = rope
    q_scale = (A_QK_DIM ** -0.5) * math.log2(math.e)
    w_qk = jnp.concatenate([w_in[:, :hq] * q_scale, w_in[:, hq:2 * hq]], axis=1).astype(BF16)
    qk = matmul_rope(h, w_qk, cos_t, sin_t)
    v = matmul(h, w_in[:, 2 * hq:a_in].astype(BF16), BF16)
    b_in = w_in.shape[1] - a_in
    wz = 3 * B_WIDTH + 512
    w_b = jnp.pad(w_in[:, a_in:], ((0, 0), (0, wz - b_in))).astype(BF16)
    zb = matmul(h, w_b, F32)

    lam_f = lam_params.astype(F32)
    lam_init = 0.8 - 0.6 * math.exp(-0.3 * layer_idx)
    lam = jnp.exp(jnp.sum(lam_f[0] * lam_f[1])) - jnp.exp(jnp.sum(lam_f[2] * lam_f[3])) + lam_init
    tq = _row_tile(n_lat, (2048, 1024, 512, 256))
    tk = _row_tile(n, (1280, 640, 256))
    o_lat = diff_attention(qk, v, lam, subln_w, lam_init, tq, tk, n_lat // tq, 0, n // tk, 0)
    o_ctx = diff_attention(qk, v, lam, subln_w, lam_init, n_ctx, n_ctx, 1, n_lat // n_ctx, 1, n_lat // n_ctx)
    a_out = jnp.concatenate([o_lat, o_ctx], axis=0)

    c = B_WIDTH
    mu_p = jnp.pad(mu, (0, wz - b_in)).reshape(1, wz)
    zc = jnp.zeros((B_DECAY_RANK, c), F32)
    w2cat = jnp.concatenate([jnp.concatenate([w2[0], zc], 1), jnp.concatenate([zc, w2[1]], 1)], 0)
    a2cat = jnp.concatenate([jnp.concatenate([a2[0], zc], 1), jnp.concatenate([zc, a2[1]], 1)], 0)
    w0cat = w0.reshape(1, 2 * c)
    a0cat = a0.reshape(1, 2 * c)
    g2pad = jnp.pad(g2, ((0, 256 - B_GATE_RANK), (0, 0))).astype(BF16)
    hid = np.arange(c) // B_HEAD_DIM
    ehead = jnp.asarray((hid[:, None] == hid[None, :]).astype(np.float32), BF16)
    r, vv, kkn, g, kd, lw, bd = rwkv_prep(zb, mu_p, w2cat, w0cat, a2cat, a0cat, g2pad, k_k.reshape(1, c),
                                          k_a.reshape(1, c), ehead, n_ctx)
    mincl, mbef, msame = rwkv_matrices()
    gh, qy = rwkv_chunk(r, vv, kkn, kd, lw, bd, mincl, mbef, msame)
    y = rwkv_seq(gh, qy, n_ctx)
    b_out = rwkv_finish(y, r, vv, kd, g, r_k.reshape(1, c), lnx_w.reshape(1, c), lnx_b.reshape(1, c), ehead)
    return jnp.concatenate([a_out, b_out], axis=1)


def odd_mixer(h, n_ctx, w_in, lb, norm_w):
    z = matmul(h, w_in.astype(BF16), F32)
    mlev, mincl = hgrn_matrices()
    o = hgrn_scan(z, lb, mlev, mincl, n_ctx)
    return hgrn_finish(o, z, norm_w)


def kernel(x, c, ctx, c_ctx, ada_w, ada_b, ln_g, ln_b, even_w_in, even_w_out, diff_lambda, diff_subln_w, rwkv_mu, rwkv_w0, rwkv_w2, rwkv_a0, rwkv_a2, rwkv_g2, rwkv_k_k, rwkv_k_a, rwkv_r_k, rwkv_lnx_w, rwkv_lnx_b, odd_w_in, odd_w_out, hgrn_lb_raw, hgrn_norm_w, moe_router_w, moe_router_b, moe_w_gu, moe_w_down, moe_shared_gu, moe_shared_down):
    bsz, n_lat, d = x.shape
    n_ctx = ctx.shape[1]
    assert bsz == 1 and n_ctx % ROW_TILE == 0 and n_lat % ROW_TILE == 0
    depth = ada_w.shape[0]
    xs = jnp.concatenate([x[0], ctx[0]], axis=0)
    c8 = jnp.zeros((8, d), F32).at[0].set(c[0]).at[1].set(c_ctx)
    mods = adaln(c8, ada_w, ada_b)
    lb_soft = jax.nn.softmax(hgrn_lb_raw.astype(F32), axis=0)
    lower_bounds = jnp.cumsum(lb_soft, axis=0) - lb_soft[0]
    rope = rope_tables(n_ctx, n_lat)

    h = modulate(xs, mods[0], n_ctx)
    for l in range(depth):
        mod = mods[l]
        if l % 2 == 0:
            e = l // 2
            mix = even_mixer(h, n_ctx, even_w_in[e], diff_lambda[e], diff_subln_w[e], rwkv_mu[e], rwkv_w0[e],
                             rwkv_w2[e], rwkv_a0[e], rwkv_a2[e], rwkv_g2[e], rwkv_k_k[e], rwkv_k_a[e],
                             rwkv_r_k[e], rwkv_lnx_w[e], rwkv_lnx_b[e], l, rope)
            w_out = even_w_out[e]
        else:
            o = l // 2
            mix = odd_mixer(h, n_ctx, odd_w_in[o], lower_bounds[l].reshape(C_HEADS, C_KEY_DIM), hgrn_norm_w[o])
            w_out = odd_w_out[o]
        x1, f_in, f3 = outproj_ln(mix, w_out.astype(BF16), xs, mod, ln_g[l, 0], ln_b[l, 0], n_ctx)
        mod_next = mods[l + 1] if l + 1 < depth else None
        res = moe_layer(f_in, f3, x1, mod, ln_g[l, 1], ln_b[l, 1], moe_router_w[l], moe_router_b[l], moe_w_gu,
                        moe_w_down, l, moe_shared_gu[l], moe_shared_down[l], n_ctx, mod_next)
        if mod_next is not None:
            xs, h = res
        else:
            (xs,) = res
    return xs[:n_lat].reshape(bsz, n_lat, d)
```
